```python
import jax, jax.numpy as jnp
from jax import lax
import numpy as np


D_MODEL = 2048
BATCH = 4
SEQ = 4096
DEPTH = 2

MEM_LEN = 256
BRANCH_W = D_MODEL // 2
N_BRANCH = 4
MOBA_HEADS = 8
MOBA_HEAD_DIM = BRANCH_W // MOBA_HEADS
MOBA_BLOCK = 256
MOBA_TOPK = 3
MOBA_Q_CHUNK = 16
CONV_WIDTH = 3
DSA_HEADS = 8
DSA_HEAD_DIM = BRANCH_W // DSA_HEADS
IDX_HEADS = 4
IDX_DIM = 64
DSA_TOPK_MAX = 256
DSA_Q_CHUNK = 64
MEM_HEADS = 4
MEM_HEAD_DIM = BRANCH_W // MEM_HEADS
EPS = 1e-6

SPLIT_SIZES = (BRANCH_W,) * 14 + (IDX_HEADS * IDX_DIM, IDX_DIM, IDX_HEADS) + (D_MODEL,) * N_BRANCH
IN_COLS = 14 * BRANCH_W + IDX_HEADS * IDX_DIM + IDX_DIM + IDX_HEADS + N_BRANCH * D_MODEL

kernel_name = 'hybrid_moba_conv_dsa_memory_gated'


def _split_points():
    pts, acc = [], 0
    for s in SPLIT_SIZES[:-1]:
        acc += s
        pts.append(acc)
    return pts


def rms_norm(x, g):
    xf = x.astype(jnp.float32)
    y = xf * lax.rsqrt(jnp.mean(xf * xf, axis=-1, keepdims=True) + EPS)
    return (y * g.astype(jnp.float32)).astype(x.dtype)


def moba_attention(q, k, v):
    b, L, h, d = q.shape
    nb = -(-L // MOBA_BLOCK)
    Lp = nb * MOBA_BLOCK
    pad = ((0, 0), (0, Lp - L), (0, 0), (0, 0))
    q, k, v = jnp.pad(q, pad), jnp.pad(k, pad), jnp.pad(v, pad)
    qh = q.transpose(0, 2, 1, 3)
    kb = k.transpose(0, 2, 1, 3).reshape(b, h, nb, MOBA_BLOCK, d)
    vb = v.transpose(0, 2, 1, 3).reshape(b, h, nb, MOBA_BLOCK, d)
    kmean = jnp.mean(kb.astype(jnp.float32), axis=3)
    qblk = jnp.arange(Lp) // MOBA_BLOCK
    bscore = jnp.einsum('bhtd,bhnd->bhtn', qh.astype(jnp.float32), kmean)
    past = jnp.arange(nb)[None, :] < qblk[:, None]
    bscore = jnp.where(past, bscore, -jnp.inf)
    _, top_idx = lax.top_k(bscore, min(MOBA_TOPK, nb))
    own = jnp.broadcast_to(qblk[None, None, :, None], (b, h, Lp, 1)).astype(top_idx.dtype)
    sel = jnp.concatenate([top_idx, own], axis=-1)
    n_sel = sel.shape[-1]
    C = MOBA_Q_CHUNK
    n_chunks = Lp // C
    qs = qh.reshape(b, h, n_chunks, C, d).transpose(2, 0, 1, 3, 4)
    sels = sel.reshape(b, h, n_chunks, C, n_sel).transpose(2, 0, 1, 3, 4)
    gather = jax.vmap(jax.vmap(lambda blocks, ix: blocks[ix]))
    scale = d ** -0.5
    is_own = (jnp.arange(n_sel) == n_sel - 1)[:, None]

    def chunk(args):
        qc, selc, cid = args
        t = cid * C + jnp.arange(C)
        kg = gather(kb, selc)
        vg = gather(vb, selc)
        s = jnp.einsum('bhcd,bhcnkd->bhcnk', qc, kg).astype(jnp.float32) * scale
        kpos = selc[..., None] * MOBA_BLOCK + jnp.arange(MOBA_BLOCK)
        valid = jnp.where(is_own, kpos <= t[:, None, None],
                          selc[..., None] < (t // MOBA_BLOCK)[:, None, None])
        s = jnp.where(valid, s, -jnp.inf)
        p = jax.nn.softmax(s.reshape(b, h, C, n_sel * MOBA_BLOCK), axis=-1)
        p = p.reshape(b, h, C, n_sel, MOBA_BLOCK).astype(vg.dtype)
        return jnp.einsum('bhcnk,bhcnkd->bhcd', p, vg)

    o = lax.map(chunk, (qs, sels, jnp.arange(n_chunks)))
    o = o.transpose(1, 2, 0, 3, 4).reshape(b, h, Lp, d).transpose(0, 2, 1, 3)
    return o[:, :L]


def short_conv(u, w, bias):
    y = lax.conv_general_dilated(u, w.astype(u.dtype)[:, None, :], window_strides=(1,),
                                 padding=[(CONV_WIDTH - 1, 0)],
                                 dimension_numbers=('NWC', 'WIO', 'NWC'),
                                 feature_group_count=u.shape[-1])
    return y + bias.astype(u.dtype)


def dsa_attention(q, k, v, qi, ki, wi):
    b, L, h, d = q.shape
    topk = min(DSA_TOPK_MAX, L // 4)
    C = DSA_Q_CHUNK
    n_chunks = L // C
    qs = q.reshape(b, n_chunks, C, h, d).transpose(1, 0, 2, 3, 4)
    qis = qi.reshape(b, n_chunks, C, IDX_HEADS, IDX_DIM).transpose(1, 0, 2, 3, 4)
    wis = wi.reshape(b, n_chunks, C, IDX_HEADS).transpose(1, 0, 2, 3)
    kif = ki.astype(jnp.float32)
    gather = jax.vmap(lambda kk, ix: kk[ix])
    scale = d ** -0.5
    idx_scale = (IDX_DIM ** -0.5) * (IDX_HEADS ** -0.5)
    key_pos = jnp.arange(L)

    def chunk(args):
        qc, qic, wic, cid = args
        t = cid * C + jnp.arange(C)
        logits = jnp.einsum('bcji,bsi->bcjs', qic.astype(jnp.float32), kif)
        score = jnp.einsum('bcj,bcjs->bcs', wic.astype(jnp.float32) * idx_scale,
                           jax.nn.relu(logits))
        score = jnp.where(key_pos[None, None, :] <= t[None, :, None], score, -jnp.inf)
        _, idx = lax.top_k(score, topk)
        kg = gather(k, idx)
        vg = gather(v, idx)
        s = jnp.einsum('bchd,bckhd->bhck', qc, kg).astype(jnp.float32) * scale
        valid = (idx <= t[None, :, None])[:, None]
        p = jax.nn.softmax(jnp.where(valid, s, -jnp.inf), axis=-1).astype(vg.dtype)
        return jnp.einsum('bhck,bckhd->bchd', p, vg)

    o = lax.map(chunk, (qs, qis, wis, jnp.arange(n_chunks)))
    return o.transpose(1, 0, 2, 3, 4).reshape(b, L, h, d)


def memory_attention(q, mk, mv):
    s = jnp.einsum('bthd,bmhd->bhtm', q, mk).astype(jnp.float32) * (q.shape[-1] ** -0.5)
    p = jax.nn.softmax(s, axis=-1).astype(mv.dtype)
    return jnp.einsum('bhtm,bmhd->bthd', p, mv)


def hybrid_layer(x, mem, ln_g, w_in, conv_w, conv_b, mem_ln_g, w_mem_kv, w_branch, w_out):
    b, L, _ = x.shape
    xn = rms_norm(x, ln_g)
    cols = jnp.split(w_in, _split_points(), axis=1)
    (a_q, a_k, a_v, a_g, c_b, c_c, c_h, c_g, s_q, s_k, s_v, s_g, m_q, m_g,
     i_q, i_k, i_w, r_a, r_c, r_s, r_m) = [xn @ w for w in cols]

    def heads(t, n, d):
        return t.reshape(b, L, n, d)

    y_a = moba_attention(heads(a_q, MOBA_HEADS, MOBA_HEAD_DIM), heads(a_k, MOBA_HEADS, MOBA_HEAD_DIM),
                         heads(a_v, MOBA_HEADS, MOBA_HEAD_DIM)).reshape(b, L, BRANCH_W)
    y_a = y_a * jax.nn.silu(a_g)
    y_c = c_b * short_conv(c_c * c_h, conv_w, conv_b) * jax.nn.silu(c_g)
    y_s = dsa_attention(heads(s_q, DSA_HEADS, DSA_HEAD_DIM), heads(s_k, DSA_HEADS, DSA_HEAD_DIM),
                        heads(s_v, DSA_HEADS, DSA_HEAD_DIM), heads(i_q, IDX_HEADS, IDX_DIM),
                        i_k, i_w).reshape(b, L, BRANCH_W)
    y_s = y_s * jax.nn.silu(s_g)
    mem_n = rms_norm(mem, mem_ln_g)
    mk, mv = jnp.split(mem_n @ w_mem_kv, 2, axis=-1)
    M = mem.shape[1]
    y_m = memory_attention(heads(m_q, MEM_HEADS, MEM_HEAD_DIM),
                           mk.reshape(b, M, MEM_HEADS, MEM_HEAD_DIM),
                           mv.reshape(b, M, MEM_HEADS, MEM_HEAD_DIM)).reshape(b, L, BRANCH_W)
    y_m = y_m * jax.nn.silu(m_g)
    merged = (jax.nn.sigmoid(r_a) * (y_a @ w_branch[0]) + jax.nn.sigmoid(r_c) * (y_c @ w_branch[1])
              + jax.nn.sigmoid(r_s) * (y_s @ w_branch[2]) + jax.nn.sigmoid(r_m) * (y_m @ w_branch[3]))
    return x + merged @ w_out


def setup_inputs(seed: int = 0) -> dict:
    key = jax.random.key(seed)
    ks = jax.random.split(key, 11)
    f32 = jnp.float32
    x = jax.random.normal(ks[0], (BATCH, SEQ, D_MODEL), f32)
    mem = jax.random.normal(ks[1], (BATCH, MEM_LEN, D_MODEL), f32)
    ln_g = 1.0 + 0.02 * jax.random.normal(ks[2], (DEPTH, D_MODEL), f32)
    w_in = jax.random.normal(ks[3], (DEPTH, D_MODEL, IN_COLS), f32) * D_MODEL ** -0.5
    conv_w = jax.random.normal(ks[4], (DEPTH, CONV_WIDTH, BRANCH_W), f32) * CONV_WIDTH ** -0.5
    conv_b = 0.02 * jax.random.normal(ks[5], (DEPTH, BRANCH_W), f32)
    mem_ln_g = 1.0 + 0.02 * jax.random.normal(ks[6], (DEPTH, D_MODEL), f32)
    w_mem_kv = jax.random.normal(ks[7], (DEPTH, D_MODEL, 2 * BRANCH_W), f32) * D_MODEL ** -0.5
    w_branch = jax.random.normal(ks[8], (DEPTH, N_BRANCH, BRANCH_W, D_MODEL), f32) * BRANCH_W ** -0.5
    w_out = jax.random.normal(ks[9], (DEPTH, D_MODEL, D_MODEL), f32) * D_MODEL ** -0.5
    final_g = 1.0 + 0.02 * jax.random.normal(ks[10], (D_MODEL,), f32)
    return {'x': x, 'mem': mem, 'ln_g': ln_g, 'w_in': w_in, 'conv_w': conv_w, 'conv_b': conv_b,
            'mem_ln_g': mem_ln_g, 'w_mem_kv': w_mem_kv, 'w_branch': w_branch, 'w_out': w_out,
            'final_g': final_g}


def reference(x, mem, ln_g, w_in, conv_w, conv_b, mem_ln_g, w_mem_kv, w_branch, w_out, final_g):
    for layer in range(DEPTH):
        x = hybrid_layer(x, mem, ln_g[layer], w_in[layer], conv_w[layer], conv_b[layer],
                         mem_ln_g[layer], w_mem_kv[layer], w_branch[layer], w_out[layer])
    return rms_norm(x, final_g)
```

```python
import functools

import numpy as np
import jax
import jax.numpy as jnp
from jax import lax
from jax.experimental import pallas as pl
from jax.experimental.pallas import tpu as pltpu

EPS = 1e-6
BRANCH_W = 1024
MOBA_HEADS = 8
MOBA_BLOCK = 256
MOBA_TOPK = 3
DSA_HEADS = 8
DSA_TOPK_MAX = 256
IDX_HEADS = 4
IDX_DIM = 64
MEM_HEADS = 4
HEAD_DIM = 128
MEM_HEAD_DIM = 256
IDX_SLOT = 256
NEG = -1e30
LANE = 128
VMEM_LIMIT = 56 * 1024 * 1024

F32 = jnp.float32
BF16 = jnp.bfloat16
_ONE = np.float32(1.0)
_ZERO = np.float32(0.0)
_NEG = np.float32(NEG)


def _cparams(sem):
    return pltpu.CompilerParams(dimension_semantics=sem, vmem_limit_bytes=VMEM_LIMIT)


def _dot(a, b):
    return jnp.dot(a, b, preferred_element_type=F32)


def _dot_nt(a, b):
    return lax.dot_general(a, b, (((1,), (1,)), ((), ())), preferred_element_type=F32)


def _split2(v):
    hi = v.astype(BF16)
    lo = (v - hi.astype(F32)).astype(BF16)
    return hi, lo


def _split3(v):
    h1 = v.astype(BF16)
    r1 = v - h1.astype(F32)
    h2 = r1.astype(BF16)
    h3 = (r1 - h2.astype(F32)).astype(BF16)
    return h1, h2, h3


def _ind(cond):
    return jnp.where(cond, _ONE, _ZERO)


def _silu(g):
    return g * jax.nn.sigmoid(g)


def _rms(x, g):
    var = jnp.mean(x * x, axis=-1, keepdims=True)
    return (x * lax.rsqrt(var + EPS)) * g


def _rmsnorm_kernel(x_ref, g_ref, o_ref):
    o_ref[...] = _rms(x_ref[...], g_ref[...]).astype(o_ref.dtype)


def _rmsnorm(x, g, out_dtype, tm=256):
    n, d = x.shape
    return pl.pallas_call(
        _rmsnorm_kernel,
        grid=(n // tm,),
        in_specs=[pl.BlockSpec((tm, d), lambda i: (i, 0)),
                  pl.BlockSpec((1, d), lambda i: (0, 0))],
        out_specs=pl.BlockSpec((tm, d), lambda i: (i, 0)),
        out_shape=jax.ShapeDtypeStruct((n, d), out_dtype),
        compiler_params=_cparams(("parallel",)),
        name="rmsnorm",
    )(x, g.reshape(1, d))


def _prep_kernel(x_ref, g_ref, whi_ref, wlo_ref, wwhi_ref, wwlo_ref, xn_ref, cat_ref, wt_ref):
    xn = _rms(x_ref[...], g_ref[...])
    hi, lo = _split2(xn)
    xn_ref[...] = hi
    acc = _dot(hi, whi_ref[...]) + _dot(hi, wlo_ref[...]) + _dot(lo, whi_ref[...])
    resid = acc - acc.astype(BF16).astype(F32)
    lane = lax.broadcasted_iota(jnp.int32, acc.shape, 1)
    piece = (lane & (IDX_SLOT - 1)) >> (IDX_DIM.bit_length() - 1)
    lo_piece = jnp.where(lane < IDX_HEADS * IDX_SLOT, np.int32(2), np.int32(1))
    cat_ref[...] = jnp.where(piece == lo_piece, resid, acc).astype(BF16)
    wt_ref[...] = (_dot_nt(wwhi_ref[...], hi) + _dot_nt(wwlo_ref[...], hi)
                   + _dot_nt(wwhi_ref[...], lo))


def _prep(x, g, whi, wlo, wwhi, wwlo, tm=256):
    n, d = x.shape
    ci = whi.shape[1]
    return pl.pallas_call(
        _prep_kernel,
        grid=(n // tm,),
        in_specs=[pl.BlockSpec((tm, d), lambda i: (i, 0)),
                  pl.BlockSpec((1, d), lambda i: (0, 0)),
                  pl.BlockSpec((d, ci), lambda i: (0, 0)),
                  pl.BlockSpec((d, ci), lambda i: (0, 0)),
                  pl.BlockSpec((8, d), lambda i: (0, 0)),
                  pl.BlockSpec((8, d), lambda i: (0, 0))],
        out_specs=[pl.BlockSpec((tm, d), lambda i: (i, 0)),
                   pl.BlockSpec((tm, ci), lambda i: (i, 0)),
                   pl.BlockSpec((8, tm), lambda i: (0, i))],
        out_shape=[jax.ShapeDtypeStruct((n, d), BF16),
                   jax.ShapeDtypeStruct((n, ci), BF16),
                   jax.ShapeDtypeStruct((8, n), F32)],
        compiler_params=_cparams(("parallel",)),
        name="prep",
    )(x, g.reshape(1, d), whi, wlo, wwhi, wwlo)


def _mm_nn_kernel(a_ref, b_ref, o_ref):
    o_ref[...] = _dot(a_ref[...], b_ref[...]).astype(o_ref.dtype)


def _mm_nn(a, b, out_dtype, tm, tn, name):
    m, k = a.shape
    n = b.shape[1]
    return pl.pallas_call(
        _mm_nn_kernel,
        grid=(m // tm, n // tn),
        in_specs=[pl.BlockSpec((tm, k), lambda i, j: (i, 0)),
                  pl.BlockSpec((k, tn), lambda i, j: (0, j))],
        out_specs=pl.BlockSpec((tm, tn), lambda i, j: (i, j)),
        out_shape=jax.ShapeDtypeStruct((m, n), out_dtype),
        compiler_params=_cparams(("parallel", "parallel")),
        name=name,
    )(a, b)


def _mm_nt_kernel(a_ref, b_ref, o_ref):
    o_ref[...] = _dot_nt(a_ref[...], b_ref[...]).astype(o_ref.dtype)


def _mm_nt(a, b, out_dtype, tm, tn, name):
    m, k = a.shape
    n = b.shape[0]
    return pl.pallas_call(
        _mm_nt_kernel,
        grid=(n // tn, m // tm),
        in_specs=[pl.BlockSpec((tm, k), lambda j, i: (i, 0)),
                  pl.BlockSpec((tn, k), lambda j, i: (j, 0))],
        out_specs=pl.BlockSpec((tm, tn), lambda j, i: (i, j)),
        out_shape=jax.ShapeDtypeStruct((m, n), out_dtype),
        compiler_params=_cparams(("parallel", "parallel")),
        name=name,
    )(a, b)


def _mm_res_kernel(a_ref, b_ref, r_ref, o_ref):
    o_ref[...] = r_ref[...] + _dot(a_ref[...], b_ref[...])


def _mm_res(a, b, res, tm, tn):
    m, k = a.shape
    n = b.shape[1]
    return pl.pallas_call(
        _mm_res_kernel,
        grid=(m // tm, n // tn),
        in_specs=[pl.BlockSpec((tm, k), lambda i, j: (i, 0)),
                  pl.BlockSpec((k, tn), lambda i, j: (0, j)),
                  pl.BlockSpec((tm, tn), lambda i, j: (i, j))],
        out_specs=pl.BlockSpec((tm, tn), lambda i, j: (i, j)),
        out_shape=jax.ShapeDtypeStruct((m, n), F32),
        compiler_params=_cparams(("parallel", "parallel")),
        name="out_proj",
    )(a, b, res)


def _merge_kernel(ya_ref, yc_ref, ys_ref, ym_ref, ra_ref, rc_ref, rs_ref, rm_ref, wb_ref, o_ref):
    ys = (ya_ref, yc_ref, ys_ref, ym_ref)
    rs = (ra_ref, rc_ref, rs_ref, rm_ref)
    acc = None
    for br in range(4):
        z = _dot(ys[br][...], wb_ref[br])
        term = jax.nn.sigmoid(rs[br][...].astype(F32)) * z
        acc = term if acc is None else acc + term
    o_ref[...] = acc.astype(o_ref.dtype)


def _merge(ys, act, r_col0, wb, tm, tn):
    n, w = ys[0].shape
    d = wb.shape[2]
    nj = d // tn
    y_specs = [pl.BlockSpec((tm, w), lambda i, j: (i, 0)) for _ in range(4)]
    r_specs = [pl.BlockSpec((tm, tn), functools.partial(
        lambda i, j, base: (i, base + j), base=(r_col0 + br * d) // tn)) for br in range(4)]
    del nj
    return pl.pallas_call(
        _merge_kernel,
        grid=(n // tm, d // tn),
        in_specs=y_specs + r_specs + [pl.BlockSpec((4, w, tn), lambda i, j: (0, 0, j))],
        out_specs=pl.BlockSpec((tm, tn), lambda i, j: (i, j)),
        out_shape=jax.ShapeDtypeStruct((n, d), BF16),
        compiler_params=_cparams(("parallel", "parallel")),
        name="merge",
    )(*ys, act, act, act, act, wb)


def _moba_kernel(q_ref, k_ref, vt_ref, g_ref, o_ref, kmean_ref, selb_ref, *, nblk):
    i = pl.program_id(2)
    blk = MOBA_BLOCK
    scale = HEAD_DIM ** -0.5

    @pl.when(i == 0)
    def _():
        kf = k_ref[...].astype(F32).reshape(nblk, blk, HEAD_DIM)
        kmean_ref[...] = jnp.mean(kf, axis=1)

    q = q_ref[...]
    k1, k2, k3 = _split3(kmean_ref[...])
    bs = _dot_nt(k1, q) + _dot_nt(k2, q) + _dot_nt(k3, q)
    n_iota = lax.broadcasted_iota(jnp.int32, bs.shape, 0)
    rank = jnp.zeros(bs.shape, jnp.int32)
    for m in range(nblk):
        row = bs[m:m + 1, :]
        beats = (row > bs) | ((row == bs) & (m < n_iota))
        rank = rank + beats.astype(jnp.int32) * (m < i).astype(jnp.int32)
    sel = (n_iota < i) & (rank < MOBA_TOPK)
    selb_ref[...] = jnp.where(sel, _ZERO, _NEG)

    off = pl.multiple_of(i * blk, blk)
    st = _dot_nt(k_ref[pl.ds(off, blk), :], q) * scale
    kpos = lax.broadcasted_iota(jnp.int32, st.shape, 0)
    qpos = lax.broadcasted_iota(jnp.int32, st.shape, 1)
    st = jnp.where(kpos <= qpos, st, NEG)
    m0 = jnp.max(st, axis=0, keepdims=True)
    p = jnp.exp(st - m0)
    l0 = jnp.sum(p, axis=0, keepdims=True)
    acc0 = _dot(vt_ref[:, pl.ds(off, blk)], p.astype(BF16))

    def body(n, carry):
        m_i, l_i, acc = carry
        o_n = pl.multiple_of(n * blk, blk)
        s = _dot_nt(k_ref[pl.ds(o_n, blk), :], q) * scale + selb_ref[pl.ds(n, 1), :]
        m_new = jnp.maximum(m_i, jnp.max(s, axis=0, keepdims=True))
        alpha = jnp.exp(m_i - m_new)
        pn = jnp.exp(s - m_new)
        l_new = alpha * l_i + jnp.sum(pn, axis=0, keepdims=True)
        acc_new = alpha * acc + _dot(vt_ref[:, pl.ds(o_n, blk)], pn.astype(BF16))
        return m_new, l_new, acc_new

    _, l_f, acc_f = lax.fori_loop(0, i, body, (m0, l0, acc0))
    o = (acc_f / l_f).T
    o_ref[...] = (o * _silu(g_ref[...].astype(F32))).astype(o_ref.dtype)


def _moba(act, act_t, batch, seq, q_cb, k_cb, g_cb, v_rb):
    nblk = seq // MOBA_BLOCK
    n = act.shape[0]
    blk = MOBA_BLOCK
    return pl.pallas_call(
        functools.partial(_moba_kernel, nblk=nblk),
        grid=(batch, MOBA_HEADS, nblk),
        in_specs=[pl.BlockSpec((blk, HEAD_DIM), lambda b, h, i: (b * nblk + i, q_cb + h)),
                  pl.BlockSpec((seq, HEAD_DIM), lambda b, h, i: (b, k_cb + h)),
                  pl.BlockSpec((HEAD_DIM, seq), lambda b, h, i: (v_rb + h, b)),
                  pl.BlockSpec((blk, HEAD_DIM), lambda b, h, i: (b * nblk + i, g_cb + h))],
        out_specs=pl.BlockSpec((blk, HEAD_DIM), lambda b, h, i: (b * nblk + i, h)),
        out_shape=jax.ShapeDtypeStruct((n, BRANCH_W), BF16),
        scratch_shapes=[pltpu.VMEM((nblk, HEAD_DIM), F32),
                        pltpu.VMEM((nblk, blk), F32)],
        compiler_params=_cparams(("parallel", "parallel", "arbitrary")),
        name="moba",
    )(act, act, act_t, act)


def _conv_kernel(cb_ref, cc_ref, ch_ref, cg_ref, w_ref, bias_ref, o_ref):
    u = cc_ref[...].astype(F32) * ch_ref[...].astype(F32)
    t = lax.broadcasted_iota(jnp.int32, u.shape, 0)
    u1 = jnp.where(t >= 1, pltpu.roll(u, 1, 0), 0.0)
    u2 = jnp.where(t >= 2, pltpu.roll(u, 2, 0), 0.0)
    w = w_ref[...]
    conv = u2 * w[0:1, :] + u1 * w[1:2, :] + u * w[2:3, :] + bias_ref[...]
    o_ref[...] = (cb_ref[...].astype(F32) * conv * _silu(cg_ref[...].astype(F32))).astype(o_ref.dtype)


def _conv(act, conv_w, conv_b, batch, seq, cb_cb, cc_cb, ch_cb, cg_cb):
    n = act.shape[0]
    ncb = BRANCH_W // LANE
    kw = conv_w.shape[0]
    wpad = jnp.zeros((8, BRANCH_W), F32).at[:kw].set(conv_w)

    def spec(base):
        return pl.BlockSpec((seq, LANE), lambda b, c: (b, base + c))

    return pl.pallas_call(
        _conv_kernel,
        grid=(batch, ncb),
        in_specs=[spec(cb_cb), spec(cc_cb), spec(ch_cb), spec(cg_cb),
                  pl.BlockSpec((8, LANE), lambda b, c: (0, c)),
                  pl.BlockSpec((1, LANE), lambda b, c: (0, c))],
        out_specs=pl.BlockSpec((seq, LANE), lambda b, c: (b, c)),
        out_shape=jax.ShapeDtypeStruct((n, BRANCH_W), BF16),
        compiler_params=_cparams(("parallel", "parallel")),
        name="conv",
    )(act, act, act, act, wpad, conv_b.reshape(1, BRANCH_W))


def _to_key(f):
    b = pltpu.bitcast(f, jnp.int32)
    return jnp.where(b >= 0, b, b ^ jnp.int32(0x7FFFFFFF))


def _from_key(k):
    return pltpu.bitcast(jnp.where(k >= 0, k, k ^ jnp.int32(0x7FFFFFFF)), F32)


def _dsa_kernel(qc_ref, kc_ref, wt_ref, q_ref, k_ref, vt_ref, g_ref, o_ref, st_ref, tri_ref,
                *, topk, tq):
    i = pl.program_id(1)
    h = pl.program_id(2)
    nch = i + 1
    scale = HEAD_DIM ** -0.5
    idx_scale = (IDX_DIM ** -0.5) * (IDX_HEADS ** -0.5)
    kf = float(topk)

    def chunk(c):
        return pl.ds(pl.multiple_of(c * tq, tq), tq)

    def count_ge(thr):
        def body(c, acc):
            ge = _ind(st_ref[chunk(c), :] >= thr)
            return acc + jnp.sum(ge.reshape(tq // 8, 8, tq), axis=0)
        acc = lax.fori_loop(0, nch, body, jnp.zeros((8, tq), F32))
        return jnp.sum(acc, axis=0, keepdims=True)

    @pl.when(h == 0)
    def _select():
        def score_body(c, _):
            kc = kc_ref[chunk(c), :]
            sc = jnp.zeros((tq, tq), F32)
            for j in range(IDX_HEADS):
                lg = _dot_nt(kc, qc_ref[:, j * IDX_SLOT:(j + 1) * IDX_SLOT])
                sc = sc + (wt_ref[j:j + 1, :] * idx_scale) * jnp.maximum(lg, 0.0)
            kpos = c * tq + lax.broadcasted_iota(jnp.int32, sc.shape, 0)
            qpos = i * tq + lax.broadcasted_iota(jnp.int32, sc.shape, 1)
            st_ref[chunk(c), :] = jnp.where(kpos <= qpos, sc, -jnp.inf)
            return 0
        lax.fori_loop(0, nch, score_body, 0)

        lo0 = _to_key(jnp.full((1, tq), -jnp.inf, F32))
        hi0 = _to_key(jnp.full((1, tq), jnp.inf, F32))

        def bis_body(_, carry):
            lo, hi = carry
            mid = (lo >> 1) + (hi >> 1) + (lo & hi & 1)
            ge = count_ge(_from_key(mid)) >= kf
            return jnp.where(ge, mid, lo), jnp.where(ge, hi, mid)

        _, hi_k = lax.fori_loop(0, 32, bis_body, (lo0, hi0))

        def walk_cond(carry):
            _, _, done = carry
            return jnp.min(done) < 0.5

        def walk_body(carry):
            hi_f, thr, done = carry

            def mx_body(c, acc):
                s = st_ref[chunk(c), :]
                v = jnp.where(s < hi_f, s, -jnp.inf)
                return jnp.maximum(acc, jnp.max(v.reshape(tq // 8, 8, tq), axis=0))
            mx = lax.fori_loop(0, nch, mx_body, jnp.full((8, tq), -jnp.inf, F32))
            v = jnp.max(mx, axis=0, keepdims=True)
            ok = count_ge(v) >= kf
            is_done = done > 0.5
            thr_n = jnp.where(is_done, thr, v)
            hi_n = jnp.where(is_done, hi_f, v)
            done_n = _ind(is_done | ok)
            return hi_n, thr_n, done_n

        hi_f0 = _from_key(hi_k)
        _, thr, _ = lax.while_loop(
            walk_cond, walk_body,
            (hi_f0, jnp.full((1, tq), -jnp.inf, F32), jnp.zeros((1, tq), F32)))

        def gt_body(c, acc):
            gt = _ind(st_ref[chunk(c), :] > thr)
            return acc + jnp.sum(gt.reshape(tq // 8, 8, tq), axis=0)
        n_gt = jnp.sum(lax.fori_loop(0, nch, gt_body, jnp.zeros((8, tq), F32)),
                       axis=0, keepdims=True)
        need = kf - n_gt

        r = lax.broadcasted_iota(jnp.int32, (tq, tq), 0)
        cidx = lax.broadcasted_iota(jnp.int32, (tq, tq), 1)
        tri_ref[...] = _ind(cidx < r).astype(BF16)

        def bias_body(c, carry):
            s = st_ref[chunk(c), :]
            eq = s == thr
            eqf = _ind(eq)
            before = _dot(tri_ref[...], eqf.astype(BF16)) + carry
            kpos = c * tq + lax.broadcasted_iota(jnp.int32, s.shape, 0)
            qpos = i * tq + lax.broadcasted_iota(jnp.int32, s.shape, 1)
            selected = ((s > thr) | (eq & (before < need))) & (kpos <= qpos)
            st_ref[chunk(c), :] = jnp.where(selected, _ZERO, _NEG)
            return carry + jnp.sum(eqf, axis=0, keepdims=True)
        lax.fori_loop(0, nch, bias_body, jnp.zeros((1, tq), F32))

    q = q_ref[...]

    def att_body(c, carry):
        m_i, l_i, acc = carry
        s = _dot_nt(k_ref[chunk(c), :], q) * scale + st_ref[chunk(c), :]
        m_new = jnp.maximum(m_i, jnp.max(s, axis=0, keepdims=True))
        alpha = jnp.exp(m_i - m_new)
        p = jnp.exp(s - m_new)
        l_new = alpha * l_i + jnp.sum(p, axis=0, keepdims=True)
        acc_new = alpha * acc + _dot(vt_ref[:, chunk(c)], p.astype(BF16))
        return m_new, l_new, acc_new

    init = (jnp.full((1, tq), NEG, F32), jnp.zeros((1, tq), F32), jnp.zeros((HEAD_DIM, tq), F32))
    _, l_f, acc_f = lax.fori_loop(0, nch, att_body, init)
    o = (acc_f / l_f).T
    o_ref[...] = (o * _silu(g_ref[...].astype(F32))).astype(o_ref.dtype)


def _dsa(act, act_t, cat, wt, batch, seq, q_cb, k_cb, g_cb, v_rb, tq=256):
    n = act.shape[0]
    nq = seq // tq
    topk = min(DSA_TOPK_MAX, seq // 4)
    n_qcat = IDX_HEADS * IDX_SLOT
    return pl.pallas_call(
        functools.partial(_dsa_kernel, topk=topk, tq=tq),
        grid=(batch, nq, DSA_HEADS),
        in_specs=[pl.BlockSpec((tq, n_qcat), lambda b, i, h: (b * nq + i, 0)),
                  pl.BlockSpec((seq, IDX_SLOT), lambda b, i, h: (b, n_qcat // IDX_SLOT)),
                  pl.BlockSpec((8, tq), lambda b, i, h: (0, b * nq + i)),
                  pl.BlockSpec((tq, HEAD_DIM), lambda b, i, h: (b * nq + i, q_cb + h)),
                  pl.BlockSpec((seq, HEAD_DIM), lambda b, i, h: (b, k_cb + h)),
                  pl.BlockSpec((HEAD_DIM, seq), lambda b, i, h: (v_rb + h, b)),
                  pl.BlockSpec((tq, HEAD_DIM), lambda b, i, h: (b * nq + i, g_cb + h))],
        out_specs=pl.BlockSpec((tq, HEAD_DIM), lambda b, i, h: (b * nq + i, h)),
        out_shape=jax.ShapeDtypeStruct((n, BRANCH_W), BF16),
        scratch_shapes=[pltpu.VMEM((seq, tq), F32),
                        pltpu.VMEM((tq, tq), BF16)],
        compiler_params=_cparams(("parallel", "arbitrary", "arbitrary")),
        name="dsa",
    )(cat, cat, wt, act, act, act_t, act)


def _mem_kernel(q_ref, mk_ref, mvt_ref, g_ref, o_ref):
    scale = MEM_HEAD_DIM ** -0.5
    s = _dot_nt(mk_ref[...], q_ref[...]) * scale
    m = jnp.max(s, axis=0, keepdims=True)
    p = jnp.exp(s - m)
    l = jnp.sum(p, axis=0, keepdims=True)
    o = (_dot(mvt_ref[...], p.astype(BF16)) / l).T
    o_ref[...] = (o * _silu(g_ref[...].astype(F32))).astype(o_ref.dtype)


def _mem_attn(act, mk, mvt, batch, seq, mem_len, q_cb, g_cb, tq=512):
    n = act.shape[0]
    nq = seq // tq
    hd = MEM_HEAD_DIM
    return pl.pallas_call(
        _mem_kernel,
        grid=(batch, nq, MEM_HEADS),
        in_specs=[pl.BlockSpec((tq, hd), lambda b, i, h: (b * nq + i, q_cb + h)),
                  pl.BlockSpec((mem_len, hd), lambda b, i, h: (b, h)),
                  pl.BlockSpec((hd, mem_len), lambda b, i, h: (h, b)),
                  pl.BlockSpec((tq, hd), lambda b, i, h: (b * nq + i, g_cb + h))],
        out_specs=pl.BlockSpec((tq, hd), lambda b, i, h: (b * nq + i, h)),
        out_shape=jax.ShapeDtypeStruct((n, BRANCH_W), BF16),
        compiler_params=_cparams(("parallel", "parallel", "parallel")),
        name="mem_attn",
    )(act, mk, mvt, act)


def _layer_weights(w_in, w_mem_kv, w_branch, w_out):
    bw = BRANCH_W
    d = w_in.shape[0]
    o_iq = 14 * bw
    o_ik = o_iq + IDX_HEADS * IDX_DIM
    o_iw = o_ik + IDX_DIM
    o_r = o_iw + IDX_HEADS
    w_tok = jnp.concatenate(
        [w_in[:, 0:2 * bw], w_in[:, 3 * bw:10 * bw], w_in[:, 11 * bw:14 * bw], w_in[:, o_r:]],
        axis=1).astype(BF16)
    w_vt = jnp.concatenate([w_in[:, 2 * bw:3 * bw], w_in[:, 10 * bw:11 * bw]], axis=1).T.astype(BF16)
    zeros = jnp.zeros((d, IDX_SLOT - 3 * IDX_DIM), F32)
    parts = []
    for j in range(IDX_HEADS):
        wq = w_in[:, o_iq + j * IDX_DIM:o_iq + (j + 1) * IDX_DIM]
        parts += [wq, wq, wq, zeros]
    wk = w_in[:, o_ik:o_iw]
    parts += [wk, wk, wk, zeros]
    w_idx = jnp.concatenate(parts, axis=1)
    w_idx_hi, w_idx_lo = _split2(w_idx)
    ww = jnp.zeros((8, d), F32).at[:IDX_HEADS].set(w_in[:, o_iw:o_r].T)
    ww_hi, ww_lo = _split2(ww)
    w_mk = w_mem_kv[:, :bw].astype(BF16)
    w_mvt = w_mem_kv[:, bw:].T.astype(BF16)
    return dict(w_tok=w_tok, w_vt=w_vt, w_idx_hi=w_idx_hi, w_idx_lo=w_idx_lo, ww_hi=ww_hi,
                ww_lo=ww_lo, w_mk=w_mk, w_mvt=w_mvt, w_branch=w_branch.astype(BF16),
                w_out=w_out.astype(BF16))


def _layer(x2, mem2, batch, seq, mem_len, ln_g, conv_w, conv_b, mem_ln_g, w):
    nb = BRANCH_W // LANE
    (A_Q, A_K, A_G, C_B, C_C, C_H, C_G, S_Q, S_K, S_G, M_Q, M_G) = range(12)
    r_col0 = 12 * BRANCH_W

    xn, cat, wt = _prep(x2, ln_g, w["w_idx_hi"], w["w_idx_lo"], w["ww_hi"], w["ww_lo"])
    act = _mm_nn(xn, w["w_tok"], BF16, tm=1024, tn=1024, name="proj_tok")
    act_t = _mm_nt(w["w_vt"], xn, BF16, tm=1024, tn=1024, name="proj_vt")

    y_a = _moba(act, act_t, batch, seq, A_Q * nb, A_K * nb, A_G * nb, 0)
    y_c = _conv(act, conv_w, conv_b, batch, seq, C_B * nb, C_C * nb, C_H * nb, C_G * nb)
    y_s = _dsa(act, act_t, cat, wt, batch, seq, S_Q * nb, S_K * nb, S_G * nb, nb)

    mem_n = _rmsnorm(mem2, mem_ln_g, BF16)
    mk = _mm_nn(mem_n, w["w_mk"], BF16, tm=mem2.shape[0], tn=512, name="mem_k")
    mvt = _mm_nt(w["w_mvt"], mem_n, BF16, tm=512, tn=mem2.shape[0], name="mem_vt")
    mb = BRANCH_W // MEM_HEAD_DIM
    y_m = _mem_attn(act, mk, mvt, batch, seq, mem_len, M_Q * mb, M_G * mb)

    merged = _merge((y_a, y_c, y_s, y_m), act, r_col0, w["w_branch"], tm=512, tn=512)
    return _mm_res(merged, w["w_out"], x2, tm=512, tn=1024)


def kernel(x, mem, ln_g, w_in, conv_w, conv_b, mem_ln_g, w_mem_kv, w_branch, w_out, final_g):
    batch, seq, d = x.shape
    mem_len = mem.shape[1]
    x2 = x.reshape(batch * seq, d)
    mem2 = mem.reshape(batch * mem_len, d)
    for layer in range(ln_g.shape[0]):
        w = _layer_weights(w_in[layer], w_mem_kv[layer], w_branch[layer], w_out[layer])
        x2 = _layer(x2, mem2, batch, seq, mem_len, ln_g[layer], conv_w[layer], conv_b[layer],
                    mem_ln_g[layer], w)
    return _rmsnorm(x2, final_g, x.dtype).reshape(batch, seq, d)
```

```python
import functools

import numpy as np
import jax
import jax.numpy as jnp
from jax import lax
from jax.experimental import pallas as pl
from jax.experimental.pallas import tpu as pltpu

EPS = 1e-6
BRANCH_W = 1024
MOBA_HEADS = 8
MOBA_BLOCK = 256
MOBA_TOPK = 3
DSA_HEADS = 8
DSA_TOPK_MAX = 256
IDX_HEADS = 4
IDX_DIM = 64
MEM_HEADS = 4
HEAD_DIM = 128
MEM_HEAD_DIM = 256
IDX_SLOT = 256
NEG = -1e30
ATT_UNROLL = 4
HEADS_PER_STEP = 4
LANE = 128
VMEM_LIMIT = 56 * 1024 * 1024

F32 = jnp.float32
BF16 = jnp.bfloat16
_ONE = np.float32(1.0)
_ZERO = np.float32(0.0)
_NEG = np.float32(NEG)


def _cparams(sem):
    return pltpu.CompilerParams(dimension_semantics=sem, vmem_limit_bytes=VMEM_LIMIT)


def _dot(a, b):
    return jnp.dot(a, b, preferred_element_type=F32)


def _dot_nt(a, b):
    return lax.dot_general(a, b, (((1,), (1,)), ((), ())), preferred_element_type=F32)


def _split2(v):
    hi = v.astype(BF16)
    lo = (v - hi.astype(F32)).astype(BF16)
    return hi, lo


def _split3(v):
    h1 = v.astype(BF16)
    r1 = v - h1.astype(F32)
    h2 = r1.astype(BF16)
    h3 = (r1 - h2.astype(F32)).astype(BF16)
    return h1, h2, h3


def _ind(cond):
    return jnp.where(cond, _ONE, _ZERO)


def _silu(g):
    return g * jax.nn.sigmoid(g)


def _rms(x, g):
    var = jnp.mean(x * x, axis=-1, keepdims=True)
    return (x * lax.rsqrt(var + EPS)) * g


def _rmsnorm_kernel(x_ref, g_ref, o_ref):
    o_ref[...] = _rms(x_ref[...], g_ref[...]).astype(o_ref.dtype)


def _rmsnorm(x, g, out_dtype, tm=256):
    n, d = x.shape
    return pl.pallas_call(
        _rmsnorm_kernel,
        grid=(n // tm,),
        in_specs=[pl.BlockSpec((tm, d), lambda i: (i, 0)),
                  pl.BlockSpec((1, d), lambda i: (0, 0))],
        out_specs=pl.BlockSpec((tm, d), lambda i: (i, 0)),
        out_shape=jax.ShapeDtypeStruct((n, d), out_dtype),
        compiler_params=_cparams(("parallel",)),
        name="rmsnorm",
    )(x, g.reshape(1, d))


def _prep_kernel(x_ref, g_ref, whi_ref, wlo_ref, wwhi_ref, wwlo_ref, xn_ref, cat_ref, wt_ref):
    xn = _rms(x_ref[...], g_ref[...])
    hi, lo = _split2(xn)
    xn_ref[...] = hi
    acc = _dot(hi, whi_ref[...]) + _dot(hi, wlo_ref[...]) + _dot(lo, whi_ref[...])
    resid = acc - acc.astype(BF16).astype(F32)
    lane = lax.broadcasted_iota(jnp.int32, acc.shape, 1)
    piece = (lane & (IDX_SLOT - 1)) >> (IDX_DIM.bit_length() - 1)
    lo_piece = jnp.where(lane < IDX_HEADS * IDX_SLOT, np.int32(2), np.int32(1))
    cat_ref[...] = jnp.where(piece == lo_piece, resid, acc).astype(BF16)
    wt_ref[...] = (_dot_nt(wwhi_ref[...], hi) + _dot_nt(wwlo_ref[...], hi)
                   + _dot_nt(wwhi_ref[...], lo))


def _prep(x, g, whi, wlo, wwhi, wwlo, tm=256):
    n, d = x.shape
    ci = whi.shape[1]
    return pl.pallas_call(
        _prep_kernel,
        grid=(n // tm,),
        in_specs=[pl.BlockSpec((tm, d), lambda i: (i, 0)),
                  pl.BlockSpec((1, d), lambda i: (0, 0)),
                  pl.BlockSpec((d, ci), lambda i: (0, 0)),
                  pl.BlockSpec((d, ci), lambda i: (0, 0)),
                  pl.BlockSpec((8, d), lambda i: (0, 0)),
                  pl.BlockSpec((8, d), lambda i: (0, 0))],
        out_specs=[pl.BlockSpec((tm, d), lambda i: (i, 0)),
                   pl.BlockSpec((tm, ci), lambda i: (i, 0)),
                   pl.BlockSpec((8, tm), lambda i: (0, i))],
        out_shape=[jax.ShapeDtypeStruct((n, d), BF16),
                   jax.ShapeDtypeStruct((n, ci), BF16),
                   jax.ShapeDtypeStruct((8, n), F32)],
        compiler_params=_cparams(("parallel",)),
        name="prep",
    )(x, g.reshape(1, d), whi, wlo, wwhi, wwlo)


def _mm_nn_kernel(a_ref, b_ref, o_ref):
    o_ref[...] = _dot(a_ref[...], b_ref[...]).astype(o_ref.dtype)


def _mm_nn(a, b, out_dtype, tm, tn, name):
    m, k = a.shape
    n = b.shape[1]
    return pl.pallas_call(
        _mm_nn_kernel,
        grid=(m // tm, n // tn),
        in_specs=[pl.BlockSpec((tm, k), lambda i, j: (i, 0)),
                  pl.BlockSpec((k, tn), lambda i, j: (0, j))],
        out_specs=pl.BlockSpec((tm, tn), lambda i, j: (i, j)),
        out_shape=jax.ShapeDtypeStruct((m, n), out_dtype),
        compiler_params=_cparams(("parallel", "parallel")),
        name=name,
    )(a, b)


def _mm_nn_colscale_kernel(a_ref, b_ref, s_ref, o_ref):
    o_ref[...] = (_dot(a_ref[...], b_ref[...]) * s_ref[...]).astype(o_ref.dtype)


def _mm_nn_colscale(a, b, colscale, out_dtype, tm, tn, name):
    m, k = a.shape
    n = b.shape[1]
    return pl.pallas_call(
        _mm_nn_colscale_kernel,
        grid=(m // tm, n // tn),
        in_specs=[pl.BlockSpec((tm, k), lambda i, j: (i, 0)),
                  pl.BlockSpec((k, tn), lambda i, j: (0, j)),
                  pl.BlockSpec((1, tn), lambda i, j: (0, j))],
        out_specs=pl.BlockSpec((tm, tn), lambda i, j: (i, j)),
        out_shape=jax.ShapeDtypeStruct((m, n), out_dtype),
        compiler_params=_cparams(("parallel", "parallel")),
        name=name,
    )(a, b, colscale)


def _mm_nt_kernel(a_ref, b_ref, o_ref):
    o_ref[...] = _dot_nt(a_ref[...], b_ref[...]).astype(o_ref.dtype)


def _mm_nt(a, b, out_dtype, tm, tn, name):
    m, k = a.shape
    n = b.shape[0]
    return pl.pallas_call(
        _mm_nt_kernel,
        grid=(n // tn, m // tm),
        in_specs=[pl.BlockSpec((tm, k), lambda j, i: (i, 0)),
                  pl.BlockSpec((tn, k), lambda j, i: (j, 0))],
        out_specs=pl.BlockSpec((tm, tn), lambda j, i: (i, j)),
        out_shape=jax.ShapeDtypeStruct((m, n), out_dtype),
        compiler_params=_cparams(("parallel", "parallel")),
        name=name,
    )(a, b)


def _mm_res_kernel(a_ref, b_ref, r_ref, o_ref):
    o_ref[...] = r_ref[...] + _dot(a_ref[...], b_ref[...])


def _mm_res(a, b, res, tm, tn):
    m, k = a.shape
    n = b.shape[1]
    return pl.pallas_call(
        _mm_res_kernel,
        grid=(m // tm, n // tn),
        in_specs=[pl.BlockSpec((tm, k), lambda i, j: (i, 0)),
                  pl.BlockSpec((k, tn), lambda i, j: (0, j)),
                  pl.BlockSpec((tm, tn), lambda i, j: (i, j))],
        out_specs=pl.BlockSpec((tm, tn), lambda i, j: (i, j)),
        out_shape=jax.ShapeDtypeStruct((m, n), F32),
        compiler_params=_cparams(("parallel", "parallel")),
        name="out_proj",
    )(a, b, res)


def _merge_kernel(ya_ref, yc_ref, ys_ref, ym_ref, ra_ref, rc_ref, rs_ref, rm_ref, wb_ref, o_ref):
    ys = (ya_ref, yc_ref, ys_ref, ym_ref)
    rs = (ra_ref, rc_ref, rs_ref, rm_ref)
    acc = None
    for br in range(4):
        z = _dot(ys[br][...], wb_ref[br])
        term = jax.nn.sigmoid(rs[br][...].astype(F32)) * z
        acc = term if acc is None else acc + term
    o_ref[...] = acc.astype(o_ref.dtype)


def _merge(ys, act, r_col0, wb, tm, tn):
    n, w = ys[0].shape
    d = wb.shape[2]
    nj = d // tn
    y_specs = [pl.BlockSpec((tm, w), lambda i, j: (i, 0)) for _ in range(4)]
    r_specs = [pl.BlockSpec((tm, tn), functools.partial(
        lambda i, j, base: (i, base + j), base=(r_col0 + br * d) // tn)) for br in range(4)]
    del nj
    return pl.pallas_call(
        _merge_kernel,
        grid=(n // tm, d // tn),
        in_specs=y_specs + r_specs + [pl.BlockSpec((4, w, tn), lambda i, j: (0, 0, j))],
        out_specs=pl.BlockSpec((tm, tn), lambda i, j: (i, j)),
        out_shape=jax.ShapeDtypeStruct((n, d), BF16),
        compiler_params=_cparams(("parallel", "parallel")),
        name="merge",
    )(*ys, act, act, act, act, wb)


def _ceil_div_unroll(n):
    return (n + (ATT_UNROLL - 1)) >> (ATT_UNROLL.bit_length() - 1)


def _two_pass_attention(q_list, k_ref, vt_ref, sc_ref, acc_ref, bias_fn, trips, blk, m0_list, l0_list):
    heads = range(len(q_list))
    hd = HEAD_DIM
    span = ATT_UNROLL * blk

    def pass1(j, ms):
        base = pl.multiple_of(j * span, span)
        out = []
        for h in heads:
            s = _dot_nt(k_ref[pl.ds(base, span), h * hd:(h + 1) * hd], q_list[h])
            m = ms[h]
            for u in range(ATT_UNROLL):
                su = s[u * blk:(u + 1) * blk, :] + bias_fn(h, j * ATT_UNROLL + u)
                sc_ref[h, pl.ds(pl.multiple_of(base + u * blk, blk), blk), :] = su
                m = jnp.maximum(m, jnp.max(su, axis=0, keepdims=True))
            out.append(m)
        return tuple(out)

    ms = lax.fori_loop(0, trips, pass1, tuple(m0_list))

    ls = []
    for h in heads:
        alpha = jnp.exp2(m0_list[h] - ms[h])
        ls.append(alpha * l0_list[h])
        acc_ref[h] = alpha * acc_ref[h]

    def pass2(j, ls):
        base = pl.multiple_of(j * span, span)
        out = []
        for h in heads:
            p = jnp.exp2(sc_ref[h, pl.ds(base, span), :] - ms[h])
            out.append(ls[h] + jnp.sum(p, axis=0, keepdims=True))
            acc_ref[h] += _dot(vt_ref[h * hd:(h + 1) * hd, pl.ds(base, span)], p.astype(BF16))
        return tuple(out)

    return lax.fori_loop(0, trips, pass2, tuple(ls))


def _moba_kernel(q_ref, k_ref, vt_ref, g_ref, o_ref, kmean_ref, selb_ref, sc_ref, acc_ref, *, nblk):
    i = pl.program_id(2)
    blk = MOBA_BLOCK
    hd = HEAD_DIM
    heads = range(HEADS_PER_STEP)

    @pl.when(i == 0)
    def _():
        for h in heads:
            kf = k_ref[:, h * hd:(h + 1) * hd].astype(F32).reshape(nblk, blk, hd)
            kmean_ref[h] = jnp.mean(kf, axis=1)

    off = pl.multiple_of(i * blk, blk)
    q_list, m0_list, l0_list = [], [], []
    for h in heads:
        q = q_ref[:, h * hd:(h + 1) * hd]
        q_list.append(q)
        k1, k2, k3 = _split3(kmean_ref[h])
        bs = _dot_nt(k1, q) + _dot_nt(k2, q) + _dot_nt(k3, q)
        n_iota = lax.broadcasted_iota(jnp.int32, bs.shape, 0)
        past = n_iota < i
        left = jnp.where(past, bs, -jnp.inf)
        bias = jnp.full(bs.shape, _NEG, F32)
        for _ in range(MOBA_TOPK):
            top = jnp.max(left, axis=0, keepdims=True)
            first = jnp.min(jnp.where(left == top, n_iota, nblk), axis=0, keepdims=True)
            hit = n_iota == first
            bias = jnp.where(hit & past, _ZERO, bias)
            left = jnp.where(hit, -jnp.inf, left)
        selb_ref[h] = bias

        st = _dot_nt(k_ref[pl.ds(off, blk), h * hd:(h + 1) * hd], q)
        kpos = lax.broadcasted_iota(jnp.int32, st.shape, 0)
        qpos = lax.broadcasted_iota(jnp.int32, st.shape, 1)
        st = jnp.where(kpos <= qpos, st, _NEG)
        m0 = jnp.max(st, axis=0, keepdims=True)
        p0 = jnp.exp2(st - m0)
        m0_list.append(m0)
        l0_list.append(jnp.sum(p0, axis=0, keepdims=True))
        acc_ref[h] = _dot(vt_ref[h * hd:(h + 1) * hd, pl.ds(off, blk)], p0.astype(BF16))

    ls = _two_pass_attention(
        q_list, k_ref, vt_ref, sc_ref, acc_ref, lambda h, n: selb_ref[h, pl.ds(n, 1), :],
        _ceil_div_unroll(i), blk, m0_list, l0_list)
    for h in heads:
        o = (acc_ref[h] / ls[h]).T
        g = g_ref[:, h * hd:(h + 1) * hd].astype(F32)
        o_ref[:, h * hd:(h + 1) * hd] = (o * _silu(g)).astype(o_ref.dtype)


def _moba(act, act_t, batch, seq, q_col, k_col, g_col, v_row):
    nblk = seq // MOBA_BLOCK
    n = act.shape[0]
    blk = MOBA_BLOCK
    hp = HEADS_PER_STEP
    w = hp * HEAD_DIM
    q_cb, k_cb, g_cb, v_rb = q_col // w, k_col // w, g_col // w, v_row // w
    return pl.pallas_call(
        functools.partial(_moba_kernel, nblk=nblk),
        grid=(batch, MOBA_HEADS // hp, nblk),
        in_specs=[pl.BlockSpec((blk, w), lambda b, h, i: (b * nblk + i, q_cb + h)),
                  pl.BlockSpec((seq, w), lambda b, h, i: (b, k_cb + h)),
                  pl.BlockSpec((w, seq), lambda b, h, i: (v_rb + h, b)),
                  pl.BlockSpec((blk, w), lambda b, h, i: (b * nblk + i, g_cb + h))],
        out_specs=pl.BlockSpec((blk, w), lambda b, h, i: (b * nblk + i, h)),
        out_shape=jax.ShapeDtypeStruct((n, BRANCH_W), BF16),
        scratch_shapes=[pltpu.VMEM((hp, nblk, HEAD_DIM), F32),
                        pltpu.VMEM((hp, nblk, blk), F32),
                        pltpu.VMEM((hp, seq, blk), F32),
                        pltpu.VMEM((hp, HEAD_DIM, blk), F32)],
        compiler_params=_cparams(("parallel", "parallel", "arbitrary")),
        name="moba",
    )(act, act, act_t, act)


def _conv_kernel(cb_ref, cc_ref, ch_ref, cg_ref, w_ref, bias_ref, o_ref):
    u = cc_ref[...].astype(F32) * ch_ref[...].astype(F32)
    t = lax.broadcasted_iota(jnp.int32, u.shape, 0)
    u1 = jnp.where(t >= 1, pltpu.roll(u, 1, 0), 0.0)
    u2 = jnp.where(t >= 2, pltpu.roll(u, 2, 0), 0.0)
    w = w_ref[...]
    conv = u2 * w[0:1, :] + u1 * w[1:2, :] + u * w[2:3, :] + bias_ref[...]
    o_ref[...] = (cb_ref[...].astype(F32) * conv * _silu(cg_ref[...].astype(F32))).astype(o_ref.dtype)


def _conv(act, conv_w, conv_b, batch, seq, cb_cb, cc_cb, ch_cb, cg_cb):
    n = act.shape[0]
    ncb = BRANCH_W // LANE
    kw = conv_w.shape[0]
    wpad = jnp.zeros((8, BRANCH_W), F32).at[:kw].set(conv_w)

    def spec(base):
        return pl.BlockSpec((seq, LANE), lambda b, c: (b, base + c))

    return pl.pallas_call(
        _conv_kernel,
        grid=(batch, ncb),
        in_specs=[spec(cb_cb), spec(cc_cb), spec(ch_cb), spec(cg_cb),
                  pl.BlockSpec((8, LANE), lambda b, c: (0, c)),
                  pl.BlockSpec((1, LANE), lambda b, c: (0, c))],
        out_specs=pl.BlockSpec((seq, LANE), lambda b, c: (b, c)),
        out_shape=jax.ShapeDtypeStruct((n, BRANCH_W), BF16),
        compiler_params=_cparams(("parallel", "parallel")),
        name="conv",
    )(act, act, act, act, wpad, conv_b.reshape(1, BRANCH_W))


def _to_key(f):
    b = pltpu.bitcast(f, jnp.int32)
    return jnp.where(b >= 0, b, b ^ jnp.int32(0x7FFFFFFF))


def _from_key(k):
    return pltpu.bitcast(jnp.where(k >= 0, k, k ^ jnp.int32(0x7FFFFFFF)), F32)


def _dsa_kernel(qc_ref, kc_ref, wt_ref, q_ref, k_ref, vt_ref, g_ref, o_ref, st_ref, tri_ref, sc_ref,
                acc_ref, *, topk, tq):
    i = pl.program_id(1)
    nch = i + 1
    trips = _ceil_div_unroll(nch)
    idx_scale = (IDX_DIM ** -0.5) * (IDX_HEADS ** -0.5)
    kf = float(topk)

    def chunk(c):
        return pl.ds(pl.multiple_of(c * tq, tq), tq)

    def count_ge(thr):
        def body(c, acc):
            ge = _ind(st_ref[chunk(c), :] >= thr)
            return acc + jnp.sum(ge.reshape(tq // 8, 8, tq), axis=0)
        acc = lax.fori_loop(0, nch, body, jnp.zeros((8, tq), F32))
        return jnp.sum(acc, axis=0, keepdims=True)

    @pl.when(pl.program_id(2) == 0)
    def _select():
        def score_body(c, _):
            kc = kc_ref[chunk(c), :]
            sc = jnp.zeros((tq, tq), F32)
            for j in range(IDX_HEADS):
                lg = _dot_nt(kc, qc_ref[:, j * IDX_SLOT:(j + 1) * IDX_SLOT])
                sc = sc + (wt_ref[j:j + 1, :] * idx_scale) * jnp.maximum(lg, 0.0)
            kpos = c * tq + lax.broadcasted_iota(jnp.int32, sc.shape, 0)
            qpos = i * tq + lax.broadcasted_iota(jnp.int32, sc.shape, 1)
            st_ref[chunk(c), :] = jnp.where(kpos <= qpos, sc, -jnp.inf)
            return 0
        lax.fori_loop(0, nch, score_body, 0)

        lo0 = _to_key(jnp.full((1, tq), -jnp.inf, F32))
        hi0 = _to_key(jnp.full((1, tq), jnp.inf, F32))

        def bis_body(_, carry):
            lo, hi = carry
            mid = (lo >> 1) + (hi >> 1) + (lo & hi & 1)
            ge = count_ge(_from_key(mid)) >= kf
            return jnp.where(ge, mid, lo), jnp.where(ge, hi, mid)

        _, hi_k = lax.fori_loop(0, 32, bis_body, (lo0, hi0))

        def walk_cond(carry):
            _, _, done = carry
            return jnp.min(done) < 0.5

        def walk_body(carry):
            hi_f, thr, done = carry

            def mx_body(c, acc):
                s = st_ref[chunk(c), :]
                v = jnp.where(s < hi_f, s, -jnp.inf)
                return jnp.maximum(acc, jnp.max(v.reshape(tq // 8, 8, tq), axis=0))
            mx = lax.fori_loop(0, nch, mx_body, jnp.full((8, tq), -jnp.inf, F32))
            v = jnp.max(mx, axis=0, keepdims=True)
            ok = count_ge(v) >= kf
            is_done = done > 0.5
            thr_n = jnp.where(is_done, thr, v)
            hi_n = jnp.where(is_done, hi_f, v)
            done_n = _ind(is_done | ok)
            return hi_n, thr_n, done_n

        hi_f0 = _from_key(hi_k)
        _, thr, _ = lax.while_loop(
            walk_cond, walk_body,
            (hi_f0, jnp.full((1, tq), -jnp.inf, F32), jnp.zeros((1, tq), F32)))

        def gt_body(c, acc):
            gt = _ind(st_ref[chunk(c), :] > thr)
            return acc + jnp.sum(gt.reshape(tq // 8, 8, tq), axis=0)
        n_gt = jnp.sum(lax.fori_loop(0, nch, gt_body, jnp.zeros((8, tq), F32)),
                       axis=0, keepdims=True)
        need = kf - n_gt

        r = lax.broadcasted_iota(jnp.int32, (tq, tq), 0)
        cidx = lax.broadcasted_iota(jnp.int32, (tq, tq), 1)
        tri_ref[...] = _ind(cidx < r).astype(BF16)

        def bias_body(c, carry):
            s = st_ref[chunk(c), :]
            eq = s == thr
            eqf = _ind(eq)
            before = _dot(tri_ref[...], eqf.astype(BF16)) + carry
            kpos = c * tq + lax.broadcasted_iota(jnp.int32, s.shape, 0)
            qpos = i * tq + lax.broadcasted_iota(jnp.int32, s.shape, 1)
            selected = ((s > thr) | (eq & (before < need))) & (kpos <= qpos)
            st_ref[chunk(c), :] = jnp.where(selected, _ZERO, _NEG)
            return carry + jnp.sum(eqf, axis=0, keepdims=True)
        lax.fori_loop(0, nch, bias_body, jnp.zeros((1, tq), F32))

        def pad_body(c, _):
            st_ref[chunk(c), :] = jnp.full((tq, tq), _NEG, F32)
            return 0
        lax.fori_loop(nch, trips * ATT_UNROLL, pad_body, 0)

    hd = HEAD_DIM
    heads = range(HEADS_PER_STEP)
    q_list = [q_ref[:, h * hd:(h + 1) * hd] for h in heads]
    for h in heads:
        acc_ref[h] = jnp.zeros((hd, tq), F32)
    ls = _two_pass_attention(
        q_list, k_ref, vt_ref, sc_ref, acc_ref, lambda h, c: st_ref[chunk(c), :], trips, tq,
        [jnp.full((1, tq), _NEG, F32)] * HEADS_PER_STEP, [jnp.zeros((1, tq), F32)] * HEADS_PER_STEP)
    for h in heads:
        o = (acc_ref[h] / ls[h]).T
        g = g_ref[:, h * hd:(h + 1) * hd].astype(F32)
        o_ref[:, h * hd:(h + 1) * hd] = (o * _silu(g)).astype(o_ref.dtype)


def _dsa(act, act_t, cat, wt, batch, seq, q_col, k_col, g_col, v_row, tq=256):
    n = act.shape[0]
    nq = seq // tq
    topk = min(DSA_TOPK_MAX, seq // 4)
    n_qcat = IDX_HEADS * IDX_SLOT
    hp = HEADS_PER_STEP
    w = hp * HEAD_DIM
    q_cb, k_cb, g_cb, v_rb = q_col // w, k_col // w, g_col // w, v_row // w
    return pl.pallas_call(
        functools.partial(_dsa_kernel, topk=topk, tq=tq),
        grid=(batch, nq, DSA_HEADS // hp),
        in_specs=[pl.BlockSpec((tq, n_qcat), lambda b, i, h: (b * nq + i, 0)),
                  pl.BlockSpec((seq, IDX_SLOT), lambda b, i, h: (b, n_qcat // IDX_SLOT)),
                  pl.BlockSpec((8, tq), lambda b, i, h: (0, b * nq + i)),
                  pl.BlockSpec((tq, w), lambda b, i, h: (b * nq + i, q_cb + h)),
                  pl.BlockSpec((seq, w), lambda b, i, h: (b, k_cb + h)),
                  pl.BlockSpec((w, seq), lambda b, i, h: (v_rb + h, b)),
                  pl.BlockSpec((tq, w), lambda b, i, h: (b * nq + i, g_cb + h))],
        out_specs=pl.BlockSpec((tq, w), lambda b, i, h: (b * nq + i, h)),
        out_shape=jax.ShapeDtypeStruct((n, BRANCH_W), BF16),
        scratch_shapes=[pltpu.VMEM((seq, tq), F32),
                        pltpu.VMEM((tq, tq), BF16),
                        pltpu.VMEM((hp, seq, tq), F32),
                        pltpu.VMEM((hp, HEAD_DIM, tq), F32)],
        compiler_params=_cparams(("parallel", "arbitrary", "arbitrary")),
        name="dsa",
    )(cat, cat, wt, act, act, act_t, act)


def _mem_kernel(q_ref, mk_ref, mvt_ref, g_ref, o_ref):
    s = _dot_nt(mk_ref[...], q_ref[...])
    m = jnp.max(s, axis=0, keepdims=True)
    p = jnp.exp2(s - m)
    l = jnp.sum(p, axis=0, keepdims=True)
    o = (_dot(mvt_ref[...], p.astype(BF16)) / l).T
    o_ref[...] = (o * _silu(g_ref[...].astype(F32))).astype(o_ref.dtype)


def _mem_attn(act, mk, mvt, batch, seq, mem_len, q_cb, g_cb, tq=512):
    n = act.shape[0]
    nq = seq // tq
    hd = MEM_HEAD_DIM
    return pl.pallas_call(
        _mem_kernel,
        grid=(batch, nq, MEM_HEADS),
        in_specs=[pl.BlockSpec((tq, hd), lambda b, i, h: (b * nq + i, q_cb + h)),
                  pl.BlockSpec((mem_len, hd), lambda b, i, h: (b, h)),
                  pl.BlockSpec((hd, mem_len), lambda b, i, h: (h, b)),
                  pl.BlockSpec((tq, hd), lambda b, i, h: (b * nq + i, g_cb + h))],
        out_specs=pl.BlockSpec((tq, hd), lambda b, i, h: (b * nq + i, h)),
        out_shape=jax.ShapeDtypeStruct((n, BRANCH_W), BF16),
        compiler_params=_cparams(("parallel", "parallel", "parallel")),
        name="mem_attn",
    )(act, mk, mvt, act)


def _layer_weights(w_in, w_mem_kv, w_branch, w_out):
    bw = BRANCH_W
    d = w_in.shape[0]
    o_iq = 14 * bw
    o_ik = o_iq + IDX_HEADS * IDX_DIM
    o_iw = o_ik + IDX_DIM
    o_r = o_iw + IDX_HEADS
    w_tok = jnp.concatenate(
        [w_in[:, 0:2 * bw], w_in[:, 3 * bw:10 * bw], w_in[:, 11 * bw:14 * bw], w_in[:, o_r:]],
        axis=1).astype(BF16)
    w_vt = jnp.concatenate([w_in[:, 2 * bw:3 * bw], w_in[:, 10 * bw:11 * bw]], axis=1).T.astype(BF16)
    zeros = jnp.zeros((d, IDX_SLOT - 3 * IDX_DIM), F32)
    parts = []
    for j in range(IDX_HEADS):
        wq = w_in[:, o_iq + j * IDX_DIM:o_iq + (j + 1) * IDX_DIM]
        parts += [wq, wq, wq, zeros]
    wk = w_in[:, o_ik:o_iw]
    parts += [wk, wk, wk, zeros]
    w_idx = jnp.concatenate(parts, axis=1)
    w_idx_hi, w_idx_lo = _split2(w_idx)
    ww = jnp.zeros((8, d), F32).at[:IDX_HEADS].set(w_in[:, o_iw:o_r].T)
    ww_hi, ww_lo = _split2(ww)
    w_mk = w_mem_kv[:, :bw].astype(BF16)
    w_mvt = w_mem_kv[:, bw:].T.astype(BF16)
    return dict(w_tok=w_tok, w_vt=w_vt, w_idx_hi=w_idx_hi, w_idx_lo=w_idx_lo, ww_hi=ww_hi,
                ww_lo=ww_lo, w_mk=w_mk, w_mvt=w_mvt, w_branch=w_branch.astype(BF16),
                w_out=w_out.astype(BF16))


def _layer(x2, mem2, batch, seq, mem_len, ln_g, conv_w, conv_b, mem_ln_g, w):
    nb = BRANCH_W // LANE
    (A_Q, A_K, A_G, C_B, C_C, C_H, C_G, S_Q, S_K, S_G, M_Q, M_G) = range(12)
    r_col0 = 12 * BRANCH_W

    log2e = float(np.log2(np.e))
    colscale = np.ones((1, w["w_tok"].shape[1]), np.float32)
    for col, hdim in ((A_Q, HEAD_DIM), (S_Q, HEAD_DIM), (M_Q, MEM_HEAD_DIM)):
        colscale[:, col * BRANCH_W:(col + 1) * BRANCH_W] = hdim ** -0.5 * log2e

    xn, cat, wt = _prep(x2, ln_g, w["w_idx_hi"], w["w_idx_lo"], w["ww_hi"], w["ww_lo"])
    act = _mm_nn_colscale(xn, w["w_tok"], jnp.asarray(colscale), BF16, tm=1024, tn=1024,
                          name="proj_tok")
    act_t = _mm_nt(w["w_vt"], xn, BF16, tm=1024, tn=1024, name="proj_vt")

    bw = BRANCH_W
    y_a = _moba(act, act_t, batch, seq, A_Q * bw, A_K * bw, A_G * bw, 0)
    y_c = _conv(act, conv_w, conv_b, batch, seq, C_B * nb, C_C * nb, C_H * nb, C_G * nb)
    y_s = _dsa(act, act_t, cat, wt, batch, seq, S_Q * bw, S_K * bw, S_G * bw, bw)

    mem_n = _rmsnorm(mem2, mem_ln_g, BF16)
    mk = _mm_nn(mem_n, w["w_mk"], BF16, tm=mem2.shape[0], tn=512, name="mem_k")
    mvt = _mm_nt(w["w_mvt"], mem_n, BF16, tm=512, tn=mem2.shape[0], name="mem_vt")
    mb = BRANCH_W // MEM_HEAD_DIM
    y_m = _mem_attn(act, mk, mvt, batch, seq, mem_len, M_Q * mb, M_G * mb)

    merged = _merge((y_a, y_c, y_s, y_m), act, r_col0, w["w_branch"], tm=512, tn=512)
    return _mm_res(merged, w["w_out"], x2, tm=512, tn=1024)


def kernel(x, mem, ln_g, w_in, conv_w, conv_b, mem_ln_g, w_mem_kv, w_branch, w_out, final_g):
    batch, seq, d = x.shape
    mem_len = mem.shape[1]
    x2 = x.reshape(batch * seq, d)
    mem2 = mem.reshape(batch * mem_len, d)
    for layer in range(ln_g.shape[0]):
        w = _layer_weights(w_in[layer], w_mem_kv[layer], w_branch[layer], w_out[layer])
        x2 = _layer(x2, mem2, batch, seq, mem_len, ln_g[layer], conv_w[layer], conv_b[layer],
                    mem_ln_g[layer], w)
    return _rmsnorm(x2, final_g, x.dtype).reshape(batch, seq, d)
```

```python
import functools

import numpy as np
import jax
import jax.numpy as jnp
from jax import lax
from jax.experimental import pallas as pl
from jax.experimental.pallas import tpu as pltpu

EPS = 1e-6
BRANCH_W = 1024
MOBA_HEADS = 8
MOBA_BLOCK = 256
MOBA_TOPK = 3
DSA_HEADS = 8
DSA_TOPK_MAX = 256
IDX_HEADS = 4
IDX_DIM = 64
MEM_HEADS = 4
HEAD_DIM = 128
MEM_HEAD_DIM = 256
IDX_SLOT = 256
NEG = -1e30
ATT_UNROLL = 4
HEADS_PER_STEP = 4
LANE = 128
VMEM_LIMIT = 56 * 1024 * 1024

F32 = jnp.float32
BF16 = jnp.bfloat16
_ONE = np.float32(1.0)
_ZERO = np.float32(0.0)
_NEG = np.float32(NEG)


def _cparams(sem):
    return pltpu.CompilerParams(dimension_semantics=sem, vmem_limit_bytes=VMEM_LIMIT)


def _dot(a, b):
    return jnp.dot(a, b, preferred_element_type=F32)


def _dot_nt(a, b):
    return lax.dot_general(a, b, (((1,), (1,)), ((), ())), preferred_element_type=F32)


def _split2(v):
    hi = v.astype(BF16)
    lo = (v - hi.astype(F32)).astype(BF16)
    return hi, lo


def _split3(v):
    h1 = v.astype(BF16)
    r1 = v - h1.astype(F32)
    h2 = r1.astype(BF16)
    h3 = (r1 - h2.astype(F32)).astype(BF16)
    return h1, h2, h3


def _ind(cond):
    return jnp.where(cond, _ONE, _ZERO)


def _silu(g):
    return g * jax.nn.sigmoid(g)


def _rms(x, g):
    var = jnp.mean(x * x, axis=-1, keepdims=True)
    return (x * lax.rsqrt(var + EPS)) * g


def _rmsnorm_kernel(x_ref, g_ref, o_ref):
    o_ref[...] = _rms(x_ref[...], g_ref[...]).astype(o_ref.dtype)


def _rmsnorm(x, g, out_dtype, tm=256):
    n, d = x.shape
    return pl.pallas_call(
        _rmsnorm_kernel,
        grid=(n // tm,),
        in_specs=[pl.BlockSpec((tm, d), lambda i: (i, 0)),
                  pl.BlockSpec((1, d), lambda i: (0, 0))],
        out_specs=pl.BlockSpec((tm, d), lambda i: (i, 0)),
        out_shape=jax.ShapeDtypeStruct((n, d), out_dtype),
        compiler_params=_cparams(("parallel",)),
        name="rmsnorm",
    )(x, g.reshape(1, d))


def _prep_kernel(x_ref, g_ref, whi_ref, wlo_ref, phi_ref, plo_ref, xn_ref, cat_ref, wt_ref):
    xn = _rms(x_ref[...], g_ref[...])
    hi, lo = _split2(xn)
    xn_ref[...] = hi
    acc = _dot(hi, whi_ref[...]) + _dot(hi, wlo_ref[...]) + _dot(lo, whi_ref[...])
    vh, vl = _split2(acc)
    cat_ref[...] = (_dot(vh, phi_ref[...]) + _dot(vl, plo_ref[...])).astype(BF16)
    gate_row = IDX_DIM
    wt_ref[...] = acc[:, IDX_W_IN - LANE:].T[gate_row:gate_row + 8, :]


IDX_W_IN = 384


def _idx_placement():
    n_out = (IDX_HEADS + 1) * IDX_SLOT
    p_hi = np.zeros((IDX_W_IN, n_out), np.float32)
    p_lo = np.zeros((IDX_W_IN, n_out), np.float32)
    e = np.arange(IDX_DIM)
    for j in range(IDX_HEADS):
        src, dst = j * IDX_DIM + e, j * IDX_SLOT + e
        p_hi[src, dst] = 1
        p_hi[src, dst + IDX_DIM] = 1
        p_lo[src, dst + 2 * IDX_DIM] = 1
    src, dst = IDX_HEADS * IDX_DIM + e, IDX_HEADS * IDX_SLOT + e
    p_hi[src, dst] = 1
    p_lo[src, dst + IDX_DIM] = 1
    p_hi[src, dst + 2 * IDX_DIM] = 1
    return jnp.asarray(p_hi, BF16), jnp.asarray(p_lo, BF16)


def _prep(x, g, whi, wlo, tm=256):
    n, d = x.shape
    p_hi, p_lo = _idx_placement()
    co = p_hi.shape[1]
    return pl.pallas_call(
        _prep_kernel,
        grid=(n // tm,),
        in_specs=[pl.BlockSpec((tm, d), lambda i: (i, 0)),
                  pl.BlockSpec((1, d), lambda i: (0, 0)),
                  pl.BlockSpec((d, IDX_W_IN), lambda i: (0, 0)),
                  pl.BlockSpec((d, IDX_W_IN), lambda i: (0, 0)),
                  pl.BlockSpec((IDX_W_IN, co), lambda i: (0, 0)),
                  pl.BlockSpec((IDX_W_IN, co), lambda i: (0, 0))],
        out_specs=[pl.BlockSpec((tm, d), lambda i: (i, 0)),
                   pl.BlockSpec((tm, co), lambda i: (i, 0)),
                   pl.BlockSpec((8, tm), lambda i: (0, i))],
        out_shape=[jax.ShapeDtypeStruct((n, d), BF16),
                   jax.ShapeDtypeStruct((n, co), BF16),
                   jax.ShapeDtypeStruct((8, n), F32)],
        compiler_params=_cparams(("parallel",)),
        name="prep",
    )(x, g.reshape(1, d), whi, wlo, p_hi, p_lo)


def _transpose_cast_kernel(w_ref, o_ref):
    o_ref[...] = w_ref[...].T.astype(o_ref.dtype)


def _transpose_cast(w, name, t=512):
    d, n = w.shape
    return pl.pallas_call(
        _transpose_cast_kernel,
        grid=(n // t, d // t),
        in_specs=[pl.BlockSpec((t, t), lambda i, j: (j, i))],
        out_specs=pl.BlockSpec((t, t), lambda i, j: (i, j)),
        out_shape=jax.ShapeDtypeStruct((n, d), BF16),
        compiler_params=_cparams(("parallel", "parallel")),
        name=name,
    )(w)


def _mm_nn_kernel(a_ref, b_ref, o_ref):
    o_ref[...] = _dot(a_ref[...], b_ref[...]).astype(o_ref.dtype)


def _mm_nn(a, b, out_dtype, tm, tn, name):
    m, k = a.shape
    n = b.shape[1]
    return pl.pallas_call(
        _mm_nn_kernel,
        grid=(m // tm, n // tn),
        in_specs=[pl.BlockSpec((tm, k), lambda i, j: (i, 0)),
                  pl.BlockSpec((k, tn), lambda i, j: (0, j))],
        out_specs=pl.BlockSpec((tm, tn), lambda i, j: (i, j)),
        out_shape=jax.ShapeDtypeStruct((m, n), out_dtype),
        compiler_params=_cparams(("parallel", "parallel")),
        name=name,
    )(a, b)


def _mm_nn_colscale_kernel(a_ref, b_ref, s_ref, o_ref):
    o_ref[...] = (_dot(a_ref[...], b_ref[...]) * s_ref[...]).astype(o_ref.dtype)


def _mm_nn_colscale(a, b, colscale, out_dtype, tm, tn, name):
    m, k = a.shape
    n = b.shape[1]
    return pl.pallas_call(
        _mm_nn_colscale_kernel,
        grid=(m // tm, n // tn),
        in_specs=[pl.BlockSpec((tm, k), lambda i, j: (i, 0)),
                  pl.BlockSpec((k, tn), lambda i, j: (0, j)),
                  pl.BlockSpec((1, tn), lambda i, j: (0, j))],
        out_specs=pl.BlockSpec((tm, tn), lambda i, j: (i, j)),
        out_shape=jax.ShapeDtypeStruct((m, n), out_dtype),
        compiler_params=_cparams(("parallel", "parallel")),
        name=name,
    )(a, b, colscale)


def _mm_nt_kernel(a_ref, b_ref, o_ref):
    o_ref[...] = _dot_nt(a_ref[...], b_ref[...]).astype(o_ref.dtype)


def _mm_nt(a, b, out_dtype, tm, tn, name):
    m, k = a.shape
    n = b.shape[0]
    return pl.pallas_call(
        _mm_nt_kernel,
        grid=(n // tn, m // tm),
        in_specs=[pl.BlockSpec((tm, k), lambda j, i: (i, 0)),
                  pl.BlockSpec((tn, k), lambda j, i: (j, 0))],
        out_specs=pl.BlockSpec((tm, tn), lambda j, i: (i, j)),
        out_shape=jax.ShapeDtypeStruct((m, n), out_dtype),
        compiler_params=_cparams(("parallel", "parallel")),
        name=name,
    )(a, b)


def _mm_res_kernel(a_ref, b_ref, r_ref, o_ref):
    o_ref[...] = r_ref[...] + _dot(a_ref[...], b_ref[...])


def _mm_res(a, b, res, tm, tn):
    m, k = a.shape
    n = b.shape[1]
    return pl.pallas_call(
        _mm_res_kernel,
        grid=(m // tm, n // tn),
        in_specs=[pl.BlockSpec((tm, k), lambda i, j: (i, 0)),
                  pl.BlockSpec((k, tn), lambda i, j: (0, j)),
                  pl.BlockSpec((tm, tn), lambda i, j: (i, j))],
        out_specs=pl.BlockSpec((tm, tn), lambda i, j: (i, j)),
        out_shape=jax.ShapeDtypeStruct((m, n), F32),
        compiler_params=_cparams(("parallel", "parallel")),
        name="out_proj",
    )(a, b, res)


def _merge_kernel(ya_ref, yc_ref, ys_ref, ym_ref, ra_ref, rc_ref, rs_ref, rm_ref, wb_ref, o_ref):
    ys = (ya_ref, yc_ref, ys_ref, ym_ref)
    rs = (ra_ref, rc_ref, rs_ref, rm_ref)
    acc = None
    for br in range(4):
        z = _dot(ys[br][...], wb_ref[br])
        term = jax.nn.sigmoid(rs[br][...].astype(F32)) * z
        acc = term if acc is None else acc + term
    o_ref[...] = acc.astype(o_ref.dtype)


def _merge(ys, act, r_col0, wb, tm, tn):
    n, w = ys[0].shape
    d = wb.shape[2]
    nj = d // tn
    y_specs = [pl.BlockSpec((tm, w), lambda i, j: (i, 0)) for _ in range(4)]
    r_specs = [pl.BlockSpec((tm, tn), functools.partial(
        lambda i, j, base: (i, base + j), base=(r_col0 + br * d) // tn)) for br in range(4)]
    del nj
    return pl.pallas_call(
        _merge_kernel,
        grid=(n // tm, d // tn),
        in_specs=y_specs + r_specs + [pl.BlockSpec((4, w, tn), lambda i, j: (0, 0, j))],
        out_specs=pl.BlockSpec((tm, tn), lambda i, j: (i, j)),
        out_shape=jax.ShapeDtypeStruct((n, d), BF16),
        compiler_params=_cparams(("parallel", "parallel")),
        name="merge",
    )(*ys, act, act, act, act, wb)


def _ceil_div_unroll(n):
    return (n + (ATT_UNROLL - 1)) >> (ATT_UNROLL.bit_length() - 1)


def _two_pass_attention(q_list, k_ref, vt_ref, sc_ref, acc_ref, bias_fn, trips, blk, m0_list, l0_list):
    heads = range(len(q_list))
    hd = HEAD_DIM
    span = ATT_UNROLL * blk

    def pass1(j, ms):
        base = pl.multiple_of(j * span, span)
        out = []
        for h in heads:
            s = _dot_nt(k_ref[pl.ds(base, span), h * hd:(h + 1) * hd], q_list[h])
            m = ms[h]
            for u in range(ATT_UNROLL):
                su = s[u * blk:(u + 1) * blk, :] + bias_fn(h, j * ATT_UNROLL + u)
                sc_ref[h, pl.ds(pl.multiple_of(base + u * blk, blk), blk), :] = su
                m = jnp.maximum(m, jnp.max(su, axis=0, keepdims=True))
            out.append(m)
        return tuple(out)

    ms = lax.fori_loop(0, trips, pass1, tuple(m0_list))

    ls = []
    for h in heads:
        alpha = jnp.exp2(m0_list[h] - ms[h])
        ls.append(alpha * l0_list[h])
        acc_ref[h] = alpha * acc_ref[h]

    def pass2(j, ls):
        base = pl.multiple_of(j * span, span)
        out = []
        for h in heads:
            p = jnp.exp2(sc_ref[h, pl.ds(base, span), :] - ms[h])
            out.append(ls[h] + jnp.sum(p, axis=0, keepdims=True))
            acc_ref[h] += _dot(vt_ref[h * hd:(h + 1) * hd, pl.ds(base, span)], p.astype(BF16))
        return tuple(out)

    return lax.fori_loop(0, trips, pass2, tuple(ls))


def _moba_kernel(q_ref, k_ref, vt_ref, g_ref, o_ref, kmean_ref, selb_ref, sc_ref, acc_ref, *, nblk):
    i = pl.program_id(2)
    blk = MOBA_BLOCK
    hd = HEAD_DIM
    heads = range(HEADS_PER_STEP)

    @pl.when(i == 0)
    def _():
        for h in heads:
            kf = k_ref[:, h * hd:(h + 1) * hd].astype(F32).reshape(nblk, blk, hd)
            kmean_ref[h] = jnp.mean(kf, axis=1)

    off = pl.multiple_of(i * blk, blk)
    q_list, m0_list, l0_list = [], [], []
    for h in heads:
        q = q_ref[:, h * hd:(h + 1) * hd]
        q_list.append(q)
        k1, k2, k3 = _split3(kmean_ref[h])
        bs = _dot_nt(k1, q) + _dot_nt(k2, q) + _dot_nt(k3, q)
        n_iota = lax.broadcasted_iota(jnp.int32, bs.shape, 0)
        past = n_iota < i
        left = jnp.where(past, bs, -jnp.inf)
        bias = jnp.full(bs.shape, _NEG, F32)
        for _ in range(MOBA_TOPK):
            top = jnp.max(left, axis=0, keepdims=True)
            first = jnp.min(jnp.where(left == top, n_iota, nblk), axis=0, keepdims=True)
            hit = n_iota == first
            bias = jnp.where(hit & past, _ZERO, bias)
            left = jnp.where(hit, -jnp.inf, left)
        selb_ref[h] = bias

        st = _dot_nt(k_ref[pl.ds(off, blk), h * hd:(h + 1) * hd], q)
        kpos = lax.broadcasted_iota(jnp.int32, st.shape, 0)
        qpos = lax.broadcasted_iota(jnp.int32, st.shape, 1)
        st = jnp.where(kpos <= qpos, st, _NEG)
        m0 = jnp.max(st, axis=0, keepdims=True)
        p0 = jnp.exp2(st - m0)
        m0_list.append(m0)
        l0_list.append(jnp.sum(p0, axis=0, keepdims=True))
        acc_ref[h] = _dot(vt_ref[h * hd:(h + 1) * hd, pl.ds(off, blk)], p0.astype(BF16))

    ls = _two_pass_attention(
        q_list, k_ref, vt_ref, sc_ref, acc_ref, lambda h, n: selb_ref[h, pl.ds(n, 1), :],
        _ceil_div_unroll(i), blk, m0_list, l0_list)
    for h in heads:
        o = (acc_ref[h] / ls[h]).T
        g = g_ref[:, h * hd:(h + 1) * hd].astype(F32)
        o_ref[:, h * hd:(h + 1) * hd] = (o * _silu(g)).astype(o_ref.dtype)


def _moba(act, act_t, batch, seq, q_col, k_col, g_col, v_row):
    nblk = seq // MOBA_BLOCK
    n = act.shape[0]
    blk = MOBA_BLOCK
    hp = HEADS_PER_STEP
    w = hp * HEAD_DIM
    q_cb, k_cb, g_cb, v_rb = q_col // w, k_col // w, g_col // w, v_row // w
    return pl.pallas_call(
        functools.partial(_moba_kernel, nblk=nblk),
        grid=(batch, MOBA_HEADS // hp, nblk),
        in_specs=[pl.BlockSpec((blk, w), lambda b, h, i: (b * nblk + i, q_cb + h)),
                  pl.BlockSpec((seq, w), lambda b, h, i: (b, k_cb + h)),
                  pl.BlockSpec((w, seq), lambda b, h, i: (v_rb + h, b)),
                  pl.BlockSpec((blk, w), lambda b, h, i: (b * nblk + i, g_cb + h))],
        out_specs=pl.BlockSpec((blk, w), lambda b, h, i: (b * nblk + i, h)),
        out_shape=jax.ShapeDtypeStruct((n, BRANCH_W), BF16),
        scratch_shapes=[pltpu.VMEM((hp, nblk, HEAD_DIM), F32),
                        pltpu.VMEM((hp, nblk, blk), F32),
                        pltpu.VMEM((hp, seq, blk), F32),
                        pltpu.VMEM((hp, HEAD_DIM, blk), F32)],
        compiler_params=_cparams(("parallel", "parallel", "arbitrary")),
        name="moba",
    )(act, act, act_t, act)


def _conv_kernel(cb_ref, cc_ref, ch_ref, cg_ref, w_ref, bias_ref, o_ref):
    u = cc_ref[...].astype(F32) * ch_ref[...].astype(F32)
    t = lax.broadcasted_iota(jnp.int32, u.shape, 0)
    u1 = jnp.where(t >= 1, pltpu.roll(u, 1, 0), 0.0)
    u2 = jnp.where(t >= 2, pltpu.roll(u, 2, 0), 0.0)
    w = w_ref[...]
    conv = u2 * w[0:1, :] + u1 * w[1:2, :] + u * w[2:3, :] + bias_ref[...]
    o_ref[...] = (cb_ref[...].astype(F32) * conv * _silu(cg_ref[...].astype(F32))).astype(o_ref.dtype)


def _conv(act, conv_w, conv_b, batch, seq, cb_cb, cc_cb, ch_cb, cg_cb):
    n = act.shape[0]
    ncb = BRANCH_W // LANE
    kw = conv_w.shape[0]
    wpad = jnp.zeros((8, BRANCH_W), F32).at[:kw].set(conv_w)

    def spec(base):
        return pl.BlockSpec((seq, LANE), lambda b, c: (b, base + c))

    return pl.pallas_call(
        _conv_kernel,
        grid=(batch, ncb),
        in_specs=[spec(cb_cb), spec(cc_cb), spec(ch_cb), spec(cg_cb),
                  pl.BlockSpec((8, LANE), lambda b, c: (0, c)),
                  pl.BlockSpec((1, LANE), lambda b, c: (0, c))],
        out_specs=pl.BlockSpec((seq, LANE), lambda b, c: (b, c)),
        out_shape=jax.ShapeDtypeStruct((n, BRANCH_W), BF16),
        compiler_params=_cparams(("parallel", "parallel")),
        name="conv",
    )(act, act, act, act, wpad, conv_b.reshape(1, BRANCH_W))


def _to_key(f):
    b = pltpu.bitcast(f, jnp.int32)
    return jnp.where(b >= 0, b, b ^ jnp.int32(0x7FFFFFFF))


def _from_key(k):
    return pltpu.bitcast(jnp.where(k >= 0, k, k ^ jnp.int32(0x7FFFFFFF)), F32)


def _dsa_kernel(qc_ref, kc_ref, wt_ref, q_ref, k_ref, vt_ref, g_ref, o_ref, st_ref, tri_ref, sc_ref,
                acc_ref, *, topk, tq):
    i = pl.program_id(1)
    nch = i + 1
    trips = _ceil_div_unroll(nch)
    idx_scale = (IDX_DIM ** -0.5) * (IDX_HEADS ** -0.5)
    kf = float(topk)

    def chunk(c):
        return pl.ds(pl.multiple_of(c * tq, tq), tq)

    def count_ge(thr):
        def body(c, acc):
            ge = _ind(st_ref[chunk(c), :] >= thr)
            return acc + jnp.sum(ge.reshape(tq // 8, 8, tq), axis=0)
        acc = lax.fori_loop(0, nch, body, jnp.zeros((8, tq), F32))
        return jnp.sum(acc, axis=0, keepdims=True)

    @pl.when(pl.program_id(2) == 0)
    def _select():
        def score_body(c, _):
            kc = kc_ref[chunk(c), :]
            sc = jnp.zeros((tq, tq), F32)
            for j in range(IDX_HEADS):
                lg = _dot_nt(kc, qc_ref[:, j * IDX_SLOT:(j + 1) * IDX_SLOT])
                sc = sc + (wt_ref[j:j + 1, :] * idx_scale) * jnp.maximum(lg, 0.0)
            kpos = c * tq + lax.broadcasted_iota(jnp.int32, sc.shape, 0)
            qpos = i * tq + lax.broadcasted_iota(jnp.int32, sc.shape, 1)
            st_ref[chunk(c), :] = jnp.where(kpos <= qpos, sc, -jnp.inf)
            return 0
        lax.fori_loop(0, nch, score_body, 0)

        lo0 = _to_key(jnp.full((1, tq), -jnp.inf, F32))
        hi0 = _to_key(jnp.full((1, tq), jnp.inf, F32))

        def bis_body(_, carry):
            lo, hi = carry
            mid = (lo >> 1) + (hi >> 1) + (lo & hi & 1)
            ge = count_ge(_from_key(mid)) >= kf
            return jnp.where(ge, mid, lo), jnp.where(ge, hi, mid)

        _, hi_k = lax.fori_loop(0, 32, bis_body, (lo0, hi0))

        def walk_cond(carry):
            _, _, done = carry
            return jnp.min(done) < 0.5

        def walk_body(carry):
            hi_f, thr, done = carry

            def mx_body(c, acc):
                s = st_ref[chunk(c), :]
                v = jnp.where(s < hi_f, s, -jnp.inf)
                return jnp.maximum(acc, jnp.max(v.reshape(tq // 8, 8, tq), axis=0))
            mx = lax.fori_loop(0, nch, mx_body, jnp.full((8, tq), -jnp.inf, F32))
            v = jnp.max(mx, axis=0, keepdims=True)
            ok = count_ge(v) >= kf
            is_done = done > 0.5
            thr_n = jnp.where(is_done, thr, v)
            hi_n = jnp.where(is_done, hi_f, v)
            done_n = _ind(is_done | ok)
            return hi_n, thr_n, done_n

        hi_f0 = _from_key(hi_k)
        _, thr, _ = lax.while_loop(
            walk_cond, walk_body,
            (hi_f0, jnp.full((1, tq), -jnp.inf, F32), jnp.zeros((1, tq), F32)))

        def gt_body(c, acc):
            gt = _ind(st_ref[chunk(c), :] > thr)
            return acc + jnp.sum(gt.reshape(tq // 8, 8, tq), axis=0)
        n_gt = jnp.sum(lax.fori_loop(0, nch, gt_body, jnp.zeros((8, tq), F32)),
                       axis=0, keepdims=True)
        need = kf - n_gt

        r = lax.broadcasted_iota(jnp.int32, (tq, tq), 0)
        cidx = lax.broadcasted_iota(jnp.int32, (tq, tq), 1)
        tri_ref[...] = _ind(cidx < r).astype(BF16)

        def bias_body(c, carry):
            s = st_ref[chunk(c), :]
            eq = s == thr
            eqf = _ind(eq)
            before = _dot(tri_ref[...], eqf.astype(BF16)) + carry
            kpos = c * tq + lax.broadcasted_iota(jnp.int32, s.shape, 0)
            qpos = i * tq + lax.broadcasted_iota(jnp.int32, s.shape, 1)
            selected = ((s > thr) | (eq & (before < need))) & (kpos <= qpos)
            st_ref[chunk(c), :] = jnp.where(selected, _ZERO, _NEG)
            return carry + jnp.sum(eqf, axis=0, keepdims=True)
        lax.fori_loop(0, nch, bias_body, jnp.zeros((1, tq), F32))

        def pad_body(c, _):
            st_ref[chunk(c), :] = jnp.full((tq, tq), _NEG, F32)
            return 0
        lax.fori_loop(nch, trips * ATT_UNROLL, pad_body, 0)

    hd = HEAD_DIM
    heads = range(HEADS_PER_STEP)
    q_list = [q_ref[:, h * hd:(h + 1) * hd] for h in heads]
    for h in heads:
        acc_ref[h] = jnp.zeros((hd, tq), F32)
    ls = _two_pass_attention(
        q_list, k_ref, vt_ref, sc_ref, acc_ref, lambda h, c: st_ref[chunk(c), :], trips, tq,
        [jnp.full((1, tq), _NEG, F32)] * HEADS_PER_STEP, [jnp.zeros((1, tq), F32)] * HEADS_PER_STEP)
    for h in heads:
        o = (acc_ref[h] / ls[h]).T
        g = g_ref[:, h * hd:(h + 1) * hd].astype(F32)
        o_ref[:, h * hd:(h + 1) * hd] = (o * _silu(g)).astype(o_ref.dtype)


def _dsa(act, act_t, cat, wt, batch, seq, q_col, k_col, g_col, v_row, tq=256):
    n = act.shape[0]
    nq = seq // tq
    topk = min(DSA_TOPK_MAX, seq // 4)
    n_qcat = IDX_HEADS * IDX_SLOT
    hp = HEADS_PER_STEP
    w = hp * HEAD_DIM
    q_cb, k_cb, g_cb, v_rb = q_col // w, k_col // w, g_col // w, v_row // w
    return pl.pallas_call(
        functools.partial(_dsa_kernel, topk=topk, tq=tq),
        grid=(batch, nq, DSA_HEADS // hp),
        in_specs=[pl.BlockSpec((tq, n_qcat), lambda b, i, h: (b * nq + i, 0)),
                  pl.BlockSpec((seq, IDX_SLOT), lambda b, i, h: (b, n_qcat // IDX_SLOT)),
                  pl.BlockSpec((8, tq), lambda b, i, h: (0, b * nq + i)),
                  pl.BlockSpec((tq, w), lambda b, i, h: (b * nq + i, q_cb + h)),
                  pl.BlockSpec((seq, w), lambda b, i, h: (b, k_cb + h)),
                  pl.BlockSpec((w, seq), lambda b, i, h: (v_rb + h, b)),
                  pl.BlockSpec((tq, w), lambda b, i, h: (b * nq + i, g_cb + h))],
        out_specs=pl.BlockSpec((tq, w), lambda b, i, h: (b * nq + i, h)),
        out_shape=jax.ShapeDtypeStruct((n, BRANCH_W), BF16),
        scratch_shapes=[pltpu.VMEM((seq, tq), F32),
                        pltpu.VMEM((tq, tq), BF16),
                        pltpu.VMEM((hp, seq, tq), F32),
                        pltpu.VMEM((hp, HEAD_DIM, tq), F32)],
        compiler_params=_cparams(("parallel", "arbitrary", "arbitrary")),
        name="dsa",
    )(cat, cat, wt, act, act, act_t, act)


def _mem_kernel(q_ref, mk_ref, mvt_ref, g_ref, o_ref):
    s = _dot_nt(mk_ref[...], q_ref[...])
    m = jnp.max(s, axis=0, keepdims=True)
    p = jnp.exp2(s - m)
    l = jnp.sum(p, axis=0, keepdims=True)
    o = (_dot(mvt_ref[...], p.astype(BF16)) / l).T
    o_ref[...] = (o * _silu(g_ref[...].astype(F32))).astype(o_ref.dtype)


def _mem_attn(act, mk, mvt, batch, seq, mem_len, q_cb, g_cb, tq=512):
    n = act.shape[0]
    nq = seq // tq
    hd = MEM_HEAD_DIM
    return pl.pallas_call(
        _mem_kernel,
        grid=(batch, nq, MEM_HEADS),
        in_specs=[pl.BlockSpec((tq, hd), lambda b, i, h: (b * nq + i, q_cb + h)),
                  pl.BlockSpec((mem_len, hd), lambda b, i, h: (b, h)),
                  pl.BlockSpec((hd, mem_len), lambda b, i, h: (h, b)),
                  pl.BlockSpec((tq, hd), lambda b, i, h: (b * nq + i, g_cb + h))],
        out_specs=pl.BlockSpec((tq, hd), lambda b, i, h: (b * nq + i, h)),
        out_shape=jax.ShapeDtypeStruct((n, BRANCH_W), BF16),
        compiler_params=_cparams(("parallel", "parallel", "parallel")),
        name="mem_attn",
    )(act, mk, mvt, act)


def _layer_weights(layer, w_in3, w_mem_kv3, w_branch3, w_out3):
    bw = BRANCH_W
    w_in = w_in3[layer]
    o_iq = 14 * bw
    o_r = o_iq + IDX_HEADS * IDX_DIM + IDX_DIM + IDX_HEADS
    w_tok = jnp.concatenate(
        [w_in[:, 0:2 * bw], w_in[:, 3 * bw:10 * bw], w_in[:, 11 * bw:14 * bw], w_in[:, o_r:]],
        axis=1).astype(BF16)
    w_vt = _transpose_cast(
        jnp.concatenate([w_in[:, 2 * bw:3 * bw], w_in[:, 10 * bw:11 * bw]], axis=1), "w_vt")
    w_idx = jnp.pad(w_in[:, o_iq:o_r], ((0, 0), (0, IDX_W_IN - (o_r - o_iq))))
    w_idx_hi, w_idx_lo = _split2(w_idx)
    w_mk = w_mem_kv3[layer][:, :bw].astype(BF16)
    w_mvt = _transpose_cast(w_mem_kv3[layer][:, bw:], "w_mvt")
    return dict(w_tok=w_tok, w_vt=w_vt, w_idx_hi=w_idx_hi, w_idx_lo=w_idx_lo, w_mk=w_mk,
                w_mvt=w_mvt, w_branch=w_branch3[layer].astype(BF16),
                w_out=w_out3[layer].astype(BF16))


def _layer(x2, mem2, batch, seq, mem_len, ln_g, conv_w, conv_b, mem_ln_g, w):
    nb = BRANCH_W // LANE
    (A_Q, A_K, A_G, C_B, C_C, C_H, C_G, S_Q, S_K, S_G, M_Q, M_G) = range(12)
    r_col0 = 12 * BRANCH_W

    log2e = float(np.log2(np.e))
    colscale = np.ones((1, w["w_tok"].shape[1]), np.float32)
    for col, hdim in ((A_Q, HEAD_DIM), (S_Q, HEAD_DIM), (M_Q, MEM_HEAD_DIM)):
        colscale[:, col * BRANCH_W:(col + 1) * BRANCH_W] = hdim ** -0.5 * log2e

    xn, cat, wt = _prep(x2, ln_g, w["w_idx_hi"], w["w_idx_lo"])
    act = _mm_nn_colscale(xn, w["w_tok"], jnp.asarray(colscale), BF16, tm=1024, tn=1024,
                          name="proj_tok")
    act_t = _mm_nt(w["w_vt"], xn, BF16, tm=1024, tn=1024, name="proj_vt")

    bw = BRANCH_W
    y_a = _moba(act, act_t, batch, seq, A_Q * bw, A_K * bw, A_G * bw, 0)
    y_c = _conv(act, conv_w, conv_b, batch, seq, C_B * nb, C_C * nb, C_H * nb, C_G * nb)
    y_s = _dsa(act, act_t, cat, wt, batch, seq, S_Q * bw, S_K * bw, S_G * bw, bw)

    mem_n = _rmsnorm(mem2, mem_ln_g, BF16)
    mk = _mm_nn(mem_n, w["w_mk"], BF16, tm=mem2.shape[0], tn=512, name="mem_k")
    mvt = _mm_nt(w["w_mvt"], mem_n, BF16, tm=512, tn=mem2.shape[0], name="mem_vt")
    mb = BRANCH_W // MEM_HEAD_DIM
    y_m = _mem_attn(act, mk, mvt, batch, seq, mem_len, M_Q * mb, M_G * mb)

    merged = _merge((y_a, y_c, y_s, y_m), act, r_col0, w["w_branch"], tm=512, tn=512)
    return _mm_res(merged, w["w_out"], x2, tm=512, tn=1024)


def kernel(x, mem, ln_g, w_in, conv_w, conv_b, mem_ln_g, w_mem_kv, w_branch, w_out, final_g):
    batch, seq, d = x.shape
    mem_len = mem.shape[1]
    x2 = x.reshape(batch * seq, d)
    mem2 = mem.reshape(batch * mem_len, d)
    for layer in range(ln_g.shape[0]):
        w = _layer_weights(layer, w_in, w_mem_kv, w_branch, w_out)
        x2 = _layer(x2, mem2, batch, seq, mem_len, ln_g[layer], conv_w[layer], conv_b[layer],
                    mem_ln_g[layer], w)
    return _rmsnorm(x2, final_g, x.dtype).reshape(batch, seq, d)
```

```python
import functools

import numpy as np
import jax
import jax.numpy as jnp
from jax import lax
from jax.experimental import pallas as pl
from jax.experimental.pallas import tpu as pltpu

EPS = 1e-6
BRANCH_W = 1024
MOBA_HEADS = 8
MOBA_BLOCK = 256
MOBA_TOPK = 3
DSA_HEADS = 8
DSA_TOPK_MAX = 256
IDX_HEADS = 4
IDX_DIM = 64
MEM_HEADS = 4
HEAD_DIM = 128
MEM_HEAD_DIM = 256
IDX_SLOT = 256
NEG = -1e30
ATT_UNROLL = 4
HEADS_PER_STEP = 4
PROBE_EXP_STEPS = 3
LANE = 128
VMEM_LIMIT = 56 * 1024 * 1024

F32 = jnp.float32
BF16 = jnp.bfloat16
_ONE = np.float32(1.0)
_ZERO = np.float32(0.0)
_NEG = np.float32(NEG)


def _cparams(sem):
    return pltpu.CompilerParams(dimension_semantics=sem, vmem_limit_bytes=VMEM_LIMIT)


def _dot(a, b):
    return jnp.dot(a, b, preferred_element_type=F32)


def _dot_nt(a, b):
    return lax.dot_general(a, b, (((1,), (1,)), ((), ())), preferred_element_type=F32)


def _split2(v):
    hi = v.astype(BF16)
    lo = (v - hi.astype(F32)).astype(BF16)
    return hi, lo


def _split3(v):
    h1 = v.astype(BF16)
    r1 = v - h1.astype(F32)
    h2 = r1.astype(BF16)
    h3 = (r1 - h2.astype(F32)).astype(BF16)
    return h1, h2, h3


def _ind(cond):
    return jnp.where(cond, _ONE, _ZERO)


def _silu(g):
    return g * jax.nn.sigmoid(g)


def _rms(x, g):
    var = jnp.mean(x * x, axis=-1, keepdims=True)
    return (x * lax.rsqrt(var + EPS)) * g


def _rmsnorm_kernel(x_ref, g_ref, o_ref):
    o_ref[...] = _rms(x_ref[...], g_ref[...]).astype(o_ref.dtype)


def _rmsnorm(x, g, out_dtype, tm=256):
    n, d = x.shape
    return pl.pallas_call(
        _rmsnorm_kernel,
        grid=(n // tm,),
        in_specs=[pl.BlockSpec((tm, d), lambda i: (i, 0)),
                  pl.BlockSpec((1, d), lambda i: (0, 0))],
        out_specs=pl.BlockSpec((tm, d), lambda i: (i, 0)),
        out_shape=jax.ShapeDtypeStruct((n, d), out_dtype),
        compiler_params=_cparams(("parallel",)),
        name="rmsnorm",
    )(x, g.reshape(1, d))


def _prep_kernel(x_ref, g_ref, whi_ref, wlo_ref, phi_ref, plo_ref, xn_ref, cat_ref, wt_ref):
    xn = _rms(x_ref[...], g_ref[...])
    hi, lo = _split2(xn)
    xn_ref[...] = hi
    acc = _dot_nt(hi, whi_ref[...]) + _dot_nt(hi, wlo_ref[...]) + _dot_nt(lo, whi_ref[...])
    vh, vl = _split2(acc)
    cat_ref[...] = (_dot(vh, phi_ref[...]) + _dot(vl, plo_ref[...])).astype(BF16)
    gate_row = IDX_DIM
    wt_ref[...] = acc[:, IDX_W_IN - LANE:].T[gate_row:gate_row + 8, :]


IDX_W_IN = 384


def _idx_placement():
    n_out = (IDX_HEADS + 1) * IDX_SLOT
    p_hi = np.zeros((IDX_W_IN, n_out), np.float32)
    p_lo = np.zeros((IDX_W_IN, n_out), np.float32)
    e = np.arange(IDX_DIM)
    for j in range(IDX_HEADS):
        src, dst = j * IDX_DIM + e, j * IDX_SLOT + e
        p_hi[src, dst] = 1
        p_hi[src, dst + IDX_DIM] = 1
        p_lo[src, dst + 2 * IDX_DIM] = 1
    src, dst = IDX_HEADS * IDX_DIM + e, IDX_HEADS * IDX_SLOT + e
    p_hi[src, dst] = 1
    p_lo[src, dst + IDX_DIM] = 1
    p_hi[src, dst + 2 * IDX_DIM] = 1
    return jnp.asarray(p_hi, BF16), jnp.asarray(p_lo, BF16)


def _prep(x, g, whi, wlo, tm=256):
    n, d = x.shape
    p_hi, p_lo = _idx_placement()
    co = p_hi.shape[1]
    return pl.pallas_call(
        _prep_kernel,
        grid=(n // tm,),
        in_specs=[pl.BlockSpec((tm, d), lambda i: (i, 0)),
                  pl.BlockSpec((1, d), lambda i: (0, 0)),
                  pl.BlockSpec((IDX_W_IN, d), lambda i: (0, 0)),
                  pl.BlockSpec((IDX_W_IN, d), lambda i: (0, 0)),
                  pl.BlockSpec((IDX_W_IN, co), lambda i: (0, 0)),
                  pl.BlockSpec((IDX_W_IN, co), lambda i: (0, 0))],
        out_specs=[pl.BlockSpec((tm, d), lambda i: (i, 0)),
                   pl.BlockSpec((tm, co), lambda i: (i, 0)),
                   pl.BlockSpec((8, tm), lambda i: (0, i))],
        out_shape=[jax.ShapeDtypeStruct((n, d), BF16),
                   jax.ShapeDtypeStruct((n, co), BF16),
                   jax.ShapeDtypeStruct((8, n), F32)],
        compiler_params=_cparams(("parallel",)),
        name="prep",
    )(x, g.reshape(1, d), whi, wlo, p_hi, p_lo)


def _transpose_cast_kernel(w_ref, o_ref):
    o_ref[...] = w_ref[...].T.astype(o_ref.dtype)


def _transpose_cast(w, name, t=512):
    d, n = w.shape
    return pl.pallas_call(
        _transpose_cast_kernel,
        grid=(n // t, d // t),
        in_specs=[pl.BlockSpec((t, t), lambda i, j: (j, i))],
        out_specs=pl.BlockSpec((t, t), lambda i, j: (i, j)),
        out_shape=jax.ShapeDtypeStruct((n, d), BF16),
        compiler_params=_cparams(("parallel", "parallel")),
        name=name,
    )(w)


def _mm_nn_kernel(a_ref, b_ref, o_ref):
    o_ref[...] = _dot(a_ref[...], b_ref[...]).astype(o_ref.dtype)


def _mm_nn(a, b, out_dtype, tm, tn, name):
    m, k = a.shape
    n = b.shape[1]
    return pl.pallas_call(
        _mm_nn_kernel,
        grid=(m // tm, n // tn),
        in_specs=[pl.BlockSpec((tm, k), lambda i, j: (i, 0)),
                  pl.BlockSpec((k, tn), lambda i, j: (0, j))],
        out_specs=pl.BlockSpec((tm, tn), lambda i, j: (i, j)),
        out_shape=jax.ShapeDtypeStruct((m, n), out_dtype),
        compiler_params=_cparams(("parallel", "parallel")),
        name=name,
    )(a, b)


def _mm_nt_colscale_kernel(a_ref, b_ref, s_ref, o_ref):
    o_ref[...] = (_dot_nt(a_ref[...], b_ref[...]) * s_ref[...]).astype(o_ref.dtype)


def _mm_nt_colscale(a, b, colscale, out_dtype, tm, tn, name):
    m, k = a.shape
    n = b.shape[0]
    return pl.pallas_call(
        _mm_nt_colscale_kernel,
        grid=(m // tm, n // tn),
        in_specs=[pl.BlockSpec((tm, k), lambda i, j: (i, 0)),
                  pl.BlockSpec((tn, k), lambda i, j: (j, 0)),
                  pl.BlockSpec((1, tn), lambda i, j: (0, j))],
        out_specs=pl.BlockSpec((tm, tn), lambda i, j: (i, j)),
        out_shape=jax.ShapeDtypeStruct((m, n), out_dtype),
        compiler_params=_cparams(("parallel", "parallel")),
        name=name,
    )(a, b, colscale)


def _mm_nt_kernel(a_ref, b_ref, o_ref):
    o_ref[...] = _dot_nt(a_ref[...], b_ref[...]).astype(o_ref.dtype)


def _mm_nt(a, b, out_dtype, tm, tn, name):
    m, k = a.shape
    n = b.shape[0]
    return pl.pallas_call(
        _mm_nt_kernel,
        grid=(n // tn, m // tm),
        in_specs=[pl.BlockSpec((tm, k), lambda j, i: (i, 0)),
                  pl.BlockSpec((tn, k), lambda j, i: (j, 0))],
        out_specs=pl.BlockSpec((tm, tn), lambda j, i: (i, j)),
        out_shape=jax.ShapeDtypeStruct((m, n), out_dtype),
        compiler_params=_cparams(("parallel", "parallel")),
        name=name,
    )(a, b)


def _mm_res_kernel(a_ref, b_ref, r_ref, o_ref):
    o_ref[...] = r_ref[...] + _dot(a_ref[...], b_ref[...])


def _mm_res(a, b, res, tm, tn):
    m, k = a.shape
    n = b.shape[1]
    return pl.pallas_call(
        _mm_res_kernel,
        grid=(m // tm, n // tn),
        in_specs=[pl.BlockSpec((tm, k), lambda i, j: (i, 0)),
                  pl.BlockSpec((k, tn), lambda i, j: (0, j)),
                  pl.BlockSpec((tm, tn), lambda i, j: (i, j))],
        out_specs=pl.BlockSpec((tm, tn), lambda i, j: (i, j)),
        out_shape=jax.ShapeDtypeStruct((m, n), F32),
        compiler_params=_cparams(("parallel", "parallel")),
        name="out_proj",
    )(a, b, res)


def _merge_kernel(ya_ref, yc_ref, ys_ref, ym_ref, ra_ref, rc_ref, rs_ref, rm_ref, wb_ref, o_ref):
    ys = (ya_ref, yc_ref, ys_ref, ym_ref)
    rs = (ra_ref, rc_ref, rs_ref, rm_ref)
    acc = None
    for br in range(4):
        z = _dot(ys[br][...], wb_ref[br])
        term = jax.nn.sigmoid(rs[br][...].astype(F32)) * z
        acc = term if acc is None else acc + term
    o_ref[...] = acc.astype(o_ref.dtype)


def _merge(ys, act, r_col0, wb, tm, tn):
    n, w = ys[0].shape
    d = wb.shape[2]
    nj = d // tn
    y_specs = [pl.BlockSpec((tm, w), lambda i, j: (i, 0)) for _ in range(4)]
    r_specs = [pl.BlockSpec((tm, tn), functools.partial(
        lambda i, j, base: (i, base + j), base=(r_col0 + br * d) // tn)) for br in range(4)]
    del nj
    return pl.pallas_call(
        _merge_kernel,
        grid=(n // tm, d // tn),
        in_specs=y_specs + r_specs + [pl.BlockSpec((4, w, tn), lambda i, j: (0, 0, j))],
        out_specs=pl.BlockSpec((tm, tn), lambda i, j: (i, j)),
        out_shape=jax.ShapeDtypeStruct((n, d), BF16),
        compiler_params=_cparams(("parallel", "parallel")),
        name="merge",
    )(*ys, act, act, act, act, wb)


def _ceil_div_unroll(n):
    return (n + (ATT_UNROLL - 1)) >> (ATT_UNROLL.bit_length() - 1)


def _two_pass_attention(q_list, k_ref, vt_ref, sc_ref, acc_ref, bias_fn, trips, blk, m0_list, l0_list):
    heads = range(len(q_list))
    hd = HEAD_DIM
    span = ATT_UNROLL * blk

    def pass1(j, ms):
        base = pl.multiple_of(j * span, span)
        out = []
        for h in heads:
            s = _dot_nt(k_ref[pl.ds(base, span), h * hd:(h + 1) * hd], q_list[h])
            m = ms[h]
            for u in range(ATT_UNROLL):
                su = s[u * blk:(u + 1) * blk, :] + bias_fn(h, j * ATT_UNROLL + u)
                sc_ref[h, pl.ds(pl.multiple_of(base + u * blk, blk), blk), :] = su
                m = jnp.maximum(m, jnp.max(su, axis=0, keepdims=True))
            out.append(m)
        return tuple(out)

    ms = lax.fori_loop(0, trips, pass1, tuple(m0_list))

    ls = []
    for h in heads:
        alpha = jnp.exp2(m0_list[h] - ms[h])
        ls.append(alpha * l0_list[h])
        acc_ref[h] = alpha * acc_ref[h]

    def pass2(j, ls):
        base = pl.multiple_of(j * span, span)
        out = []
        for h in heads:
            p = jnp.exp2(sc_ref[h, pl.ds(base, span), :] - ms[h])
            out.append(ls[h] + jnp.sum(p, axis=0, keepdims=True))
            acc_ref[h] += _dot(vt_ref[h * hd:(h + 1) * hd, pl.ds(base, span)], p.astype(BF16))
        return tuple(out)

    return lax.fori_loop(0, trips, pass2, tuple(ls))


def _moba_kernel(q_ref, k_ref, vt_ref, g_ref, o_ref, kmean_ref, selb_ref, sc_ref, acc_ref, *, nblk):
    i = pl.program_id(2)
    blk = MOBA_BLOCK
    hd = HEAD_DIM
    heads = range(HEADS_PER_STEP)

    @pl.when(i == 0)
    def _():
        for h in heads:
            kf = k_ref[:, h * hd:(h + 1) * hd].astype(F32).reshape(nblk, blk, hd)
            kmean_ref[h] = jnp.mean(kf, axis=1)

    off = pl.multiple_of(i * blk, blk)
    q_list, m0_list, l0_list = [], [], []
    for h in heads:
        q = q_ref[:, h * hd:(h + 1) * hd]
        q_list.append(q)
        k1, k2, k3 = _split3(kmean_ref[h])
        bs = _dot_nt(k1, q) + _dot_nt(k2, q) + _dot_nt(k3, q)
        n_iota = lax.broadcasted_iota(jnp.int32, bs.shape, 0)
        past = n_iota < i
        left = jnp.where(past, bs, -jnp.inf)
        bias = jnp.full(bs.shape, _NEG, F32)
        for _ in range(MOBA_TOPK):
            top = jnp.max(left, axis=0, keepdims=True)
            first = jnp.min(jnp.where(left == top, n_iota, nblk), axis=0, keepdims=True)
            hit = n_iota == first
            bias = jnp.where(hit & past, _ZERO, bias)
            left = jnp.where(hit, -jnp.inf, left)
        selb_ref[h] = bias

        st = _dot_nt(k_ref[pl.ds(off, blk), h * hd:(h + 1) * hd], q)
        kpos = lax.broadcasted_iota(jnp.int32, st.shape, 0)
        qpos = lax.broadcasted_iota(jnp.int32, st.shape, 1)
        st = jnp.where(kpos <= qpos, st, _NEG)
        m0 = jnp.max(st, axis=0, keepdims=True)
        p0 = jnp.exp2(st - m0)
        m0_list.append(m0)
        l0_list.append(jnp.sum(p0, axis=0, keepdims=True))
        acc_ref[h] = _dot(vt_ref[h * hd:(h + 1) * hd, pl.ds(off, blk)], p0.astype(BF16))

    ls = _two_pass_attention(
        q_list, k_ref, vt_ref, sc_ref, acc_ref, lambda h, n: selb_ref[h, pl.ds(n, 1), :],
        _ceil_div_unroll(i), blk, m0_list, l0_list)
    for h in heads:
        o = (acc_ref[h] / ls[h]).T
        g = g_ref[:, h * hd:(h + 1) * hd].astype(F32)
        o_ref[:, h * hd:(h + 1) * hd] = (o * _silu(g)).astype(o_ref.dtype)


def _moba(act, act_t, batch, seq, q_col, k_col, g_col, v_row):
    nblk = seq // MOBA_BLOCK
    n = act.shape[0]
    blk = MOBA_BLOCK
    hp = HEADS_PER_STEP
    w = hp * HEAD_DIM
    q_cb, k_cb, g_cb, v_rb = q_col // w, k_col // w, g_col // w, v_row // w
    return pl.pallas_call(
        functools.partial(_moba_kernel, nblk=nblk),
        grid=(batch, MOBA_HEADS // hp, nblk),
        in_specs=[pl.BlockSpec((blk, w), lambda b, h, i: (b * nblk + i, q_cb + h)),
                  pl.BlockSpec((seq, w), lambda b, h, i: (b, k_cb + h)),
                  pl.BlockSpec((w, seq), lambda b, h, i: (v_rb + h, b)),
                  pl.BlockSpec((blk, w), lambda b, h, i: (b * nblk + i, g_cb + h))],
        out_specs=pl.BlockSpec((blk, w), lambda b, h, i: (b * nblk + i, h)),
        out_shape=jax.ShapeDtypeStruct((n, BRANCH_W), BF16),
        scratch_shapes=[pltpu.VMEM((hp, nblk, HEAD_DIM), F32),
                        pltpu.VMEM((hp, nblk, blk), F32),
                        pltpu.VMEM((hp, seq, blk), F32),
                        pltpu.VMEM((hp, HEAD_DIM, blk), F32)],
        compiler_params=_cparams(("parallel", "parallel", "arbitrary")),
        name="moba",
    )(act, act, act_t, act)


def _conv_kernel(cb_ref, cc_ref, ch_ref, cg_ref, w_ref, bias_ref, o_ref):
    u = cc_ref[...].astype(F32) * ch_ref[...].astype(F32)
    t = lax.broadcasted_iota(jnp.int32, u.shape, 0)
    u1 = jnp.where(t >= 1, pltpu.roll(u, 1, 0), 0.0)
    u2 = jnp.where(t >= 2, pltpu.roll(u, 2, 0), 0.0)
    w = w_ref[...]
    conv = u2 * w[0:1, :] + u1 * w[1:2, :] + u * w[2:3, :] + bias_ref[...]
    o_ref[...] = (cb_ref[...].astype(F32) * conv * _silu(cg_ref[...].astype(F32))).astype(o_ref.dtype)


def _conv(act, conv_w, conv_b, batch, seq, cb_cb, cc_cb, ch_cb, cg_cb):
    n = act.shape[0]
    ncb = BRANCH_W // LANE
    kw = conv_w.shape[0]
    wpad = jnp.zeros((8, BRANCH_W), F32).at[:kw].set(conv_w)

    def spec(base):
        return pl.BlockSpec((seq, LANE), lambda b, c: (b, base + c))

    return pl.pallas_call(
        _conv_kernel,
        grid=(batch, ncb),
        in_specs=[spec(cb_cb), spec(cc_cb), spec(ch_cb), spec(cg_cb),
                  pl.BlockSpec((8, LANE), lambda b, c: (0, c)),
                  pl.BlockSpec((1, LANE), lambda b, c: (0, c))],
        out_specs=pl.BlockSpec((seq, LANE), lambda b, c: (b, c)),
        out_shape=jax.ShapeDtypeStruct((n, BRANCH_W), BF16),
        compiler_params=_cparams(("parallel", "parallel")),
        name="conv",
    )(act, act, act, act, wpad, conv_b.reshape(1, BRANCH_W))


def _to_key(f):
    b = pltpu.bitcast(f, jnp.int32)
    return jnp.where(b >= 0, b, b ^ jnp.int32(0x7FFFFFFF))


def _from_key(k):
    return pltpu.bitcast(jnp.where(k >= 0, k, k ^ jnp.int32(0x7FFFFFFF)), F32)


def _dsa_kernel(qc_ref, kc_ref, wt_ref, q_ref, k_ref, vt_ref, g_ref, o_ref, st_ref, tri_ref, sc_ref,
                acc_ref, *, topk, tq):
    i = pl.program_id(1)
    nch = i + 1
    npair = (nch + 1) >> 1
    trips = _ceil_div_unroll(nch)
    idx_scale = (IDX_DIM ** -0.5) * (IDX_HEADS ** -0.5)
    kf = float(topk)
    ninf = np.float32(-np.inf)

    def chunk(c):
        return pl.ds(pl.multiple_of(c * tq, tq), tq)

    def pair(c):
        return pl.ds(pl.multiple_of(c * (2 * tq), 2 * tq), 2 * tq)

    def fold8(x, op):
        return op(x.reshape(x.shape[0] // 8, 8, tq), axis=0)

    def count(pred):
        def body(c, acc):
            return acc + fold8(_ind(pred(st_ref[chunk(c), :])), jnp.sum)
        acc = lax.fori_loop(0, nch, body, jnp.zeros((8, tq), F32))
        return jnp.sum(acc, axis=0, keepdims=True)

    @pl.when(pl.program_id(2) == 0)
    def _select():
        def score_body(c, mx):
            for u in range(2):
                cc = 2 * c + u
                kc = kc_ref[chunk(cc), :]
                sc = jnp.zeros((tq, tq), F32)
                for j in range(IDX_HEADS):
                    lg = _dot_nt(kc, qc_ref[:, j * IDX_SLOT:(j + 1) * IDX_SLOT])
                    sc = sc + (wt_ref[j:j + 1, :] * idx_scale) * jnp.maximum(lg, 0.0)
                kpos = cc * tq + lax.broadcasted_iota(jnp.int32, sc.shape, 0)
                qpos = i * tq + lax.broadcasted_iota(jnp.int32, sc.shape, 1)
                sc = jnp.where(kpos <= qpos, sc, ninf)
                st_ref[chunk(cc), :] = sc
                mx = jnp.maximum(mx, fold8(sc, jnp.max))
            return mx
        mx8 = lax.fori_loop(0, npair, score_body, jnp.full((8, tq), ninf, F32))
        key_max = _to_key(jnp.max(mx8, axis=0, keepdims=True))

        key_ninf = _to_key(jnp.full((1, tq), ninf, F32))
        step = jnp.int32(PROBE_EXP_STEPS << 23)
        probe = jnp.where(key_max < key_ninf + step, key_ninf, key_max - step)
        hit = count(lambda s: s >= _from_key(probe)) >= kf
        lo0 = jnp.where(hit, probe, key_ninf)
        hi0 = jnp.where(hit, key_max + 1, probe)
        width = (hi0 >> 1) - (lo0 >> 1)
        n_iter = jnp.max(34 - lax.clz(width))

        def bis_body(_, carry):
            lo, hi = carry
            mid = (lo >> 1) + (hi >> 1) + (lo & hi & 1)
            ge = count(lambda s: s >= _from_key(mid)) >= kf
            return jnp.where(ge, mid, lo), jnp.where(ge, hi, mid)

        _, hi_k = lax.fori_loop(0, n_iter, bis_body, (lo0, hi0))

        def walk_cond(carry):
            return jnp.min(carry[3]) < 0.5

        def walk_body(carry):
            hi_f, thr, cnt, done = carry

            def mx_body(c, acc):
                s = st_ref[pair(c), :]
                return jnp.maximum(acc, fold8(jnp.where(s < hi_f, s, ninf), jnp.max))
            mx = lax.fori_loop(0, npair, mx_body, jnp.full((8, tq), ninf, F32))
            v = jnp.max(mx, axis=0, keepdims=True)
            c_v = count(lambda s: s >= v)
            is_done = done > 0.5
            return (jnp.where(is_done, hi_f, v), jnp.where(is_done, thr, v),
                    jnp.where(is_done, cnt, c_v), _ind(is_done | (c_v >= kf)))

        zero_row = jnp.zeros((1, tq), F32)
        _, thr, n_ge, _ = lax.while_loop(
            walk_cond, walk_body,
            (_from_key(hi_k), jnp.full((1, tq), ninf, F32), zero_row, zero_row))

        n_gt = count(lambda s: s > thr)
        need = kf - n_gt
        n_eq = n_ge - n_gt

        def bias_all_ties(c, _):
            st_ref[pair(c), :] = jnp.where(st_ref[pair(c), :] >= thr, _ZERO, _NEG)
            return 0

        def bias_some_ties(c, carry):
            for u in range(2):
                cc = 2 * c + u
                s = st_ref[chunk(cc), :]
                eq = s == thr
                eqf = _ind(eq)
                before = _dot(tri_ref[...], eqf.astype(BF16)) + carry
                kpos = cc * tq + lax.broadcasted_iota(jnp.int32, s.shape, 0)
                qpos = i * tq + lax.broadcasted_iota(jnp.int32, s.shape, 1)
                selected = ((s > thr) | (eq & (before < need))) & (kpos <= qpos)
                st_ref[chunk(cc), :] = jnp.where(selected, _ZERO, _NEG)
                carry = carry + jnp.sum(eqf, axis=0, keepdims=True)
            return carry

        def all_ties():
            lax.fori_loop(0, npair, bias_all_ties, 0)

        def some_ties():
            r = lax.broadcasted_iota(jnp.int32, (tq, tq), 0)
            cidx = lax.broadcasted_iota(jnp.int32, (tq, tq), 1)
            tri_ref[...] = _ind(cidx < r).astype(BF16)
            lax.fori_loop(0, npair, bias_some_ties, zero_row)

        simple = jnp.logical_and(jnp.max(n_eq - need) <= 0.0, jnp.min(thr) > ninf)
        lax.cond(simple, all_ties, some_ties)

        def pad_body(c, _):
            st_ref[chunk(c), :] = jnp.full((tq, tq), _NEG, F32)
            return 0
        lax.fori_loop(2 * npair, trips * ATT_UNROLL, pad_body, 0)

    hd = HEAD_DIM
    heads = range(HEADS_PER_STEP)
    q_list = [q_ref[:, h * hd:(h + 1) * hd] for h in heads]
    for h in heads:
        acc_ref[h] = jnp.zeros((hd, tq), F32)
    ls = _two_pass_attention(
        q_list, k_ref, vt_ref, sc_ref, acc_ref, lambda h, c: st_ref[chunk(c), :], trips, tq,
        [jnp.full((1, tq), _NEG, F32)] * HEADS_PER_STEP, [jnp.zeros((1, tq), F32)] * HEADS_PER_STEP)
    for h in heads:
        o = (acc_ref[h] / ls[h]).T
        g = g_ref[:, h * hd:(h + 1) * hd].astype(F32)
        o_ref[:, h * hd:(h + 1) * hd] = (o * _silu(g)).astype(o_ref.dtype)


def _dsa(act, act_t, cat, wt, batch, seq, q_col, k_col, g_col, v_row, tq=256):
    n = act.shape[0]
    nq = seq // tq
    topk = min(DSA_TOPK_MAX, seq // 4)
    n_qcat = IDX_HEADS * IDX_SLOT
    hp = HEADS_PER_STEP
    w = hp * HEAD_DIM
    q_cb, k_cb, g_cb, v_rb = q_col // w, k_col // w, g_col // w, v_row // w
    return pl.pallas_call(
        functools.partial(_dsa_kernel, topk=topk, tq=tq),
        grid=(batch, nq, DSA_HEADS // hp),
        in_specs=[pl.BlockSpec((tq, n_qcat), lambda b, i, h: (b * nq + i, 0)),
                  pl.BlockSpec((seq, IDX_SLOT), lambda b, i, h: (b, n_qcat // IDX_SLOT)),
                  pl.BlockSpec((8, tq), lambda b, i, h: (0, b * nq + i)),
                  pl.BlockSpec((tq, w), lambda b, i, h: (b * nq + i, q_cb + h)),
                  pl.BlockSpec((seq, w), lambda b, i, h: (b, k_cb + h)),
                  pl.BlockSpec((w, seq), lambda b, i, h: (v_rb + h, b)),
                  pl.BlockSpec((tq, w), lambda b, i, h: (b * nq + i, g_cb + h))],
        out_specs=pl.BlockSpec((tq, w), lambda b, i, h: (b * nq + i, h)),
        out_shape=jax.ShapeDtypeStruct((n, BRANCH_W), BF16),
        scratch_shapes=[pltpu.VMEM((seq, tq), F32),
                        pltpu.VMEM((tq, tq), BF16),
                        pltpu.VMEM((hp, seq, tq), F32),
                        pltpu.VMEM((hp, HEAD_DIM, tq), F32)],
        compiler_params=_cparams(("parallel", "arbitrary", "arbitrary")),
        name="dsa",
    )(cat, cat, wt, act, act, act_t, act)


def _mem_kernel(q_ref, mk_ref, mvt_ref, g_ref, o_ref):
    s = _dot_nt(mk_ref[...], q_ref[...])
    m = jnp.max(s, axis=0, keepdims=True)
    p = jnp.exp2(s - m)
    l = jnp.sum(p, axis=0, keepdims=True)
    o = (_dot(mvt_ref[...], p.astype(BF16)) / l).T
    o_ref[...] = (o * _silu(g_ref[...].astype(F32))).astype(o_ref.dtype)


def _mem_attn(act, mk, mvt, batch, seq, mem_len, q_cb, g_cb, tq=512):
    n = act.shape[0]
    nq = seq // tq
    hd = MEM_HEAD_DIM
    return pl.pallas_call(
        _mem_kernel,
        grid=(batch, nq, MEM_HEADS),
        in_specs=[pl.BlockSpec((tq, hd), lambda b, i, h: (b * nq + i, q_cb + h)),
                  pl.BlockSpec((mem_len, hd), lambda b, i, h: (b, h)),
                  pl.BlockSpec((hd, mem_len), lambda b, i, h: (h, b)),
                  pl.BlockSpec((tq, hd), lambda b, i, h: (b * nq + i, g_cb + h))],
        out_specs=pl.BlockSpec((tq, hd), lambda b, i, h: (b * nq + i, h)),
        out_shape=jax.ShapeDtypeStruct((n, BRANCH_W), BF16),
        compiler_params=_cparams(("parallel", "parallel", "parallel")),
        name="mem_attn",
    )(act, mk, mvt, act)


def _layer_weights(layer, w_in3, w_mem_kv3, w_branch3, w_out3):
    bw = BRANCH_W
    def cols(a, b, dtype):
        return jnp.swapaxes(w_in3[layer, :, a:b], 0, 1).astype(dtype)

    o_iq = 14 * bw
    o_r = o_iq + IDX_HEADS * IDX_DIM + IDX_DIM + IDX_HEADS
    n_in = w_in3.shape[2]
    w_tok = jnp.concatenate(
        [cols(0, 2 * bw, BF16), cols(3 * bw, 10 * bw, BF16), cols(11 * bw, 14 * bw, BF16),
         cols(o_r, n_in, BF16)], axis=0)
    w_vt = jnp.concatenate([cols(2 * bw, 3 * bw, BF16), cols(10 * bw, 11 * bw, BF16)], axis=0)
    w_idx = jnp.pad(cols(o_iq, o_r, F32), ((0, IDX_W_IN - (o_r - o_iq)), (0, 0)))
    w_idx_hi, w_idx_lo = _split2(w_idx)
    w_mk = w_mem_kv3[layer][:, :bw].astype(BF16)
    w_mvt = _transpose_cast(w_mem_kv3[layer][:, bw:], "w_mvt")
    return dict(w_tok=w_tok, w_vt=w_vt, w_idx_hi=w_idx_hi, w_idx_lo=w_idx_lo, w_mk=w_mk,
                w_mvt=w_mvt, w_branch=w_branch3[layer].astype(BF16),
                w_out=w_out3[layer].astype(BF16))


def _layer(x2, mem2, batch, seq, mem_len, ln_g, conv_w, conv_b, mem_ln_g, w):
    nb = BRANCH_W // LANE
    (A_Q, A_K, A_G, C_B, C_C, C_H, C_G, S_Q, S_K, S_G, M_Q, M_G) = range(12)
    r_col0 = 12 * BRANCH_W

    log2e = float(np.log2(np.e))
    colscale = np.ones((1, w["w_tok"].shape[0]), np.float32)
    for col, hdim in ((A_Q, HEAD_DIM), (S_Q, HEAD_DIM), (M_Q, MEM_HEAD_DIM)):
        colscale[:, col * BRANCH_W:(col + 1) * BRANCH_W] = hdim ** -0.5 * log2e

    xn, cat, wt = _prep(x2, ln_g, w["w_idx_hi"], w["w_idx_lo"])
    act = _mm_nt_colscale(xn, w["w_tok"], jnp.asarray(colscale), BF16, tm=1024, tn=1024,
                          name="proj_tok")
    act_t = _mm_nt(w["w_vt"], xn, BF16, tm=1024, tn=1024, name="proj_vt")

    bw = BRANCH_W
    y_a = _moba(act, act_t, batch, seq, A_Q * bw, A_K * bw, A_G * bw, 0)
    y_c = _conv(act, conv_w, conv_b, batch, seq, C_B * nb, C_C * nb, C_H * nb, C_G * nb)
    y_s = _dsa(act, act_t, cat, wt, batch, seq, S_Q * bw, S_K * bw, S_G * bw, bw)

    mem_n = _rmsnorm(mem2, mem_ln_g, BF16)
    mk = _mm_nn(mem_n, w["w_mk"], BF16, tm=mem2.shape[0], tn=512, name="mem_k")
    mvt = _mm_nt(w["w_mvt"], mem_n, BF16, tm=512, tn=mem2.shape[0], name="mem_vt")
    mb = BRANCH_W // MEM_HEAD_DIM
    y_m = _mem_attn(act, mk, mvt, batch, seq, mem_len, M_Q * mb, M_G * mb)

    merged = _merge((y_a, y_c, y_s, y_m), act, r_col0, w["w_branch"], tm=512, tn=512)
    return _mm_res(merged, w["w_out"], x2, tm=512, tn=1024)


def kernel(x, mem, ln_g, w_in, conv_w, conv_b, mem_ln_g, w_mem_kv, w_branch, w_out, final_g):
    batch, seq, d = x.shape
    mem_len = mem.shape[1]
    x2 = x.reshape(batch * seq, d)
    mem2 = mem.reshape(batch * mem_len, d)
    for layer in range(ln_g.shape[0]):
        w = _layer_weights(layer, w_in, w_mem_kv, w_branch, w_out)
        x2 = _layer(x2, mem2, batch, seq, mem_len, ln_g[layer], conv_w[layer], conv_b[layer],
                    mem_ln_g[layer], w)
    return _rmsnorm(x2, final_g, x.dtype).reshape(batch, seq, d)
```

```python
import functools

import numpy as np
import jax
import jax.numpy as jnp
from jax import lax
from jax.experimental import pallas as pl
from jax.experimental.pallas import tpu as pltpu

EPS = 1e-6
BRANCH_W = 1024
MOBA_HEADS = 8
MOBA_BLOCK = 256
MOBA_TOPK = 3
DSA_HEADS = 8
DSA_TOPK_MAX = 256
IDX_HEADS = 4
IDX_DIM = 64
MEM_HEADS = 4
HEAD_DIM = 128
MEM_HEAD_DIM = 256
IDX_SLOT = 256
NEG = -1e30
ATT_UNROLL = 4
HEADS_PER_STEP = 4
BISECT_ITERS = 22
LANE = 128
VMEM_LIMIT = 56 * 1024 * 1024

F32 = jnp.float32
BF16 = jnp.bfloat16
_ONE = np.float32(1.0)
_ZERO = np.float32(0.0)
_NEG = np.float32(NEG)


def _cparams(sem):
    return pltpu.CompilerParams(dimension_semantics=sem, vmem_limit_bytes=VMEM_LIMIT)


def _dot(a, b):
    return jnp.dot(a, b, preferred_element_type=F32)


def _dot_nt(a, b):
    return lax.dot_general(a, b, (((1,), (1,)), ((), ())), preferred_element_type=F32)


def _split2(v):
    hi = v.astype(BF16)
    lo = (v - hi.astype(F32)).astype(BF16)
    return hi, lo


def _split3(v):
    h1 = v.astype(BF16)
    r1 = v - h1.astype(F32)
    h2 = r1.astype(BF16)
    h3 = (r1 - h2.astype(F32)).astype(BF16)
    return h1, h2, h3


def _ind(cond):
    return jnp.where(cond, _ONE, _ZERO)


def _silu(g):
    return g * jax.nn.sigmoid(g)


def _rms(x, g):
    var = jnp.mean(x * x, axis=-1, keepdims=True)
    return (x * lax.rsqrt(var + EPS)) * g


def _rmsnorm_kernel(x_ref, g_ref, o_ref):
    o_ref[...] = _rms(x_ref[...], g_ref[...]).astype(o_ref.dtype)


def _rmsnorm(x, g, out_dtype, tm=256):
    n, d = x.shape
    return pl.pallas_call(
        _rmsnorm_kernel,
        grid=(n // tm,),
        in_specs=[pl.BlockSpec((tm, d), lambda i: (i, 0)),
                  pl.BlockSpec((1, d), lambda i: (0, 0))],
        out_specs=pl.BlockSpec((tm, d), lambda i: (i, 0)),
        out_shape=jax.ShapeDtypeStruct((n, d), out_dtype),
        compiler_params=_cparams(("parallel",)),
        name="rmsnorm",
    )(x, g.reshape(1, d))


def _prep_kernel(x_ref, g_ref, whi_ref, wlo_ref, phi_ref, plo_ref, xn_ref, cat_ref, wt_ref):
    xn = _rms(x_ref[...], g_ref[...])
    hi, lo = _split2(xn)
    xn_ref[...] = hi
    acc = _dot_nt(hi, whi_ref[...]) + _dot_nt(hi, wlo_ref[...]) + _dot_nt(lo, whi_ref[...])
    vh, vl = _split2(acc)
    cat_ref[...] = (_dot(vh, phi_ref[...]) + _dot(vl, plo_ref[...])).astype(BF16)
    gate_row = IDX_DIM
    wt_ref[...] = acc[:, IDX_W_IN - LANE:].T[gate_row:gate_row + 8, :]


IDX_W_IN = 384


def _idx_placement():
    n_out = (IDX_HEADS + 1) * IDX_SLOT
    p_hi = np.zeros((IDX_W_IN, n_out), np.float32)
    p_lo = np.zeros((IDX_W_IN, n_out), np.float32)
    e = np.arange(IDX_DIM)
    for j in range(IDX_HEADS):
        src, dst = j * IDX_DIM + e, j * IDX_SLOT + e
        p_hi[src, dst] = 1
        p_hi[src, dst + IDX_DIM] = 1
        p_lo[src, dst + 2 * IDX_DIM] = 1
    src, dst = IDX_HEADS * IDX_DIM + e, IDX_HEADS * IDX_SLOT + e
    p_hi[src, dst] = 1
    p_lo[src, dst + IDX_DIM] = 1
    p_hi[src, dst + 2 * IDX_DIM] = 1
    return jnp.asarray(p_hi, BF16), jnp.asarray(p_lo, BF16)


def _prep(x, g, whi, wlo, tm=256):
    n, d = x.shape
    p_hi, p_lo = _idx_placement()
    co = p_hi.shape[1]
    return pl.pallas_call(
        _prep_kernel,
        grid=(n // tm,),
        in_specs=[pl.BlockSpec((tm, d), lambda i: (i, 0)),
                  pl.BlockSpec((1, d), lambda i: (0, 0)),
                  pl.BlockSpec((IDX_W_IN, d), lambda i: (0, 0)),
                  pl.BlockSpec((IDX_W_IN, d), lambda i: (0, 0)),
                  pl.BlockSpec((IDX_W_IN, co), lambda i: (0, 0)),
                  pl.BlockSpec((IDX_W_IN, co), lambda i: (0, 0))],
        out_specs=[pl.BlockSpec((tm, d), lambda i: (i, 0)),
                   pl.BlockSpec((tm, co), lambda i: (i, 0)),
                   pl.BlockSpec((8, tm), lambda i: (0, i))],
        out_shape=[jax.ShapeDtypeStruct((n, d), BF16),
                   jax.ShapeDtypeStruct((n, co), BF16),
                   jax.ShapeDtypeStruct((8, n), F32)],
        compiler_params=_cparams(("parallel",)),
        name="prep",
    )(x, g.reshape(1, d), whi, wlo, p_hi, p_lo)


def _transpose_cast_kernel(w_ref, o_ref):
    o_ref[...] = w_ref[...].T.astype(o_ref.dtype)


def _transpose_cast(w, name, t=512):
    d, n = w.shape
    return pl.pallas_call(
        _transpose_cast_kernel,
        grid=(n // t, d // t),
        in_specs=[pl.BlockSpec((t, t), lambda i, j: (j, i))],
        out_specs=pl.BlockSpec((t, t), lambda i, j: (i, j)),
        out_shape=jax.ShapeDtypeStruct((n, d), BF16),
        compiler_params=_cparams(("parallel", "parallel")),
        name=name,
    )(w)


def _mm_nn_kernel(a_ref, b_ref, o_ref):
    o_ref[...] = _dot(a_ref[...], b_ref[...]).astype(o_ref.dtype)


def _mm_nn(a, b, out_dtype, tm, tn, name):
    m, k = a.shape
    n = b.shape[1]
    return pl.pallas_call(
        _mm_nn_kernel,
        grid=(m // tm, n // tn),
        in_specs=[pl.BlockSpec((tm, k), lambda i, j: (i, 0)),
                  pl.BlockSpec((k, tn), lambda i, j: (0, j))],
        out_specs=pl.BlockSpec((tm, tn), lambda i, j: (i, j)),
        out_shape=jax.ShapeDtypeStruct((m, n), out_dtype),
        compiler_params=_cparams(("parallel", "parallel")),
        name=name,
    )(a, b)


def _proj_tok_kernel(blk_ref, a_ref, b_ref, s_ref, o_ref):
    del blk_ref
    o_ref[...] = (_dot_nt(a_ref[...], b_ref[...]) * s_ref[...]).astype(o_ref.dtype)


def _proj_tok(a, wt3, layer, row_blocks, colscale, tm, tn):
    m, k = a.shape
    n = len(row_blocks) * tn
    return pl.pallas_call(
        _proj_tok_kernel,
        grid_spec=pltpu.PrefetchScalarGridSpec(
            num_scalar_prefetch=1,
            grid=(m // tm, n // tn),
            in_specs=[pl.BlockSpec((tm, k), lambda i, j, blk: (i, 0)),
                      pl.BlockSpec((None, tn, k), lambda i, j, blk: (layer, blk[j], 0)),
                      pl.BlockSpec((1, tn), lambda i, j, blk: (0, j))],
            out_specs=pl.BlockSpec((tm, tn), lambda i, j, blk: (i, j))),
        out_shape=jax.ShapeDtypeStruct((m, n), BF16),
        compiler_params=_cparams(("parallel", "parallel")),
        name="proj_tok",
    )(jnp.asarray(row_blocks, jnp.int32), a, wt3, colscale)


def _proj_vt_kernel(blk_ref, a_ref, b_ref, o_ref):
    del blk_ref
    o_ref[...] = _dot_nt(a_ref[...], b_ref[...]).astype(o_ref.dtype)


def _proj_vt(wt3, layer, row_blocks, b, tm, tn):
    n, k = b.shape
    m = len(row_blocks) * tm
    return pl.pallas_call(
        _proj_vt_kernel,
        grid_spec=pltpu.PrefetchScalarGridSpec(
            num_scalar_prefetch=1,
            grid=(n // tn, m // tm),
            in_specs=[pl.BlockSpec((None, tm, k), lambda j, i, blk: (layer, blk[i], 0)),
                      pl.BlockSpec((tn, k), lambda j, i, blk: (j, 0))],
            out_specs=pl.BlockSpec((tm, tn), lambda j, i, blk: (i, j))),
        out_shape=jax.ShapeDtypeStruct((m, n), BF16),
        compiler_params=_cparams(("parallel", "parallel")),
        name="proj_vt",
    )(jnp.asarray(row_blocks, jnp.int32), wt3, b)


def _mm_nt_kernel(a_ref, b_ref, o_ref):
    o_ref[...] = _dot_nt(a_ref[...], b_ref[...]).astype(o_ref.dtype)


def _mm_nt(a, b, out_dtype, tm, tn, name):
    m, k = a.shape
    n = b.shape[0]
    return pl.pallas_call(
        _mm_nt_kernel,
        grid=(n // tn, m // tm),
        in_specs=[pl.BlockSpec((tm, k), lambda j, i: (i, 0)),
                  pl.BlockSpec((tn, k), lambda j, i: (j, 0))],
        out_specs=pl.BlockSpec((tm, tn), lambda j, i: (i, j)),
        out_shape=jax.ShapeDtypeStruct((m, n), out_dtype),
        compiler_params=_cparams(("parallel", "parallel")),
        name=name,
    )(a, b)


def _mm_res_kernel(a_ref, b_ref, r_ref, o_ref):
    o_ref[...] = r_ref[...] + _dot(a_ref[...], b_ref[...])


def _mm_res(a, b, res, tm, tn):
    m, k = a.shape
    n = b.shape[1]
    return pl.pallas_call(
        _mm_res_kernel,
        grid=(m // tm, n // tn),
        in_specs=[pl.BlockSpec((tm, k), lambda i, j: (i, 0)),
                  pl.BlockSpec((k, tn), lambda i, j: (0, j)),
                  pl.BlockSpec((tm, tn), lambda i, j: (i, j))],
        out_specs=pl.BlockSpec((tm, tn), lambda i, j: (i, j)),
        out_shape=jax.ShapeDtypeStruct((m, n), F32),
        compiler_params=_cparams(("parallel", "parallel")),
        name="out_proj",
    )(a, b, res)


def _merge_kernel(ya_ref, yc_ref, ys_ref, ym_ref, ra_ref, rc_ref, rs_ref, rm_ref, wb_ref, o_ref):
    ys = (ya_ref, yc_ref, ys_ref, ym_ref)
    rs = (ra_ref, rc_ref, rs_ref, rm_ref)
    acc = None
    for br in range(4):
        z = _dot(ys[br][...], wb_ref[br])
        term = jax.nn.sigmoid(rs[br][...].astype(F32)) * z
        acc = term if acc is None else acc + term
    o_ref[...] = acc.astype(o_ref.dtype)


def _merge(ys, act, r_col0, wb, tm, tn):
    n, w = ys[0].shape
    d = wb.shape[2]
    nj = d // tn
    y_specs = [pl.BlockSpec((tm, w), lambda i, j: (i, 0)) for _ in range(4)]
    r_specs = [pl.BlockSpec((tm, tn), functools.partial(
        lambda i, j, base: (i, base + j), base=(r_col0 + br * d) // tn)) for br in range(4)]
    del nj
    return pl.pallas_call(
        _merge_kernel,
        grid=(n // tm, d // tn),
        in_specs=y_specs + r_specs + [pl.BlockSpec((4, w, tn), lambda i, j: (0, 0, j))],
        out_specs=pl.BlockSpec((tm, tn), lambda i, j: (i, j)),
        out_shape=jax.ShapeDtypeStruct((n, d), BF16),
        compiler_params=_cparams(("parallel", "parallel")),
        name="merge",
    )(*ys, act, act, act, act, wb)


def _ceil_div_unroll(n):
    return (n + (ATT_UNROLL - 1)) >> (ATT_UNROLL.bit_length() - 1)


def _two_pass_attention(q_list, k_ref, vt_ref, sc_ref, acc_ref, bias_fn, trips, blk, m0_list, l0_list):
    heads = range(len(q_list))
    hd = HEAD_DIM
    span = ATT_UNROLL * blk

    def pass1(j, ms):
        base = pl.multiple_of(j * span, span)
        out = []
        for h in heads:
            s = _dot_nt(k_ref[pl.ds(base, span), h * hd:(h + 1) * hd], q_list[h])
            m = ms[h]
            for u in range(ATT_UNROLL):
                su = s[u * blk:(u + 1) * blk, :] + bias_fn(h, j * ATT_UNROLL + u)
                sc_ref[h, pl.ds(pl.multiple_of(base + u * blk, blk), blk), :] = su
                m = jnp.maximum(m, jnp.max(su, axis=0, keepdims=True))
            out.append(m)
        return tuple(out)

    ms = lax.fori_loop(0, trips, pass1, tuple(m0_list))

    ls = []
    for h in heads:
        alpha = jnp.exp2(m0_list[h] - ms[h])
        ls.append(alpha * l0_list[h])
        acc_ref[h] = alpha * acc_ref[h]

    def pass2(j, ls):
        base = pl.multiple_of(j * span, span)
        out = []
        for h in heads:
            p = jnp.exp2(sc_ref[h, pl.ds(base, span), :] - ms[h])
            out.append(ls[h] + jnp.sum(p, axis=0, keepdims=True))
            acc_ref[h] += _dot(vt_ref[h * hd:(h + 1) * hd, pl.ds(base, span)], p.astype(BF16))
        return tuple(out)

    return lax.fori_loop(0, trips, pass2, tuple(ls))


def _moba_kernel(q_ref, k_ref, vt_ref, g_ref, o_ref, kmean_ref, selb_ref, sc_ref, acc_ref, *, nblk):
    i = pl.program_id(2)
    blk = MOBA_BLOCK
    hd = HEAD_DIM
    heads = range(HEADS_PER_STEP)

    @pl.when(i == 0)
    def _():
        for h in heads:
            kf = k_ref[:, h * hd:(h + 1) * hd].astype(F32).reshape(nblk, blk, hd)
            kmean_ref[h] = jnp.mean(kf, axis=1)

    off = pl.multiple_of(i * blk, blk)
    q_list, m0_list, l0_list = [], [], []
    for h in heads:
        q = q_ref[:, h * hd:(h + 1) * hd]
        q_list.append(q)
        k1, k2, k3 = _split3(kmean_ref[h])
        bs = _dot_nt(k1, q) + _dot_nt(k2, q) + _dot_nt(k3, q)
        n_iota = lax.broadcasted_iota(jnp.int32, bs.shape, 0)
        past = n_iota < i
        left = jnp.where(past, bs, -jnp.inf)
        bias = jnp.full(bs.shape, _NEG, F32)
        for _ in range(MOBA_TOPK):
            top = jnp.max(left, axis=0, keepdims=True)
            first = jnp.min(jnp.where(left == top, n_iota, nblk), axis=0, keepdims=True)
            hit = n_iota == first
            bias = jnp.where(hit & past, _ZERO, bias)
            left = jnp.where(hit, -jnp.inf, left)
        selb_ref[h] = bias

        st = _dot_nt(k_ref[pl.ds(off, blk), h * hd:(h + 1) * hd], q)
        kpos = lax.broadcasted_iota(jnp.int32, st.shape, 0)
        qpos = lax.broadcasted_iota(jnp.int32, st.shape, 1)
        st = jnp.where(kpos <= qpos, st, _NEG)
        m0 = jnp.max(st, axis=0, keepdims=True)
        p0 = jnp.exp2(st - m0)
        m0_list.append(m0)
        l0_list.append(jnp.sum(p0, axis=0, keepdims=True))
        acc_ref[h] = _dot(vt_ref[h * hd:(h + 1) * hd, pl.ds(off, blk)], p0.astype(BF16))

    ls = _two_pass_attention(
        q_list, k_ref, vt_ref, sc_ref, acc_ref, lambda h, n: selb_ref[h, pl.ds(n, 1), :],
        _ceil_div_unroll(i), blk, m0_list, l0_list)
    for h in heads:
        o = (acc_ref[h] / ls[h]).T
        g = g_ref[:, h * hd:(h + 1) * hd].astype(F32)
        o_ref[:, h * hd:(h + 1) * hd] = (o * _silu(g)).astype(o_ref.dtype)


def _moba(act, act_t, batch, seq, q_col, k_col, g_col, v_row):
    nblk = seq // MOBA_BLOCK
    n = act.shape[0]
    blk = MOBA_BLOCK
    hp = HEADS_PER_STEP
    w = hp * HEAD_DIM
    q_cb, k_cb, g_cb, v_rb = q_col // w, k_col // w, g_col // w, v_row // w
    return pl.pallas_call(
        functools.partial(_moba_kernel, nblk=nblk),
        grid=(batch, MOBA_HEADS // hp, nblk),
        in_specs=[pl.BlockSpec((blk, w), lambda b, h, i: (b * nblk + i, q_cb + h)),
                  pl.BlockSpec((seq, w), lambda b, h, i: (b, k_cb + h)),
                  pl.BlockSpec((w, seq), lambda b, h, i: (v_rb + h, b)),
                  pl.BlockSpec((blk, w), lambda b, h, i: (b * nblk + i, g_cb + h))],
        out_specs=pl.BlockSpec((blk, w), lambda b, h, i: (b * nblk + i, h)),
        out_shape=jax.ShapeDtypeStruct((n, BRANCH_W), BF16),
        scratch_shapes=[pltpu.VMEM((hp, nblk, HEAD_DIM), F32),
                        pltpu.VMEM((hp, nblk, blk), F32),
                        pltpu.VMEM((hp, seq, blk), F32),
                        pltpu.VMEM((hp, HEAD_DIM, blk), F32)],
        compiler_params=_cparams(("parallel", "parallel", "arbitrary")),
        name="moba",
    )(act, act, act_t, act)


def _conv_kernel(cb_ref, cc_ref, ch_ref, cg_ref, w_ref, bias_ref, o_ref):
    u = cc_ref[...].astype(F32) * ch_ref[...].astype(F32)
    t = lax.broadcasted_iota(jnp.int32, u.shape, 0)
    u1 = jnp.where(t >= 1, pltpu.roll(u, 1, 0), 0.0)
    u2 = jnp.where(t >= 2, pltpu.roll(u, 2, 0), 0.0)
    w = w_ref[...]
    conv = u2 * w[0:1, :] + u1 * w[1:2, :] + u * w[2:3, :] + bias_ref[...]
    o_ref[...] = (cb_ref[...].astype(F32) * conv * _silu(cg_ref[...].astype(F32))).astype(o_ref.dtype)


def _conv(act, conv_w, conv_b, batch, seq, cb_cb, cc_cb, ch_cb, cg_cb):
    n = act.shape[0]
    ncb = BRANCH_W // LANE
    kw = conv_w.shape[0]
    wpad = jnp.zeros((8, BRANCH_W), F32).at[:kw].set(conv_w)

    def spec(base):
        return pl.BlockSpec((seq, LANE), lambda b, c: (b, base + c))

    return pl.pallas_call(
        _conv_kernel,
        grid=(batch, ncb),
        in_specs=[spec(cb_cb), spec(cc_cb), spec(ch_cb), spec(cg_cb),
                  pl.BlockSpec((8, LANE), lambda b, c: (0, c)),
                  pl.BlockSpec((1, LANE), lambda b, c: (0, c))],
        out_specs=pl.BlockSpec((seq, LANE), lambda b, c: (b, c)),
        out_shape=jax.ShapeDtypeStruct((n, BRANCH_W), BF16),
        compiler_params=_cparams(("parallel", "parallel")),
        name="conv",
    )(act, act, act, act, wpad, conv_b.reshape(1, BRANCH_W))


def _dsa_kernel(qc_ref, kc_ref, wt_ref, q_ref, k_ref, vt_ref, g_ref, o_ref, st_ref, tri_ref, sc_ref,
                acc_ref, *, topk, tq):
    i = pl.program_id(1)
    nch = i + 1
    npair = (nch + 1) >> 1
    trips = _ceil_div_unroll(nch)
    idx_scale = (IDX_DIM ** -0.5) * (IDX_HEADS ** -0.5)
    kf = float(topk)
    ninf = np.float32(-np.inf)

    def chunk(c):
        return pl.ds(pl.multiple_of(c * tq, tq), tq)

    def pair(c):
        return pl.ds(pl.multiple_of(c * (2 * tq), 2 * tq), 2 * tq)

    def fold8(x, op):
        return op(x.reshape(x.shape[0] // 8, 8, tq), axis=0)

    def count(pred):
        def body(c, acc):
            for u in range(2):
                acc = acc + fold8(_ind(pred(st_ref[chunk(2 * c + u), :])), jnp.sum)
            return acc
        acc = lax.fori_loop(0, npair, body, jnp.zeros((8, tq), F32))
        return jnp.sum(acc, axis=0, keepdims=True)

    @pl.when(pl.program_id(2) == 0)
    def _select():
        def score_body(c, carry):
            mx, mn = carry
            for u in range(2):
                cc = 2 * c + u
                kc = kc_ref[chunk(cc), :]
                sc = jnp.zeros((tq, tq), F32)
                for j in range(IDX_HEADS):
                    lg = _dot_nt(kc, qc_ref[:, j * IDX_SLOT:(j + 1) * IDX_SLOT])
                    sc = sc + (wt_ref[j:j + 1, :] * idx_scale) * jnp.maximum(lg, 0.0)
                kpos = cc * tq + lax.broadcasted_iota(jnp.int32, sc.shape, 0)
                qpos = i * tq + lax.broadcasted_iota(jnp.int32, sc.shape, 1)
                causal = kpos <= qpos
                st_ref[chunk(cc), :] = jnp.where(causal, sc, ninf)
                mx = jnp.maximum(mx, fold8(jnp.where(causal, sc, ninf), jnp.max))
                mn = jnp.minimum(mn, fold8(jnp.where(causal, sc, -ninf), jnp.min))
            return mx, mn
        mx8, mn8 = lax.fori_loop(0, npair, score_body,
                                 (jnp.full((8, tq), ninf, F32), jnp.full((8, tq), -ninf, F32)))
        smax = jnp.max(mx8, axis=0, keepdims=True)
        smin = jnp.min(mn8, axis=0, keepdims=True)

        qrow = i * tq + lax.broadcasted_iota(jnp.int32, (1, tq), 1)
        short = qrow + 1 < topk
        lo0 = jnp.where(short, _ZERO, smin)
        hi0 = jnp.where(short, _ONE, smax + jnp.maximum(jnp.abs(smax), np.float32(1e-30))
                        * np.float32(2.0 ** -10))

        def bis_body(_, carry):
            lo, hi = carry
            mid = lo + (hi - lo) * np.float32(0.5)
            ge = count(lambda s: s >= mid) >= kf
            return jnp.where(ge, mid, lo), jnp.where(ge, hi, mid)

        _, hi_f0 = lax.fori_loop(0, BISECT_ITERS, bis_body, (lo0, hi0))

        def walk_cond(carry):
            return jnp.min(carry[3]) < 0.5

        def walk_body(carry):
            hi_f, thr, cnt, done = carry

            def mx_body(c, acc):
                s = st_ref[pair(c), :]
                return jnp.maximum(acc, fold8(jnp.where(s < hi_f, s, ninf), jnp.max))
            mx = lax.fori_loop(0, npair, mx_body, jnp.full((8, tq), ninf, F32))
            v = jnp.max(mx, axis=0, keepdims=True)
            c_v = count(lambda s: s >= v)
            is_done = done > 0.5
            return (jnp.where(is_done, hi_f, v), jnp.where(is_done, thr, v),
                    jnp.where(is_done, cnt, c_v), _ind(is_done | (c_v >= kf)))

        zero_row = jnp.zeros((1, tq), F32)
        all_rows = (2 * npair * tq).astype(F32)
        _, thr, n_ge, _ = lax.while_loop(
            walk_cond, walk_body,
            (hi_f0, jnp.full((1, tq), ninf, F32), zero_row + all_rows, _ind(short)))

        n_gt = count(lambda s: s > thr)
        need = kf - n_gt
        n_eq = n_ge - n_gt

        def bias_all_ties(c, _):
            st_ref[pair(c), :] = jnp.where(st_ref[pair(c), :] >= thr, _ZERO, _NEG)
            return 0

        def bias_some_ties(c, carry):
            for u in range(2):
                cc = 2 * c + u
                s = st_ref[chunk(cc), :]
                eq = s == thr
                eqf = _ind(eq)
                before = _dot(tri_ref[...], eqf.astype(BF16)) + carry
                kpos = cc * tq + lax.broadcasted_iota(jnp.int32, s.shape, 0)
                qpos = i * tq + lax.broadcasted_iota(jnp.int32, s.shape, 1)
                selected = ((s > thr) | (eq & (before < need))) & (kpos <= qpos)
                st_ref[chunk(cc), :] = jnp.where(selected, _ZERO, _NEG)
                carry = carry + jnp.sum(eqf, axis=0, keepdims=True)
            return carry

        def all_ties():
            lax.fori_loop(0, npair, bias_all_ties, 0)

        def some_ties():
            r = lax.broadcasted_iota(jnp.int32, (tq, tq), 0)
            cidx = lax.broadcasted_iota(jnp.int32, (tq, tq), 1)
            tri_ref[...] = _ind(cidx < r).astype(BF16)
            lax.fori_loop(0, npair, bias_some_ties, zero_row)

        simple = jnp.logical_and(jnp.max(n_eq - need) <= 0.0, jnp.min(thr) > ninf)
        lax.cond(simple, all_ties, some_ties)

        def pad_body(c, _):
            st_ref[chunk(c), :] = jnp.full((tq, tq), _NEG, F32)
            return 0
        lax.fori_loop(2 * npair, trips * ATT_UNROLL, pad_body, 0)

    hd = HEAD_DIM
    heads = range(HEADS_PER_STEP)
    q_list = [q_ref[:, h * hd:(h + 1) * hd] for h in heads]
    for h in heads:
        acc_ref[h] = jnp.zeros((hd, tq), F32)
    ls = _two_pass_attention(
        q_list, k_ref, vt_ref, sc_ref, acc_ref, lambda h, c: st_ref[chunk(c), :], trips, tq,
        [jnp.full((1, tq), _NEG, F32)] * HEADS_PER_STEP, [jnp.zeros((1, tq), F32)] * HEADS_PER_STEP)
    for h in heads:
        o = (acc_ref[h] / ls[h]).T
        g = g_ref[:, h * hd:(h + 1) * hd].astype(F32)
        o_ref[:, h * hd:(h + 1) * hd] = (o * _silu(g)).astype(o_ref.dtype)


def _dsa(act, act_t, cat, wt, batch, seq, q_col, k_col, g_col, v_row, tq=256):
    n = act.shape[0]
    nq = seq // tq
    topk = min(DSA_TOPK_MAX, seq // 4)
    n_qcat = IDX_HEADS * IDX_SLOT
    hp = HEADS_PER_STEP
    w = hp * HEAD_DIM
    q_cb, k_cb, g_cb, v_rb = q_col // w, k_col // w, g_col // w, v_row // w
    return pl.pallas_call(
        functools.partial(_dsa_kernel, topk=topk, tq=tq),
        grid=(batch, nq, DSA_HEADS // hp),
        in_specs=[pl.BlockSpec((tq, n_qcat), lambda b, i, h: (b * nq + i, 0)),
                  pl.BlockSpec((seq, IDX_SLOT), lambda b, i, h: (b, n_qcat // IDX_SLOT)),
                  pl.BlockSpec((8, tq), lambda b, i, h: (0, b * nq + i)),
                  pl.BlockSpec((tq, w), lambda b, i, h: (b * nq + i, q_cb + h)),
                  pl.BlockSpec((seq, w), lambda b, i, h: (b, k_cb + h)),
                  pl.BlockSpec((w, seq), lambda b, i, h: (v_rb + h, b)),
                  pl.BlockSpec((tq, w), lambda b, i, h: (b * nq + i, g_cb + h))],
        out_specs=pl.BlockSpec((tq, w), lambda b, i, h: (b * nq + i, h)),
        out_shape=jax.ShapeDtypeStruct((n, BRANCH_W), BF16),
        scratch_shapes=[pltpu.VMEM((seq, tq), F32),
                        pltpu.VMEM((tq, tq), BF16),
                        pltpu.VMEM((hp, seq, tq), F32),
                        pltpu.VMEM((hp, HEAD_DIM, tq), F32)],
        compiler_params=_cparams(("parallel", "arbitrary", "arbitrary")),
        name="dsa",
    )(cat, cat, wt, act, act, act_t, act)


def _mem_kernel(q_ref, mk_ref, mvt_ref, g_ref, o_ref):
    s = _dot_nt(mk_ref[...], q_ref[...])
    m = jnp.max(s, axis=0, keepdims=True)
    p = jnp.exp2(s - m)
    l = jnp.sum(p, axis=0, keepdims=True)
    o = (_dot(mvt_ref[...], p.astype(BF16)) / l).T
    o_ref[...] = (o * _silu(g_ref[...].astype(F32))).astype(o_ref.dtype)


def _mem_attn(act, mk, mvt, batch, seq, mem_len, q_cb, g_cb, tq=512):
    n = act.shape[0]
    nq = seq // tq
    hd = MEM_HEAD_DIM
    return pl.pallas_call(
        _mem_kernel,
        grid=(batch, nq, MEM_HEADS),
        in_specs=[pl.BlockSpec((tq, hd), lambda b, i, h: (b * nq + i, q_cb + h)),
                  pl.BlockSpec((mem_len, hd), lambda b, i, h: (b, h)),
                  pl.BlockSpec((hd, mem_len), lambda b, i, h: (h, b)),
                  pl.BlockSpec((tq, hd), lambda b, i, h: (b * nq + i, g_cb + h))],
        out_specs=pl.BlockSpec((tq, hd), lambda b, i, h: (b * nq + i, h)),
        out_shape=jax.ShapeDtypeStruct((n, BRANCH_W), BF16),
        compiler_params=_cparams(("parallel", "parallel", "parallel")),
        name="mem_attn",
    )(act, mk, mvt, act)


O_IDX = 14 * BRANCH_W
O_MERGE = O_IDX + IDX_HEADS * IDX_DIM + IDX_DIM + IDX_HEADS


def _input_weights(w_in3):
    wt3 = jnp.swapaxes(w_in3, 1, 2).astype(BF16)
    idx3 = jnp.swapaxes(w_in3[:, :, O_IDX:O_MERGE], 1, 2)
    idx3 = jnp.pad(idx3, ((0, 0), (0, IDX_W_IN - idx3.shape[1]), (0, 0)))
    return wt3, idx3


def _layer_weights(layer, wt3, idx3, w_mem_kv3, w_branch3, w_out3):
    bw = BRANCH_W
    w_idx_hi, w_idx_lo = _split2(idx3[layer])
    return dict(w_merge=wt3[layer, O_MERGE:], w_idx_hi=w_idx_hi, w_idx_lo=w_idx_lo,
                w_mk=w_mem_kv3[layer][:, :bw].astype(BF16),
                w_mvt=_transpose_cast(w_mem_kv3[layer][:, bw:], "w_mvt"),
                w_branch=w_branch3[layer].astype(BF16), w_out=w_out3[layer].astype(BF16))


def _layer(layer, x2, mem2, batch, seq, mem_len, ln_g, conv_w, conv_b, mem_ln_g, wt3, w):
    nb = BRANCH_W // LANE
    bw = BRANCH_W
    (A_Q, A_K, A_G, C_B, C_C, C_H, C_G, S_Q, S_K, S_G, M_Q, M_G) = range(12)
    tok_groups = (0, 1, 3, 4, 5, 6, 7, 8, 9, 11, 12, 13)
    val_groups = (2, 10)

    log2e = float(np.log2(np.e))
    colscale = np.ones((1, len(tok_groups) * bw), np.float32)
    for col, hdim in ((A_Q, HEAD_DIM), (S_Q, HEAD_DIM), (M_Q, MEM_HEAD_DIM)):
        colscale[:, col * bw:(col + 1) * bw] = hdim ** -0.5 * log2e

    xn, cat, wt = _prep(x2, ln_g, w["w_idx_hi"], w["w_idx_lo"])
    act = _proj_tok(xn, wt3, layer, tok_groups, jnp.asarray(colscale), tm=1024, tn=bw)
    act_r = _mm_nt(xn, w["w_merge"], BF16, tm=1024, tn=1024, name="proj_merge")
    act_t = _proj_vt(wt3, layer, val_groups, xn, tm=bw, tn=1024)

    y_a = _moba(act, act_t, batch, seq, A_Q * bw, A_K * bw, A_G * bw, 0)
    y_c = _conv(act, conv_w, conv_b, batch, seq, C_B * nb, C_C * nb, C_H * nb, C_G * nb)
    y_s = _dsa(act, act_t, cat, wt, batch, seq, S_Q * bw, S_K * bw, S_G * bw, bw)

    mem_n = _rmsnorm(mem2, mem_ln_g, BF16)
    mk = _mm_nn(mem_n, w["w_mk"], BF16, tm=mem2.shape[0], tn=512, name="mem_k")
    mvt = _mm_nt(w["w_mvt"], mem_n, BF16, tm=512, tn=mem2.shape[0], name="mem_vt")
    mb = BRANCH_W // MEM_HEAD_DIM
    y_m = _mem_attn(act, mk, mvt, batch, seq, mem_len, M_Q * mb, M_G * mb)

    merged = _merge((y_a, y_c, y_s, y_m), act_r, 0, w["w_branch"], tm=512, tn=512)
    return _mm_res(merged, w["w_out"], x2, tm=512, tn=1024)


def kernel(x, mem, ln_g, w_in, conv_w, conv_b, mem_ln_g, w_mem_kv, w_branch, w_out, final_g):
    batch, seq, d = x.shape
    mem_len = mem.shape[1]
    x2 = x.reshape(batch * seq, d)
    mem2 = mem.reshape(batch * mem_len, d)
    wt3, idx3 = _input_weights(w_in)
    for layer in range(ln_g.shape[0]):
        w = _layer_weights(layer, wt3, idx3, w_mem_kv, w_branch, w_out)
        x2 = _layer(layer, x2, mem2, batch, seq, mem_len, ln_g[layer], conv_w[layer],
                    conv_b[layer], mem_ln_g[layer], wt3, w)
    return _rmsnorm(x2, final_g, x.dtype).reshape(batch, seq, d)
```

```python
import functools

import numpy as np
import jax
import jax.numpy as jnp
from jax import lax
from jax.experimental import pallas as pl
from jax.experimental.pallas import tpu as pltpu

EPS = 1e-6
BRANCH_W = 1024
MOBA_HEADS = 8
MOBA_BLOCK = 256
MOBA_TOPK = 3
DSA_HEADS = 8
DSA_TOPK_MAX = 256
IDX_HEADS = 4
IDX_DIM = 64
MEM_HEADS = 4
HEAD_DIM = 128
MEM_HEAD_DIM = 256
IDX_SLOT = 256
NEG = -1e30
ATT_UNROLL = 4
HEADS_PER_STEP = 4
BISECT_ITERS = 22
LANE = 128
VMEM_LIMIT = 56 * 1024 * 1024

F32 = jnp.float32
BF16 = jnp.bfloat16
_ONE = np.float32(1.0)
_ZERO = np.float32(0.0)
_NEG = np.float32(NEG)


def _cparams(sem):
    return pltpu.CompilerParams(dimension_semantics=sem, vmem_limit_bytes=VMEM_LIMIT)


def _dot(a, b):
    return jnp.dot(a, b, preferred_element_type=F32)


def _dot_nt(a, b):
    return lax.dot_general(a, b, (((1,), (1,)), ((), ())), preferred_element_type=F32)


def _split2(v):
    hi = v.astype(BF16)
    lo = (v - hi.astype(F32)).astype(BF16)
    return hi, lo


def _split3(v):
    h1 = v.astype(BF16)
    r1 = v - h1.astype(F32)
    h2 = r1.astype(BF16)
    h3 = (r1 - h2.astype(F32)).astype(BF16)
    return h1, h2, h3


def _ind(cond):
    return jnp.where(cond, _ONE, _ZERO)


def _silu(g):
    return g * jax.nn.sigmoid(g)


def _rms(x, g):
    var = jnp.mean(x * x, axis=-1, keepdims=True)
    return (x * lax.rsqrt(var + EPS)) * g


def _rmsnorm_kernel(x_ref, g_ref, o_ref):
    o_ref[...] = _rms(x_ref[...], g_ref[...]).astype(o_ref.dtype)


def _rmsnorm(x, g, out_dtype, tm=256):
    n, d = x.shape
    return pl.pallas_call(
        _rmsnorm_kernel,
        grid=(n // tm,),
        in_specs=[pl.BlockSpec((tm, d), lambda i: (i, 0)),
                  pl.BlockSpec((1, d), lambda i: (0, 0))],
        out_specs=pl.BlockSpec((tm, d), lambda i: (i, 0)),
        out_shape=jax.ShapeDtypeStruct((n, d), out_dtype),
        compiler_params=_cparams(("parallel",)),
        name="rmsnorm",
    )(x, g.reshape(1, d))


def _prep_kernel(x_ref, g_ref, whi_ref, wlo_ref, phi_ref, plo_ref, xn_ref, cat_ref, wt_ref):
    xn = _rms(x_ref[...], g_ref[...])
    hi, lo = _split2(xn)
    xn_ref[...] = hi
    acc = _dot_nt(hi, whi_ref[...]) + _dot_nt(hi, wlo_ref[...]) + _dot_nt(lo, whi_ref[...])
    vh, vl = _split2(acc)
    cat_ref[...] = (_dot(vh, phi_ref[...]) + _dot(vl, plo_ref[...])).astype(BF16)
    gate_row = IDX_DIM
    wt_ref[...] = acc[:, IDX_W_IN - LANE:].T[gate_row:gate_row + 8, :]


IDX_W_IN = 384


def _idx_placement():
    n_out = (IDX_HEADS + 1) * IDX_SLOT
    p_hi = np.zeros((IDX_W_IN, n_out), np.float32)
    p_lo = np.zeros((IDX_W_IN, n_out), np.float32)
    e = np.arange(IDX_DIM)
    for j in range(IDX_HEADS):
        src, dst = j * IDX_DIM + e, j * IDX_SLOT + e
        p_hi[src, dst] = 1
        p_hi[src, dst + IDX_DIM] = 1
        p_lo[src, dst + 2 * IDX_DIM] = 1
    src, dst = IDX_HEADS * IDX_DIM + e, IDX_HEADS * IDX_SLOT + e
    p_hi[src, dst] = 1
    p_lo[src, dst + IDX_DIM] = 1
    p_hi[src, dst + 2 * IDX_DIM] = 1
    return jnp.asarray(p_hi, BF16), jnp.asarray(p_lo, BF16)


def _prep(x, g, whi, wlo, tm=512):
    n, d = x.shape
    p_hi, p_lo = _idx_placement()
    co = p_hi.shape[1]
    return pl.pallas_call(
        _prep_kernel,
        grid=(n // tm,),
        in_specs=[pl.BlockSpec((tm, d), lambda i: (i, 0)),
                  pl.BlockSpec((1, d), lambda i: (0, 0)),
                  pl.BlockSpec((IDX_W_IN, d), lambda i: (0, 0)),
                  pl.BlockSpec((IDX_W_IN, d), lambda i: (0, 0)),
                  pl.BlockSpec((IDX_W_IN, co), lambda i: (0, 0)),
                  pl.BlockSpec((IDX_W_IN, co), lambda i: (0, 0))],
        out_specs=[pl.BlockSpec((tm, d), lambda i: (i, 0)),
                   pl.BlockSpec((tm, co), lambda i: (i, 0)),
                   pl.BlockSpec((8, tm), lambda i: (0, i))],
        out_shape=[jax.ShapeDtypeStruct((n, d), BF16),
                   jax.ShapeDtypeStruct((n, co), BF16),
                   jax.ShapeDtypeStruct((8, n), F32)],
        compiler_params=_cparams(("parallel",)),
        name="prep",
    )(x, g.reshape(1, d), whi, wlo, p_hi, p_lo)


def _transpose_cast_kernel(w_ref, o_ref):
    o_ref[...] = w_ref[...].T.astype(o_ref.dtype)


def _transpose_cast(w, name, t=512):
    d, n = w.shape
    return pl.pallas_call(
        _transpose_cast_kernel,
        grid=(n // t, d // t),
        in_specs=[pl.BlockSpec((t, t), lambda i, j: (j, i))],
        out_specs=pl.BlockSpec((t, t), lambda i, j: (i, j)),
        out_shape=jax.ShapeDtypeStruct((n, d), BF16),
        compiler_params=_cparams(("parallel", "parallel")),
        name=name,
    )(w)


def _mm_nn_kernel(a_ref, b_ref, o_ref):
    o_ref[...] = _dot(a_ref[...], b_ref[...]).astype(o_ref.dtype)


def _mm_nn(a, b, out_dtype, tm, tn, name):
    m, k = a.shape
    n = b.shape[1]
    return pl.pallas_call(
        _mm_nn_kernel,
        grid=(m // tm, n // tn),
        in_specs=[pl.BlockSpec((tm, k), lambda i, j: (i, 0)),
                  pl.BlockSpec((k, tn), lambda i, j: (0, j))],
        out_specs=pl.BlockSpec((tm, tn), lambda i, j: (i, j)),
        out_shape=jax.ShapeDtypeStruct((m, n), out_dtype),
        compiler_params=_cparams(("parallel", "parallel")),
        name=name,
    )(a, b)


def _proj_tok_kernel(blk_ref, a_ref, b_ref, s_ref, o_ref):
    del blk_ref
    o_ref[...] = (_dot_nt(a_ref[...], b_ref[...]) * s_ref[...]).astype(o_ref.dtype)


def _proj_tok(a, wt3, layer, row_blocks, colscale, tm, tn):
    m, k = a.shape
    n = len(row_blocks) * tn
    return pl.pallas_call(
        _proj_tok_kernel,
        grid_spec=pltpu.PrefetchScalarGridSpec(
            num_scalar_prefetch=1,
            grid=(m // tm, n // tn),
            in_specs=[pl.BlockSpec((tm, k), lambda i, j, blk: (i, 0)),
                      pl.BlockSpec((None, tn, k), lambda i, j, blk: (layer, blk[j], 0)),
                      pl.BlockSpec((1, tn), lambda i, j, blk: (0, j))],
            out_specs=pl.BlockSpec((tm, tn), lambda i, j, blk: (i, j))),
        out_shape=jax.ShapeDtypeStruct((m, n), BF16),
        compiler_params=_cparams(("parallel", "parallel")),
        name="proj_tok",
    )(jnp.asarray(row_blocks, jnp.int32), a, wt3, colscale)


def _proj_vt_kernel(blk_ref, a_ref, b_ref, o_ref):
    del blk_ref
    o_ref[...] = _dot_nt(a_ref[...], b_ref[...]).astype(o_ref.dtype)


def _proj_vt(wt3, layer, row_blocks, b, tm, tn):
    n, k = b.shape
    m = len(row_blocks) * tm
    return pl.pallas_call(
        _proj_vt_kernel,
        grid_spec=pltpu.PrefetchScalarGridSpec(
            num_scalar_prefetch=1,
            grid=(n // tn, m // tm),
            in_specs=[pl.BlockSpec((None, tm, k), lambda j, i, blk: (layer, blk[i], 0)),
                      pl.BlockSpec((tn, k), lambda j, i, blk: (j, 0))],
            out_specs=pl.BlockSpec((tm, tn), lambda j, i, blk: (i, j))),
        out_shape=jax.ShapeDtypeStruct((m, n), BF16),
        compiler_params=_cparams(("parallel", "parallel")),
        name="proj_vt",
    )(jnp.asarray(row_blocks, jnp.int32), wt3, b)


def _mm_nt_kernel(a_ref, b_ref, o_ref):
    o_ref[...] = _dot_nt(a_ref[...], b_ref[...]).astype(o_ref.dtype)


def _mm_nt(a, b, out_dtype, tm, tn, name):
    m, k = a.shape
    n = b.shape[0]
    return pl.pallas_call(
        _mm_nt_kernel,
        grid=(n // tn, m // tm),
        in_specs=[pl.BlockSpec((tm, k), lambda j, i: (i, 0)),
                  pl.BlockSpec((tn, k), lambda j, i: (j, 0))],
        out_specs=pl.BlockSpec((tm, tn), lambda j, i: (i, j)),
        out_shape=jax.ShapeDtypeStruct((m, n), out_dtype),
        compiler_params=_cparams(("parallel", "parallel")),
        name=name,
    )(a, b)


def _out_proj_kernel(a_ref, b_ref, r_ref, g_ref, o_ref, *, final_norm):
    y = r_ref[...] + _dot(a_ref[...], b_ref[...])
    o_ref[...] = _rms(y, g_ref[...]) if final_norm else y


def _out_proj(a, b, res, final_g, tm):
    m, k = a.shape
    n = b.shape[1]
    g = jnp.ones((1, n), F32) if final_g is None else final_g.reshape(1, n)
    return pl.pallas_call(
        functools.partial(_out_proj_kernel, final_norm=final_g is not None),
        grid=(m // tm,),
        in_specs=[pl.BlockSpec((tm, k), lambda i: (i, 0)),
                  pl.BlockSpec((k, n), lambda i: (0, 0)),
                  pl.BlockSpec((tm, n), lambda i: (i, 0)),
                  pl.BlockSpec((1, n), lambda i: (0, 0))],
        out_specs=pl.BlockSpec((tm, n), lambda i: (i, 0)),
        out_shape=jax.ShapeDtypeStruct((m, n), F32),
        compiler_params=_cparams(("parallel",)),
        name="out_proj",
    )(a, b, res, g)


def _merge_kernel(ya_ref, yc_ref, ys_ref, ym_ref, ra_ref, rc_ref, rs_ref, rm_ref, wb_ref, o_ref):
    ys = (ya_ref, yc_ref, ys_ref, ym_ref)
    rs = (ra_ref, rc_ref, rs_ref, rm_ref)
    acc = None
    for br in range(4):
        z = _dot(ys[br][...], wb_ref[br])
        term = jax.nn.sigmoid(rs[br][...].astype(F32)) * z
        acc = term if acc is None else acc + term
    o_ref[...] = acc.astype(o_ref.dtype)


def _merge(ys, gates, wb, tm, tn):
    n, w = ys[0].shape
    d = wb.shape[2]
    y_specs = [pl.BlockSpec((tm, w), lambda j, i: (i, 0)) for _ in range(4)]
    r_specs = [pl.BlockSpec((tm, tn), functools.partial(
        lambda j, i, base: (i, base + j), base=br * d // tn)) for br in range(4)]
    return pl.pallas_call(
        _merge_kernel,
        grid=(d // tn, n // tm),
        in_specs=y_specs + r_specs + [pl.BlockSpec((4, w, tn), lambda j, i: (0, 0, j))],
        out_specs=pl.BlockSpec((tm, tn), lambda j, i: (i, j)),
        out_shape=jax.ShapeDtypeStruct((n, d), BF16),
        compiler_params=_cparams(("parallel", "parallel")),
        name="merge",
    )(*ys, gates, gates, gates, gates, wb)


def _ceil_div_unroll(n):
    return (n + (ATT_UNROLL - 1)) >> (ATT_UNROLL.bit_length() - 1)


def _two_pass_attention(q_list, k_ref, vt_ref, sc_ref, acc_ref, bias_fn, trips, blk, m0_list, l0_list):
    heads = range(len(q_list))
    hd = HEAD_DIM
    span = ATT_UNROLL * blk

    def pass1(j, ms):
        base = pl.multiple_of(j * span, span)
        out = []
        for h in heads:
            s = _dot_nt(k_ref[pl.ds(base, span), h * hd:(h + 1) * hd], q_list[h])
            m = ms[h]
            for u in range(ATT_UNROLL):
                su = s[u * blk:(u + 1) * blk, :] + bias_fn(h, j * ATT_UNROLL + u)
                sc_ref[h, pl.ds(pl.multiple_of(base + u * blk, blk), blk), :] = su
                m = jnp.maximum(m, jnp.max(su, axis=0, keepdims=True))
            out.append(m)
        return tuple(out)

    ms = lax.fori_loop(0, trips, pass1, tuple(m0_list))

    ls = []
    for h in heads:
        alpha = jnp.exp2(m0_list[h] - ms[h])
        ls.append(alpha * l0_list[h])
        acc_ref[h] = alpha * acc_ref[h]

    def pass2(j, ls):
        base = pl.multiple_of(j * span, span)
        out = []
        for h in heads:
            p = jnp.exp2(sc_ref[h, pl.ds(base, span), :] - ms[h])
            out.append(ls[h] + jnp.sum(p, axis=0, keepdims=True))
            acc_ref[h] += _dot(vt_ref[h * hd:(h + 1) * hd, pl.ds(base, span)], p.astype(BF16))
        return tuple(out)

    return lax.fori_loop(0, trips, pass2, tuple(ls))


def _moba_kernel(q_ref, k_ref, vt_ref, g_ref, o_ref, kmean_ref, selb_ref, sc_ref, acc_ref, *, nblk):
    i = pl.program_id(2)
    blk = MOBA_BLOCK
    hd = HEAD_DIM
    heads = range(HEADS_PER_STEP)

    @pl.when(i == 0)
    def _():
        for h in heads:
            kf = k_ref[:, h * hd:(h + 1) * hd].astype(F32).reshape(nblk, blk, hd)
            kmean_ref[h] = jnp.mean(kf, axis=1)

    off = pl.multiple_of(i * blk, blk)
    q_list, m0_list, l0_list = [], [], []
    for h in heads:
        q = q_ref[:, h * hd:(h + 1) * hd]
        q_list.append(q)
        k1, k2, k3 = _split3(kmean_ref[h])
        bs = _dot_nt(k1, q) + _dot_nt(k2, q) + _dot_nt(k3, q)
        n_iota = lax.broadcasted_iota(jnp.int32, bs.shape, 0)
        past = n_iota < i
        left = jnp.where(past, bs, -jnp.inf)
        bias = jnp.full(bs.shape, _NEG, F32)
        for _ in range(MOBA_TOPK):
            top = jnp.max(left, axis=0, keepdims=True)
            first = jnp.min(jnp.where(left == top, n_iota, nblk), axis=0, keepdims=True)
            hit = n_iota == first
            bias = jnp.where(hit & past, _ZERO, bias)
            left = jnp.where(hit, -jnp.inf, left)
        selb_ref[h] = bias

        st = _dot_nt(k_ref[pl.ds(off, blk), h * hd:(h + 1) * hd], q)
        kpos = lax.broadcasted_iota(jnp.int32, st.shape, 0)
        qpos = lax.broadcasted_iota(jnp.int32, st.shape, 1)
        st = jnp.where(kpos <= qpos, st, _NEG)
        m0 = jnp.max(st, axis=0, keepdims=True)
        p0 = jnp.exp2(st - m0)
        m0_list.append(m0)
        l0_list.append(jnp.sum(p0, axis=0, keepdims=True))
        acc_ref[h] = _dot(vt_ref[h * hd:(h + 1) * hd, pl.ds(off, blk)], p0.astype(BF16))

    ls = _two_pass_attention(
        q_list, k_ref, vt_ref, sc_ref, acc_ref, lambda h, n: selb_ref[h, pl.ds(n, 1), :],
        _ceil_div_unroll(i), blk, m0_list, l0_list)
    for h in heads:
        o = (acc_ref[h] / ls[h]).T
        g = g_ref[:, h * hd:(h + 1) * hd].astype(F32)
        o_ref[:, h * hd:(h + 1) * hd] = (o * _silu(g)).astype(o_ref.dtype)


def _moba(act, act_t, batch, seq, q_col, k_col, g_col, v_row):
    nblk = seq // MOBA_BLOCK
    n = act.shape[0]
    blk = MOBA_BLOCK
    hp = HEADS_PER_STEP
    w = hp * HEAD_DIM
    q_cb, k_cb, g_cb, v_rb = q_col // w, k_col // w, g_col // w, v_row // w
    return pl.pallas_call(
        functools.partial(_moba_kernel, nblk=nblk),
        grid=(batch, MOBA_HEADS // hp, nblk),
        in_specs=[pl.BlockSpec((blk, w), lambda b, h, i: (b * nblk + i, q_cb + h)),
                  pl.BlockSpec((seq, w), lambda b, h, i: (b, k_cb + h)),
                  pl.BlockSpec((w, seq), lambda b, h, i: (v_rb + h, b)),
                  pl.BlockSpec((blk, w), lambda b, h, i: (b * nblk + i, g_cb + h))],
        out_specs=pl.BlockSpec((blk, w), lambda b, h, i: (b * nblk + i, h)),
        out_shape=jax.ShapeDtypeStruct((n, BRANCH_W), BF16),
        scratch_shapes=[pltpu.VMEM((hp, nblk, HEAD_DIM), F32),
                        pltpu.VMEM((hp, nblk, blk), F32),
                        pltpu.VMEM((hp, seq, blk), F32),
                        pltpu.VMEM((hp, HEAD_DIM, blk), F32)],
        compiler_params=_cparams(("parallel", "parallel", "arbitrary")),
        name="moba",
    )(act, act, act_t, act)


def _conv_kernel(cb_ref, cc_ref, ch_ref, cg_ref, w_ref, bias_ref, o_ref):
    u = cc_ref[...].astype(F32) * ch_ref[...].astype(F32)
    t = lax.broadcasted_iota(jnp.int32, u.shape, 0)
    u1 = jnp.where(t >= 1, pltpu.roll(u, 1, 0), 0.0)
    u2 = jnp.where(t >= 2, pltpu.roll(u, 2, 0), 0.0)
    w = w_ref[...]
    conv = u2 * w[0:1, :] + u1 * w[1:2, :] + u * w[2:3, :] + bias_ref[...]
    o_ref[...] = (cb_ref[...].astype(F32) * conv * _silu(cg_ref[...].astype(F32))).astype(o_ref.dtype)


def _conv(act, conv_w, conv_b, batch, seq, cb_cb, cc_cb, ch_cb, cg_cb):
    n = act.shape[0]
    ncb = BRANCH_W // LANE
    kw = conv_w.shape[0]
    wpad = jnp.zeros((8, BRANCH_W), F32).at[:kw].set(conv_w)

    def spec(base):
        return pl.BlockSpec((seq, LANE), lambda b, c: (b, base + c))

    return pl.pallas_call(
        _conv_kernel,
        grid=(batch, ncb),
        in_specs=[spec(cb_cb), spec(cc_cb), spec(ch_cb), spec(cg_cb),
                  pl.BlockSpec((8, LANE), lambda b, c: (0, c)),
                  pl.BlockSpec((1, LANE), lambda b, c: (0, c))],
        out_specs=pl.BlockSpec((seq, LANE), lambda b, c: (b, c)),
        out_shape=jax.ShapeDtypeStruct((n, BRANCH_W), BF16),
        compiler_params=_cparams(("parallel", "parallel")),
        name="conv",
    )(act, act, act, act, wpad, conv_b.reshape(1, BRANCH_W))


def _dsa_kernel(qc_ref, kc_ref, wt_ref, q_ref, k_ref, vt_ref, g_ref, o_ref, st_ref, tri_ref, sc_ref,
                acc_ref, *, topk, tq):
    i = pl.program_id(1)
    nch = i + 1
    npair = (nch + 1) >> 1
    trips = _ceil_div_unroll(nch)
    idx_scale = (IDX_DIM ** -0.5) * (IDX_HEADS ** -0.5)
    kf = float(topk)
    ninf = np.float32(-np.inf)

    def chunk(c):
        return pl.ds(pl.multiple_of(c * tq, tq), tq)

    def pair(c):
        return pl.ds(pl.multiple_of(c * (2 * tq), 2 * tq), 2 * tq)

    def fold8(x, op):
        return op(x.reshape(x.shape[0] // 8, 8, tq), axis=0)

    def count(pred):
        def body(c, acc):
            for u in range(2):
                acc = acc + fold8(_ind(pred(st_ref[chunk(2 * c + u), :])), jnp.sum)
            return acc
        acc = lax.fori_loop(0, npair, body, jnp.zeros((8, tq), F32))
        return jnp.sum(acc, axis=0, keepdims=True)

    @pl.when(pl.program_id(2) == 0)
    def _select():
        def score_body(c, carry):
            mx, mn = carry
            for u in range(2):
                cc = 2 * c + u
                kc = kc_ref[chunk(cc), :]
                sc = jnp.zeros((tq, tq), F32)
                for j in range(IDX_HEADS):
                    lg = _dot_nt(kc, qc_ref[:, j * IDX_SLOT:(j + 1) * IDX_SLOT])
                    sc = sc + (wt_ref[j:j + 1, :] * idx_scale) * jnp.maximum(lg, 0.0)
                kpos = cc * tq + lax.broadcasted_iota(jnp.int32, sc.shape, 0)
                qpos = i * tq + lax.broadcasted_iota(jnp.int32, sc.shape, 1)
                causal = kpos <= qpos
                st_ref[chunk(cc), :] = jnp.where(causal, sc, ninf)
                mx = jnp.maximum(mx, fold8(jnp.where(causal, sc, ninf), jnp.max))
                mn = jnp.minimum(mn, fold8(jnp.where(causal, sc, -ninf), jnp.min))
            return mx, mn
        mx8, mn8 = lax.fori_loop(0, npair, score_body,
                                 (jnp.full((8, tq), ninf, F32), jnp.full((8, tq), -ninf, F32)))
        smax = jnp.max(mx8, axis=0, keepdims=True)
        smin = jnp.min(mn8, axis=0, keepdims=True)

        qrow = i * tq + lax.broadcasted_iota(jnp.int32, (1, tq), 1)
        short = qrow + 1 < topk
        lo0 = jnp.where(short, _ZERO, smin)
        hi0 = jnp.where(short, _ONE, smax + jnp.maximum(jnp.abs(smax), np.float32(1e-30))
                        * np.float32(2.0 ** -10))

        def bis_body(_, carry):
            lo, hi = carry
            mid = lo + (hi - lo) * np.float32(0.5)
            ge = count(lambda s: s >= mid) >= kf
            return jnp.where(ge, mid, lo), jnp.where(ge, hi, mid)

        _, hi_f0 = lax.fori_loop(0, BISECT_ITERS, bis_body, (lo0, hi0))

        def walk_cond(carry):
            return jnp.min(carry[3]) < 0.5

        def walk_body(carry):
            hi_f, thr, cnt, done = carry

            def mx_body(c, acc):
                s = st_ref[pair(c), :]
                return jnp.maximum(acc, fold8(jnp.where(s < hi_f, s, ninf), jnp.max))
            mx = lax.fori_loop(0, npair, mx_body, jnp.full((8, tq), ninf, F32))
            v = jnp.max(mx, axis=0, keepdims=True)
            c_v = count(lambda s: s >= v)
            is_done = done > 0.5
            return (jnp.where(is_done, hi_f, v), jnp.where(is_done, thr, v),
                    jnp.where(is_done, cnt, c_v), _ind(is_done | (c_v >= kf)))

        zero_row = jnp.zeros((1, tq), F32)
        all_rows = (2 * npair * tq).astype(F32)
        _, thr, n_ge, _ = lax.while_loop(
            walk_cond, walk_body,
            (hi_f0, jnp.full((1, tq), ninf, F32), zero_row + all_rows, _ind(short)))

        n_gt = count(lambda s: s > thr)
        need = kf - n_gt
        n_eq = n_ge - n_gt

        def bias_all_ties(c, _):
            st_ref[pair(c), :] = jnp.where(st_ref[pair(c), :] >= thr, _ZERO, _NEG)
            return 0

        def bias_some_ties(c, carry):
            for u in range(2):
                cc = 2 * c + u
                s = st_ref[chunk(cc), :]
                eq = s == thr
                eqf = _ind(eq)
                before = _dot(tri_ref[...], eqf.astype(BF16)) + carry
                kpos = cc * tq + lax.broadcasted_iota(jnp.int32, s.shape, 0)
                qpos = i * tq + lax.broadcasted_iota(jnp.int32, s.shape, 1)
                selected = ((s > thr) | (eq & (before < need))) & (kpos <= qpos)
                st_ref[chunk(cc), :] = jnp.where(selected, _ZERO, _NEG)
                carry = carry + jnp.sum(eqf, axis=0, keepdims=True)
            return carry

        def all_ties():
            lax.fori_loop(0, npair, bias_all_ties, 0)

        def some_ties():
            r = lax.broadcasted_iota(jnp.int32, (tq, tq), 0)
            cidx = lax.broadcasted_iota(jnp.int32, (tq, tq), 1)
            tri_ref[...] = _ind(cidx < r).astype(BF16)
            lax.fori_loop(0, npair, bias_some_ties, zero_row)

        simple = jnp.logical_and(jnp.max(n_eq - need) <= 0.0, jnp.min(thr) > ninf)
        lax.cond(simple, all_ties, some_ties)

        def pad_body(c, _):
            st_ref[chunk(c), :] = jnp.full((tq, tq), _NEG, F32)
            return 0
        lax.fori_loop(2 * npair, trips * ATT_UNROLL, pad_body, 0)

    hd = HEAD_DIM
    heads = range(HEADS_PER_STEP)
    q_list = [q_ref[:, h * hd:(h + 1) * hd] for h in heads]
    for h in heads:
        acc_ref[h] = jnp.zeros((hd, tq), F32)
    ls = _two_pass_attention(
        q_list, k_ref, vt_ref, sc_ref, acc_ref, lambda h, c: st_ref[chunk(c), :], trips, tq,
        [jnp.full((1, tq), _NEG, F32)] * HEADS_PER_STEP, [jnp.zeros((1, tq), F32)] * HEADS_PER_STEP)
    for h in heads:
        o = (acc_ref[h] / ls[h]).T
        g = g_ref[:, h * hd:(h + 1) * hd].astype(F32)
        o_ref[:, h * hd:(h + 1) * hd] = (o * _silu(g)).astype(o_ref.dtype)


def _dsa(act, act_t, cat, wt, batch, seq, q_col, k_col, g_col, v_row, tq=256):
    n = act.shape[0]
    nq = seq // tq
    topk = min(DSA_TOPK_MAX, seq // 4)
    n_qcat = IDX_HEADS * IDX_SLOT
    hp = HEADS_PER_STEP
    w = hp * HEAD_DIM
    q_cb, k_cb, g_cb, v_rb = q_col // w, k_col // w, g_col // w, v_row // w
    return pl.pallas_call(
        functools.partial(_dsa_kernel, topk=topk, tq=tq),
        grid=(batch, nq, DSA_HEADS // hp),
        in_specs=[pl.BlockSpec((tq, n_qcat), lambda b, i, h: (b * nq + i, 0)),
                  pl.BlockSpec((seq, IDX_SLOT), lambda b, i, h: (b, n_qcat // IDX_SLOT)),
                  pl.BlockSpec((8, tq), lambda b, i, h: (0, b * nq + i)),
                  pl.BlockSpec((tq, w), lambda b, i, h: (b * nq + i, q_cb + h)),
                  pl.BlockSpec((seq, w), lambda b, i, h: (b, k_cb + h)),
                  pl.BlockSpec((w, seq), lambda b, i, h: (v_rb + h, b)),
                  pl.BlockSpec((tq, w), lambda b, i, h: (b * nq + i, g_cb + h))],
        out_specs=pl.BlockSpec((tq, w), lambda b, i, h: (b * nq + i, h)),
        out_shape=jax.ShapeDtypeStruct((n, BRANCH_W), BF16),
        scratch_shapes=[pltpu.VMEM((seq, tq), F32),
                        pltpu.VMEM((tq, tq), BF16),
                        pltpu.VMEM((hp, seq, tq), F32),
                        pltpu.VMEM((hp, HEAD_DIM, tq), F32)],
        compiler_params=_cparams(("parallel", "arbitrary", "arbitrary")),
        name="dsa",
    )(cat, cat, wt, act, act, act_t, act)


def _mem_kernel(q_ref, mk_ref, mvt_ref, g_ref, o_ref):
    hd = MEM_HEAD_DIM
    for h in range(MEM_HEADS):
        cols = slice(h * hd, (h + 1) * hd)
        s = _dot_nt(mk_ref[:, cols], q_ref[:, cols])
        m = jnp.max(s, axis=0, keepdims=True)
        p = jnp.exp2(s - m)
        l = jnp.sum(p, axis=0, keepdims=True)
        o = (_dot(mvt_ref[cols, :], p.astype(BF16)) / l).T
        o_ref[:, cols] = (o * _silu(g_ref[:, cols].astype(F32))).astype(o_ref.dtype)


def _mem_attn(act, mk, mvt, batch, seq, mem_len, q_col, g_col, tq=512):
    n = act.shape[0]
    nq = seq // tq
    w = BRANCH_W
    return pl.pallas_call(
        _mem_kernel,
        grid=(batch, nq),
        in_specs=[pl.BlockSpec((tq, w), lambda b, i: (b * nq + i, q_col // w)),
                  pl.BlockSpec((mem_len, w), lambda b, i: (b, 0)),
                  pl.BlockSpec((w, mem_len), lambda b, i: (0, b)),
                  pl.BlockSpec((tq, w), lambda b, i: (b * nq + i, g_col // w))],
        out_specs=pl.BlockSpec((tq, w), lambda b, i: (b * nq + i, 0)),
        out_shape=jax.ShapeDtypeStruct((n, BRANCH_W), BF16),
        compiler_params=_cparams(("parallel", "parallel")),
        name="mem_attn",
    )(act, mk, mvt, act)


O_IDX = 14 * BRANCH_W
O_MERGE = O_IDX + IDX_HEADS * IDX_DIM + IDX_DIM + IDX_HEADS


def _input_weights(w_in3):
    wt3 = jnp.swapaxes(w_in3, 1, 2).astype(BF16)
    idx3 = jnp.swapaxes(w_in3[:, :, O_IDX:O_MERGE], 1, 2)
    idx3 = jnp.pad(idx3, ((0, 0), (0, IDX_W_IN - idx3.shape[1]), (0, 0)))
    return wt3, idx3


def _layer_weights(layer, wt3, idx3, w_mem_kv3, w_branch3, w_out3):
    bw = BRANCH_W
    w_idx_hi, w_idx_lo = _split2(idx3[layer])
    return dict(w_merge=wt3[layer, O_MERGE:], w_idx_hi=w_idx_hi, w_idx_lo=w_idx_lo,
                w_mk=w_mem_kv3[layer][:, :bw].astype(BF16),
                w_mvt=_transpose_cast(w_mem_kv3[layer][:, bw:], "w_mvt"),
                w_branch=w_branch3[layer].astype(BF16), w_out=w_out3[layer].astype(BF16))


def _layer(layer, x2, mem2, batch, seq, mem_len, ln_g, conv_w, conv_b, mem_ln_g, wt3, w, final_g):
    nb = BRANCH_W // LANE
    bw = BRANCH_W
    (A_Q, A_K, A_G, C_B, C_C, C_H, C_G, S_Q, S_K, S_G, M_Q, M_G) = range(12)
    tok_groups = (0, 1, 3, 4, 5, 6, 7, 8, 9, 11, 12, 13)
    val_groups = (2, 10)

    log2e = float(np.log2(np.e))
    colscale = np.ones((1, len(tok_groups) * bw), np.float32)
    for col, hdim in ((A_Q, HEAD_DIM), (S_Q, HEAD_DIM), (M_Q, MEM_HEAD_DIM)):
        colscale[:, col * bw:(col + 1) * bw] = hdim ** -0.5 * log2e

    xn, cat, wt = _prep(x2, ln_g, w["w_idx_hi"], w["w_idx_lo"])
    act = _proj_tok(xn, wt3, layer, tok_groups, jnp.asarray(colscale), tm=1024, tn=bw)
    act_r = _mm_nt(xn, w["w_merge"], BF16, tm=1024, tn=1024, name="proj_merge")
    act_t = _proj_vt(wt3, layer, val_groups, xn, tm=bw, tn=1024)

    y_a = _moba(act, act_t, batch, seq, A_Q * bw, A_K * bw, A_G * bw, 0)
    y_c = _conv(act, conv_w, conv_b, batch, seq, C_B * nb, C_C * nb, C_H * nb, C_G * nb)
    y_s = _dsa(act, act_t, cat, wt, batch, seq, S_Q * bw, S_K * bw, S_G * bw, bw)

    mem_n = _rmsnorm(mem2, mem_ln_g, BF16)
    mk = _mm_nn(mem_n, w["w_mk"], BF16, tm=mem2.shape[0], tn=512, name="mem_k")
    mvt = _mm_nt(w["w_mvt"], mem_n, BF16, tm=512, tn=mem2.shape[0], name="mem_vt")
    y_m = _mem_attn(act, mk, mvt, batch, seq, mem_len, M_Q * bw, M_G * bw)

    merged = _merge((y_a, y_c, y_s, y_m), act_r, w["w_branch"], tm=512, tn=1024)
    return _out_proj(merged, w["w_out"], x2, final_g, tm=512)


def kernel(x, mem, ln_g, w_in, conv_w, conv_b, mem_ln_g, w_mem_kv, w_branch, w_out, final_g):
    batch, seq, d = x.shape
    mem_len = mem.shape[1]
    x2 = x.reshape(batch * seq, d)
    mem2 = mem.reshape(batch * mem_len, d)
    wt3, idx3 = _input_weights(w_in)
    depth = ln_g.shape[0]
    for layer in range(depth):
        w = _layer_weights(layer, wt3, idx3, w_mem_kv, w_branch, w_out)
        x2 = _layer(layer, x2, mem2, batch, seq, mem_len, ln_g[layer], conv_w[layer],
                    conv_b[layer], mem_ln_g[layer], wt3, w,
                    final_g if layer == depth - 1 else None)
    return x2.reshape(batch, seq, d)
```

```python
import functools

import numpy as np
import jax
import jax.numpy as jnp
from jax import lax
from jax.experimental import pallas as pl
from jax.experimental.pallas import tpu as pltpu

EPS = 1e-6
BRANCH_W = 1024
MOBA_HEADS = 8
MOBA_BLOCK = 256
MOBA_TOPK = 3
DSA_HEADS = 8
DSA_TOPK_MAX = 256
IDX_HEADS = 4
IDX_DIM = 64
MEM_HEADS = 4
HEAD_DIM = 128
MEM_HEAD_DIM = 256
IDX_SLOT = 256
NEG = -1e30
ATT_UNROLL = 4
HEADS_PER_STEP = 4
BISECT_ITERS = 22
LANE = 128
VMEM_LIMIT = 56 * 1024 * 1024

F32 = jnp.float32
BF16 = jnp.bfloat16
_ONE = np.float32(1.0)
_ZERO = np.float32(0.0)
_NEG = np.float32(NEG)


def _cparams(sem):
    return pltpu.CompilerParams(dimension_semantics=sem, vmem_limit_bytes=VMEM_LIMIT)


def _dot(a, b):
    return jnp.dot(a, b, preferred_element_type=F32)


def _dot_nt(a, b):
    return lax.dot_general(a, b, (((1,), (1,)), ((), ())), preferred_element_type=F32)


def _split2(v):
    hi = v.astype(BF16)
    lo = (v - hi.astype(F32)).astype(BF16)
    return hi, lo


def _split3(v):
    h1 = v.astype(BF16)
    r1 = v - h1.astype(F32)
    h2 = r1.astype(BF16)
    h3 = (r1 - h2.astype(F32)).astype(BF16)
    return h1, h2, h3


def _ind(cond):
    return jnp.where(cond, _ONE, _ZERO)


def _silu(g):
    return g * jax.nn.sigmoid(g)


def _rms(x, g):
    var = jnp.mean(x * x, axis=-1, keepdims=True)
    return (x * lax.rsqrt(var + EPS)) * g


def _rmsnorm_kernel(x_ref, g_ref, o_ref):
    o_ref[...] = _rms(x_ref[...], g_ref[...]).astype(o_ref.dtype)


def _rmsnorm(x, g, out_dtype, tm=256):
    n, d = x.shape
    return pl.pallas_call(
        _rmsnorm_kernel,
        grid=(n // tm,),
        in_specs=[pl.BlockSpec((tm, d), lambda i: (i, 0)),
                  pl.BlockSpec((1, d), lambda i: (0, 0))],
        out_specs=pl.BlockSpec((tm, d), lambda i: (i, 0)),
        out_shape=jax.ShapeDtypeStruct((n, d), out_dtype),
        compiler_params=_cparams(("parallel",)),
        name="rmsnorm",
    )(x, g.reshape(1, d))


def _prep_kernel(x_ref, g_ref, whi_ref, wlo_ref, phi_ref, plo_ref, xn_ref, cat_ref, wt_ref):
    xn = _rms(x_ref[...], g_ref[...])
    hi, lo = _split2(xn)
    xn_ref[...] = hi
    acc = _dot_nt(hi, whi_ref[...]) + _dot_nt(hi, wlo_ref[...]) + _dot_nt(lo, whi_ref[...])
    vh, vl = _split2(acc)
    cat_ref[...] = (_dot(vh, phi_ref[...]) + _dot(vl, plo_ref[...])).astype(BF16)
    gate_row = IDX_DIM
    wt_ref[...] = acc[:, IDX_W_IN - LANE:].T[gate_row:gate_row + 8, :]


IDX_W_IN = 384


def _idx_placement():
    n_out = (IDX_HEADS + 1) * IDX_SLOT
    p_hi = np.zeros((IDX_W_IN, n_out), np.float32)
    p_lo = np.zeros((IDX_W_IN, n_out), np.float32)
    e = np.arange(IDX_DIM)
    for j in range(IDX_HEADS):
        src, dst = j * IDX_DIM + e, j * IDX_SLOT + e
        p_hi[src, dst] = 1
        p_hi[src, dst + IDX_DIM] = 1
        p_lo[src, dst + 2 * IDX_DIM] = 1
    src, dst = IDX_HEADS * IDX_DIM + e, IDX_HEADS * IDX_SLOT + e
    p_hi[src, dst] = 1
    p_lo[src, dst + IDX_DIM] = 1
    p_hi[src, dst + 2 * IDX_DIM] = 1
    return jnp.asarray(p_hi, BF16), jnp.asarray(p_lo, BF16)


def _prep(x, g, whi, wlo, tm=512):
    n, d = x.shape
    p_hi, p_lo = _idx_placement()
    co = p_hi.shape[1]
    return pl.pallas_call(
        _prep_kernel,
        grid=(n // tm,),
        in_specs=[pl.BlockSpec((tm, d), lambda i: (i, 0)),
                  pl.BlockSpec((1, d), lambda i: (0, 0)),
                  pl.BlockSpec((IDX_W_IN, d), lambda i: (0, 0)),
                  pl.BlockSpec((IDX_W_IN, d), lambda i: (0, 0)),
                  pl.BlockSpec((IDX_W_IN, co), lambda i: (0, 0)),
                  pl.BlockSpec((IDX_W_IN, co), lambda i: (0, 0))],
        out_specs=[pl.BlockSpec((tm, d), lambda i: (i, 0)),
                   pl.BlockSpec((tm, co), lambda i: (i, 0)),
                   pl.BlockSpec((8, tm), lambda i: (0, i))],
        out_shape=[jax.ShapeDtypeStruct((n, d), BF16),
                   jax.ShapeDtypeStruct((n, co), BF16),
                   jax.ShapeDtypeStruct((8, n), F32)],
        compiler_params=_cparams(("parallel",)),
        name="prep",
    )(x, g.reshape(1, d), whi, wlo, p_hi, p_lo)


def _transpose_cast_kernel(w_ref, o_ref):
    o_ref[...] = w_ref[...].T.astype(o_ref.dtype)


def _transpose_cast(w, name, t=512):
    d, n = w.shape
    return pl.pallas_call(
        _transpose_cast_kernel,
        grid=(n // t, d // t),
        in_specs=[pl.BlockSpec((t, t), lambda i, j: (j, i))],
        out_specs=pl.BlockSpec((t, t), lambda i, j: (i, j)),
        out_shape=jax.ShapeDtypeStruct((n, d), BF16),
        compiler_params=_cparams(("parallel", "parallel")),
        name=name,
    )(w)


def _mm_nn_kernel(a_ref, b_ref, o_ref):
    o_ref[...] = _dot(a_ref[...], b_ref[...]).astype(o_ref.dtype)


def _mm_nn(a, b, out_dtype, tm, tn, name):
    m, k = a.shape
    n = b.shape[1]
    return pl.pallas_call(
        _mm_nn_kernel,
        grid=(m // tm, n // tn),
        in_specs=[pl.BlockSpec((tm, k), lambda i, j: (i, 0)),
                  pl.BlockSpec((k, tn), lambda i, j: (0, j))],
        out_specs=pl.BlockSpec((tm, tn), lambda i, j: (i, j)),
        out_shape=jax.ShapeDtypeStruct((m, n), out_dtype),
        compiler_params=_cparams(("parallel", "parallel")),
        name=name,
    )(a, b)


def _proj_tok_kernel(blk_ref, a_ref, b_ref, s_ref, o_ref):
    del blk_ref
    o_ref[...] = (_dot_nt(a_ref[...], b_ref[...]) * s_ref[...]).astype(o_ref.dtype)


def _proj_tok(a, wt3, layer, row_blocks, colscale, tm, tn):
    m, k = a.shape
    n = len(row_blocks) * tn
    return pl.pallas_call(
        _proj_tok_kernel,
        grid_spec=pltpu.PrefetchScalarGridSpec(
            num_scalar_prefetch=1,
            grid=(m // tm, n // tn),
            in_specs=[pl.BlockSpec((tm, k), lambda i, j, blk: (i, 0)),
                      pl.BlockSpec((None, tn, k), lambda i, j, blk: (layer, blk[j], 0)),
                      pl.BlockSpec((1, tn), lambda i, j, blk: (0, j))],
            out_specs=pl.BlockSpec((tm, tn), lambda i, j, blk: (i, j))),
        out_shape=jax.ShapeDtypeStruct((m, n), BF16),
        compiler_params=_cparams(("parallel", "parallel")),
        name="proj_tok",
    )(jnp.asarray(row_blocks, jnp.int32), a, wt3, colscale)


def _proj_vt_kernel(blk_ref, a_ref, b_ref, o_ref):
    del blk_ref
    o_ref[...] = _dot_nt(a_ref[...], b_ref[...]).astype(o_ref.dtype)


def _proj_vt(wt3, layer, row_blocks, b, tm, tn):
    n, k = b.shape
    m = len(row_blocks) * tm
    return pl.pallas_call(
        _proj_vt_kernel,
        grid_spec=pltpu.PrefetchScalarGridSpec(
            num_scalar_prefetch=1,
            grid=(n // tn, m // tm),
            in_specs=[pl.BlockSpec((None, tm, k), lambda j, i, blk: (layer, blk[i], 0)),
                      pl.BlockSpec((tn, k), lambda j, i, blk: (j, 0))],
            out_specs=pl.BlockSpec((tm, tn), lambda j, i, blk: (i, j))),
        out_shape=jax.ShapeDtypeStruct((m, n), BF16),
        compiler_params=_cparams(("parallel", "parallel")),
        name="proj_vt",
    )(jnp.asarray(row_blocks, jnp.int32), wt3, b)


def _mm_nt_kernel(a_ref, b_ref, o_ref):
    o_ref[...] = _dot_nt(a_ref[...], b_ref[...]).astype(o_ref.dtype)


def _mm_nt(a, b, out_dtype, tm, tn, name):
    m, k = a.shape
    n = b.shape[0]
    return pl.pallas_call(
        _mm_nt_kernel,
        grid=(n // tn, m // tm),
        in_specs=[pl.BlockSpec((tm, k), lambda j, i: (i, 0)),
                  pl.BlockSpec((tn, k), lambda j, i: (j, 0))],
        out_specs=pl.BlockSpec((tm, tn), lambda j, i: (i, j)),
        out_shape=jax.ShapeDtypeStruct((m, n), out_dtype),
        compiler_params=_cparams(("parallel", "parallel")),
        name=name,
    )(a, b)


def _out_proj_kernel(a_ref, b_ref, r_ref, g_ref, o_ref, *, final_norm):
    y = r_ref[...] + _dot(a_ref[...], b_ref[...])
    o_ref[...] = _rms(y, g_ref[...]) if final_norm else y


def _out_proj(a, b, res, final_g, tm):
    m, k = a.shape
    n = b.shape[1]
    g = jnp.ones((1, n), F32) if final_g is None else final_g.reshape(1, n)
    return pl.pallas_call(
        functools.partial(_out_proj_kernel, final_norm=final_g is not None),
        grid=(m // tm,),
        in_specs=[pl.BlockSpec((tm, k), lambda i: (i, 0)),
                  pl.BlockSpec((k, n), lambda i: (0, 0)),
                  pl.BlockSpec((tm, n), lambda i: (i, 0)),
                  pl.BlockSpec((1, n), lambda i: (0, 0))],
        out_specs=pl.BlockSpec((tm, n), lambda i: (i, 0)),
        out_shape=jax.ShapeDtypeStruct((m, n), F32),
        compiler_params=_cparams(("parallel",)),
        name="out_proj",
    )(a, b, res, g)


def _merge_kernel(ya_ref, yc_ref, ys_ref, ym_ref, ra_ref, rc_ref, rs_ref, rm_ref, wb_ref, o_ref):
    ys = (ya_ref, yc_ref, ys_ref, ym_ref)
    rs = (ra_ref, rc_ref, rs_ref, rm_ref)
    acc = None
    for br in range(4):
        z = _dot(ys[br][...], wb_ref[br])
        term = jax.nn.sigmoid(rs[br][...].astype(F32)) * z
        acc = term if acc is None else acc + term
    o_ref[...] = acc.astype(o_ref.dtype)


def _merge(ys, gates, wb, tm, tn):
    n, w = ys[0].shape
    d = wb.shape[2]
    y_specs = [pl.BlockSpec((tm, w), lambda j, i: (i, 0)) for _ in range(4)]
    r_specs = [pl.BlockSpec((tm, tn), functools.partial(
        lambda j, i, base: (i, base + j), base=br * d // tn)) for br in range(4)]
    return pl.pallas_call(
        _merge_kernel,
        grid=(d // tn, n // tm),
        in_specs=y_specs + r_specs + [pl.BlockSpec((4, w, tn), lambda j, i: (0, 0, j))],
        out_specs=pl.BlockSpec((tm, tn), lambda j, i: (i, j)),
        out_shape=jax.ShapeDtypeStruct((n, d), BF16),
        compiler_params=_cparams(("parallel", "parallel")),
        name="merge",
    )(*ys, gates, gates, gates, gates, wb)


def _ceil_div_unroll(n):
    return (n + (ATT_UNROLL - 1)) >> (ATT_UNROLL.bit_length() - 1)


def _two_pass_attention(q_list, k_ref, vt_ref, sc_ref, acc_ref, bias_fn, trips, blk, m0_list, l0_list):
    heads = range(len(q_list))
    hd = HEAD_DIM
    span = ATT_UNROLL * blk
    max_trips = k_ref.shape[0] // span

    def pass1(j, ms):
        out = []
        for h in heads:
            s = _dot_nt(k_ref[j * span:(j + 1) * span, h * hd:(h + 1) * hd], q_list[h])
            m = ms[h]
            for u in range(ATT_UNROLL):
                n = j * ATT_UNROLL + u
                su = s[u * blk:(u + 1) * blk, :] + bias_fn(h, n)
                sc_ref[h, n * blk:(n + 1) * blk, :] = su
                m = jnp.maximum(m, jnp.max(su, axis=0, keepdims=True))
            out.append(m)
        return out

    def pass2(j, ms, ls):
        out = []
        for h in heads:
            p = jnp.exp2(sc_ref[h, j * span:(j + 1) * span, :] - ms[h])
            out.append(ls[h] + jnp.sum(p, axis=0, keepdims=True))
            acc_ref[h] += _dot(vt_ref[h * hd:(h + 1) * hd, j * span:(j + 1) * span],
                               p.astype(BF16))
        return out

    def run(n_trips):
        ms = list(m0_list)
        for j in range(n_trips):
            ms = pass1(j, ms)
        ls = []
        for h in heads:
            alpha = jnp.exp2(m0_list[h] - ms[h])
            ls.append(alpha * l0_list[h])
            acc_ref[h] = alpha * acc_ref[h]
        for j in range(n_trips):
            ls = pass2(j, ms, ls)
        return tuple(ls)

    return lax.switch(trips, [functools.partial(run, t) for t in range(max_trips + 1)])


def _moba_kernel(q_ref, k_ref, vt_ref, g_ref, o_ref, kmean_ref, selb_ref, sc_ref, acc_ref, *, nblk):
    i = pl.program_id(2)
    blk = MOBA_BLOCK
    hd = HEAD_DIM
    heads = range(HEADS_PER_STEP)

    @pl.when(i == 0)
    def _():
        for h in heads:
            kf = k_ref[:, h * hd:(h + 1) * hd].astype(F32).reshape(nblk, blk, hd)
            kmean_ref[h] = jnp.mean(kf, axis=1)

    off = pl.multiple_of(i * blk, blk)
    q_list, m0_list, l0_list = [], [], []
    for h in heads:
        q = q_ref[:, h * hd:(h + 1) * hd]
        q_list.append(q)
        k1, k2, k3 = _split3(kmean_ref[h])
        bs = _dot_nt(k1, q) + _dot_nt(k2, q) + _dot_nt(k3, q)
        n_iota = lax.broadcasted_iota(jnp.int32, bs.shape, 0)
        past = n_iota < i
        left = jnp.where(past, bs, -jnp.inf)
        bias = jnp.full(bs.shape, _NEG, F32)
        for _ in range(MOBA_TOPK):
            top = jnp.max(left, axis=0, keepdims=True)
            first = jnp.min(jnp.where(left == top, n_iota, nblk), axis=0, keepdims=True)
            hit = n_iota == first
            bias = jnp.where(hit & past, _ZERO, bias)
            left = jnp.where(hit, -jnp.inf, left)
        selb_ref[h] = bias

        st = _dot_nt(k_ref[pl.ds(off, blk), h * hd:(h + 1) * hd], q)
        kpos = lax.broadcasted_iota(jnp.int32, st.shape, 0)
        qpos = lax.broadcasted_iota(jnp.int32, st.shape, 1)
        st = jnp.where(kpos <= qpos, st, _NEG)
        m0 = jnp.max(st, axis=0, keepdims=True)
        p0 = jnp.exp2(st - m0)
        m0_list.append(m0)
        l0_list.append(jnp.sum(p0, axis=0, keepdims=True))
        acc_ref[h] = _dot(vt_ref[h * hd:(h + 1) * hd, pl.ds(off, blk)], p0.astype(BF16))

    ls = _two_pass_attention(
        q_list, k_ref, vt_ref, sc_ref, acc_ref, lambda h, n: selb_ref[h, pl.ds(n, 1), :],
        _ceil_div_unroll(i), blk, m0_list, l0_list)
    for h in heads:
        o = (acc_ref[h] / ls[h]).T
        g = g_ref[:, h * hd:(h + 1) * hd].astype(F32)
        o_ref[:, h * hd:(h + 1) * hd] = (o * _silu(g)).astype(o_ref.dtype)


def _moba(act, act_t, batch, seq, q_col, k_col, g_col, v_row):
    nblk = seq // MOBA_BLOCK
    n = act.shape[0]
    blk = MOBA_BLOCK
    hp = HEADS_PER_STEP
    w = hp * HEAD_DIM
    q_cb, k_cb, g_cb, v_rb = q_col // w, k_col // w, g_col // w, v_row // w
    return pl.pallas_call(
        functools.partial(_moba_kernel, nblk=nblk),
        grid=(batch, MOBA_HEADS // hp, nblk),
        in_specs=[pl.BlockSpec((blk, w), lambda b, h, i: (b * nblk + i, q_cb + h)),
                  pl.BlockSpec((seq, w), lambda b, h, i: (b, k_cb + h)),
                  pl.BlockSpec((w, seq), lambda b, h, i: (v_rb + h, b)),
                  pl.BlockSpec((blk, w), lambda b, h, i: (b * nblk + i, g_cb + h))],
        out_specs=pl.BlockSpec((blk, w), lambda b, h, i: (b * nblk + i, h)),
        out_shape=jax.ShapeDtypeStruct((n, BRANCH_W), BF16),
        scratch_shapes=[pltpu.VMEM((hp, nblk, HEAD_DIM), F32),
                        pltpu.VMEM((hp, nblk, blk), F32),
                        pltpu.VMEM((hp, seq, blk), F32),
                        pltpu.VMEM((hp, HEAD_DIM, blk), F32)],
        compiler_params=_cparams(("parallel", "parallel", "arbitrary")),
        name="moba",
    )(act, act, act_t, act)


def _conv_kernel(cb_ref, cc_ref, ch_ref, cg_ref, w_ref, bias_ref, o_ref):
    u = cc_ref[...].astype(F32) * ch_ref[...].astype(F32)
    t = lax.broadcasted_iota(jnp.int32, u.shape, 0)
    u1 = jnp.where(t >= 1, pltpu.roll(u, 1, 0), 0.0)
    u2 = jnp.where(t >= 2, pltpu.roll(u, 2, 0), 0.0)
    w = w_ref[...]
    conv = u2 * w[0:1, :] + u1 * w[1:2, :] + u * w[2:3, :] + bias_ref[...]
    o_ref[...] = (cb_ref[...].astype(F32) * conv * _silu(cg_ref[...].astype(F32))).astype(o_ref.dtype)


def _conv(act, conv_w, conv_b, batch, seq, cb_cb, cc_cb, ch_cb, cg_cb):
    n = act.shape[0]
    ncb = BRANCH_W // LANE
    kw = conv_w.shape[0]
    wpad = jnp.zeros((8, BRANCH_W), F32).at[:kw].set(conv_w)

    def spec(base):
        return pl.BlockSpec((seq, LANE), lambda b, c: (b, base + c))

    return pl.pallas_call(
        _conv_kernel,
        grid=(batch, ncb),
        in_specs=[spec(cb_cb), spec(cc_cb), spec(ch_cb), spec(cg_cb),
                  pl.BlockSpec((8, LANE), lambda b, c: (0, c)),
                  pl.BlockSpec((1, LANE), lambda b, c: (0, c))],
        out_specs=pl.BlockSpec((seq, LANE), lambda b, c: (b, c)),
        out_shape=jax.ShapeDtypeStruct((n, BRANCH_W), BF16),
        compiler_params=_cparams(("parallel", "parallel")),
        name="conv",
    )(act, act, act, act, wpad, conv_b.reshape(1, BRANCH_W))


def _dsa_kernel(qc_ref, kc_ref, wt_ref, q_ref, k_ref, vt_ref, g_ref, o_ref, st_ref, tri_ref, sc_ref,
                acc_ref, *, topk, tq):
    i = pl.program_id(1)
    nch = i + 1
    npair = (nch + 1) >> 1
    trips = _ceil_div_unroll(nch)
    idx_scale = (IDX_DIM ** -0.5) * (IDX_HEADS ** -0.5)
    kf = float(topk)
    ninf = np.float32(-np.inf)

    def chunk(c):
        return pl.ds(pl.multiple_of(c * tq, tq), tq)

    def pair(c):
        return pl.ds(pl.multiple_of(c * (2 * tq), 2 * tq), 2 * tq)

    def fold8(x, op):
        return op(x.reshape(x.shape[0] // 8, 8, tq), axis=0)

    def count(pred):
        def body(c, acc):
            for u in range(2):
                acc = acc + fold8(_ind(pred(st_ref[chunk(2 * c + u), :])), jnp.sum)
            return acc
        acc = lax.fori_loop(0, npair, body, jnp.zeros((8, tq), F32))
        return jnp.sum(acc, axis=0, keepdims=True)

    @pl.when(pl.program_id(2) == 0)
    def _select():
        def score_body(c, carry):
            mx, mn = carry
            for u in range(2):
                cc = 2 * c + u
                kc = kc_ref[chunk(cc), :]
                sc = jnp.zeros((tq, tq), F32)
                for j in range(IDX_HEADS):
                    lg = _dot_nt(kc, qc_ref[:, j * IDX_SLOT:(j + 1) * IDX_SLOT])
                    sc = sc + (wt_ref[j:j + 1, :] * idx_scale) * jnp.maximum(lg, 0.0)
                kpos = cc * tq + lax.broadcasted_iota(jnp.int32, sc.shape, 0)
                qpos = i * tq + lax.broadcasted_iota(jnp.int32, sc.shape, 1)
                causal = kpos <= qpos
                st_ref[chunk(cc), :] = jnp.where(causal, sc, ninf)
                mx = jnp.maximum(mx, fold8(jnp.where(causal, sc, ninf), jnp.max))
                mn = jnp.minimum(mn, fold8(jnp.where(causal, sc, -ninf), jnp.min))
            return mx, mn
        mx8, mn8 = lax.fori_loop(0, npair, score_body,
                                 (jnp.full((8, tq), ninf, F32), jnp.full((8, tq), -ninf, F32)))
        smax = jnp.max(mx8, axis=0, keepdims=True)
        smin = jnp.min(mn8, axis=0, keepdims=True)

        qrow = i * tq + lax.broadcasted_iota(jnp.int32, (1, tq), 1)
        short = qrow + 1 < topk
        lo0 = jnp.where(short, _ZERO, smin)
        hi0 = jnp.where(short, _ONE, smax + jnp.maximum(jnp.abs(smax), np.float32(1e-30))
                        * np.float32(2.0 ** -10))

        def bis_body(_, carry):
            lo, hi = carry
            mid = lo + (hi - lo) * np.float32(0.5)
            ge = count(lambda s: s >= mid) >= kf
            return jnp.where(ge, mid, lo), jnp.where(ge, hi, mid)

        _, hi_f0 = lax.fori_loop(0, BISECT_ITERS, bis_body, (lo0, hi0))

        def walk_cond(carry):
            return jnp.min(carry[3]) < 0.5

        def walk_body(carry):
            hi_f, thr, cnt, done = carry

            def mx_body(c, acc):
                s = st_ref[pair(c), :]
                return jnp.maximum(acc, fold8(jnp.where(s < hi_f, s, ninf), jnp.max))
            mx = lax.fori_loop(0, npair, mx_body, jnp.full((8, tq), ninf, F32))
            v = jnp.max(mx, axis=0, keepdims=True)
            c_v = count(lambda s: s >= v)
            is_done = done > 0.5
            return (jnp.where(is_done, hi_f, v), jnp.where(is_done, thr, v),
                    jnp.where(is_done, cnt, c_v), _ind(is_done | (c_v >= kf)))

        zero_row = jnp.zeros((1, tq), F32)
        all_rows = (2 * npair * tq).astype(F32)
        _, thr, n_ge, _ = lax.while_loop(
            walk_cond, walk_body,
            (hi_f0, jnp.full((1, tq), ninf, F32), zero_row + all_rows, _ind(short)))

        n_gt = count(lambda s: s > thr)
        need = kf - n_gt
        n_eq = n_ge - n_gt

        def bias_all_ties(c, _):
            st_ref[pair(c), :] = jnp.where(st_ref[pair(c), :] >= thr, _ZERO, _NEG)
            return 0

        def bias_some_ties(c, carry):
            for u in range(2):
                cc = 2 * c + u
                s = st_ref[chunk(cc), :]
                eq = s == thr
                eqf = _ind(eq)
                before = _dot(tri_ref[...], eqf.astype(BF16)) + carry
                kpos = cc * tq + lax.broadcasted_iota(jnp.int32, s.shape, 0)
                qpos = i * tq + lax.broadcasted_iota(jnp.int32, s.shape, 1)
                selected = ((s > thr) | (eq & (before < need))) & (kpos <= qpos)
                st_ref[chunk(cc), :] = jnp.where(selected, _ZERO, _NEG)
                carry = carry + jnp.sum(eqf, axis=0, keepdims=True)
            return carry

        def all_ties():
            lax.fori_loop(0, npair, bias_all_ties, 0)

        def some_ties():
            r = lax.broadcasted_iota(jnp.int32, (tq, tq), 0)
            cidx = lax.broadcasted_iota(jnp.int32, (tq, tq), 1)
            tri_ref[...] = _ind(cidx < r).astype(BF16)
            lax.fori_loop(0, npair, bias_some_ties, zero_row)

        simple = jnp.logical_and(jnp.max(n_eq - need) <= 0.0, jnp.min(thr) > ninf)
        lax.cond(simple, all_ties, some_ties)

        def pad_body(c, _):
            st_ref[chunk(c), :] = jnp.full((tq, tq), _NEG, F32)
            return 0
        lax.fori_loop(2 * npair, trips * ATT_UNROLL, pad_body, 0)

    hd = HEAD_DIM
    heads = range(HEADS_PER_STEP)
    q_list = [q_ref[:, h * hd:(h + 1) * hd] for h in heads]
    for h in heads:
        acc_ref[h] = jnp.zeros((hd, tq), F32)
    ls = _two_pass_attention(
        q_list, k_ref, vt_ref, sc_ref, acc_ref, lambda h, c: st_ref[c * tq:(c + 1) * tq, :], trips, tq,
        [jnp.full((1, tq), _NEG, F32)] * HEADS_PER_STEP, [jnp.zeros((1, tq), F32)] * HEADS_PER_STEP)
    for h in heads:
        o = (acc_ref[h] / ls[h]).T
        g = g_ref[:, h * hd:(h + 1) * hd].astype(F32)
        o_ref[:, h * hd:(h + 1) * hd] = (o * _silu(g)).astype(o_ref.dtype)


def _dsa(act, act_t, cat, wt, batch, seq, q_col, k_col, g_col, v_row, tq=256):
    n = act.shape[0]
    nq = seq // tq
    topk = min(DSA_TOPK_MAX, seq // 4)
    n_qcat = IDX_HEADS * IDX_SLOT
    hp = HEADS_PER_STEP
    w = hp * HEAD_DIM
    q_cb, k_cb, g_cb, v_rb = q_col // w, k_col // w, g_col // w, v_row // w
    return pl.pallas_call(
        functools.partial(_dsa_kernel, topk=topk, tq=tq),
        grid=(batch, nq, DSA_HEADS // hp),
        in_specs=[pl.BlockSpec((tq, n_qcat), lambda b, i, h: (b * nq + i, 0)),
                  pl.BlockSpec((seq, IDX_SLOT), lambda b, i, h: (b, n_qcat // IDX_SLOT)),
                  pl.BlockSpec((8, tq), lambda b, i, h: (0, b * nq + i)),
                  pl.BlockSpec((tq, w), lambda b, i, h: (b * nq + i, q_cb + h)),
                  pl.BlockSpec((seq, w), lambda b, i, h: (b, k_cb + h)),
                  pl.BlockSpec((w, seq), lambda b, i, h: (v_rb + h, b)),
                  pl.BlockSpec((tq, w), lambda b, i, h: (b * nq + i, g_cb + h))],
        out_specs=pl.BlockSpec((tq, w), lambda b, i, h: (b * nq + i, h)),
        out_shape=jax.ShapeDtypeStruct((n, BRANCH_W), BF16),
        scratch_shapes=[pltpu.VMEM((seq, tq), F32),
                        pltpu.VMEM((tq, tq), BF16),
                        pltpu.VMEM((hp, seq, tq), F32),
                        pltpu.VMEM((hp, HEAD_DIM, tq), F32)],
        compiler_params=_cparams(("parallel", "arbitrary", "arbitrary")),
        name="dsa",
    )(cat, cat, wt, act, act, act_t, act)


def _mem_kernel(q_ref, mk_ref, mvt_ref, g_ref, o_ref):
    hd = MEM_HEAD_DIM
    for h in range(MEM_HEADS):
        cols = slice(h * hd, (h + 1) * hd)
        s = _dot_nt(mk_ref[:, cols], q_ref[:, cols])
        m = jnp.max(s, axis=0, keepdims=True)
        p = jnp.exp2(s - m)
        l = jnp.sum(p, axis=0, keepdims=True)
        o = (_dot(mvt_ref[cols, :], p.astype(BF16)) / l).T
        o_ref[:, cols] = (o * _silu(g_ref[:, cols].astype(F32))).astype(o_ref.dtype)


def _mem_attn(act, mk, mvt, batch, seq, mem_len, q_col, g_col, tq=512):
    n = act.shape[0]
    nq = seq // tq
    w = BRANCH_W
    return pl.pallas_call(
        _mem_kernel,
        grid=(batch, nq),
        in_specs=[pl.BlockSpec((tq, w), lambda b, i: (b * nq + i, q_col // w)),
                  pl.BlockSpec((mem_len, w), lambda b, i: (b, 0)),
                  pl.BlockSpec((w, mem_len), lambda b, i: (0, b)),
                  pl.BlockSpec((tq, w), lambda b, i: (b * nq + i, g_col // w))],
        out_specs=pl.BlockSpec((tq, w), lambda b, i: (b * nq + i, 0)),
        out_shape=jax.ShapeDtypeStruct((n, BRANCH_W), BF16),
        compiler_params=_cparams(("parallel", "parallel")),
        name="mem_attn",
    )(act, mk, mvt, act)


O_IDX = 14 * BRANCH_W
O_MERGE = O_IDX + IDX_HEADS * IDX_DIM + IDX_DIM + IDX_HEADS


def _input_weights(w_in3):
    wt3 = jnp.swapaxes(w_in3, 1, 2).astype(BF16)
    idx3 = jnp.swapaxes(w_in3[:, :, O_IDX:O_MERGE], 1, 2)
    idx3 = jnp.pad(idx3, ((0, 0), (0, IDX_W_IN - idx3.shape[1]), (0, 0)))
    return wt3, idx3


def _layer_weights(layer, wt3, idx3, w_mem_kv3, w_branch3, w_out3):
    bw = BRANCH_W
    w_idx_hi, w_idx_lo = _split2(idx3[layer])
    return dict(w_merge=wt3[layer, O_MERGE:], w_idx_hi=w_idx_hi, w_idx_lo=w_idx_lo,
                w_mk=w_mem_kv3[layer][:, :bw].astype(BF16),
                w_mvt=_transpose_cast(w_mem_kv3[layer][:, bw:], "w_mvt"),
                w_branch=w_branch3[layer].astype(BF16), w_out=w_out3[layer].astype(BF16))


def _layer(layer, x2, mem2, batch, seq, mem_len, ln_g, conv_w, conv_b, mem_ln_g, wt3, w, final_g):
    nb = BRANCH_W // LANE
    bw = BRANCH_W
    (A_Q, A_K, A_G, C_B, C_C, C_H, C_G, S_Q, S_K, S_G, M_Q, M_G) = range(12)
    tok_groups = (0, 1, 3, 4, 5, 6, 7, 8, 9, 11, 12, 13)
    val_groups = (2, 10)

    log2e = float(np.log2(np.e))
    colscale = np.ones((1, len(tok_groups) * bw), np.float32)
    for col, hdim in ((A_Q, HEAD_DIM), (S_Q, HEAD_DIM), (M_Q, MEM_HEAD_DIM)):
        colscale[:, col * bw:(col + 1) * bw] = hdim ** -0.5 * log2e

    xn, cat, wt = _prep(x2, ln_g, w["w_idx_hi"], w["w_idx_lo"])
    act = _proj_tok(xn, wt3, layer, tok_groups, jnp.asarray(colscale), tm=1024, tn=bw)
    act_r = _mm_nt(xn, w["w_merge"], BF16, tm=1024, tn=1024, name="proj_merge")
    act_t = _proj_vt(wt3, layer, val_groups, xn, tm=bw, tn=1024)

    y_a = _moba(act, act_t, batch, seq, A_Q * bw, A_K * bw, A_G * bw, 0)
    y_c = _conv(act, conv_w, conv_b, batch, seq, C_B * nb, C_C * nb, C_H * nb, C_G * nb)
    y_s = _dsa(act, act_t, cat, wt, batch, seq, S_Q * bw, S_K * bw, S_G * bw, bw)

    mem_n = _rmsnorm(mem2, mem_ln_g, BF16)
    mk = _mm_nn(mem_n, w["w_mk"], BF16, tm=mem2.shape[0], tn=512, name="mem_k")
    mvt = _mm_nt(w["w_mvt"], mem_n, BF16, tm=512, tn=mem2.shape[0], name="mem_vt")
    y_m = _mem_attn(act, mk, mvt, batch, seq, mem_len, M_Q * bw, M_G * bw)

    merged = _merge((y_a, y_c, y_s, y_m), act_r, w["w_branch"], tm=512, tn=1024)
    return _out_proj(merged, w["w_out"], x2, final_g, tm=512)


def kernel(x, mem, ln_g, w_in, conv_w, conv_b, mem_ln_g, w_mem_kv, w_branch, w_out, final_g):
    batch, seq, d = x.shape
    mem_len = mem.shape[1]
    x2 = x.reshape(batch * seq, d)
    mem2 = mem.reshape(batch * mem_len, d)
    wt3, idx3 = _input_weights(w_in)
    depth = ln_g.shape[0]
    for layer in range(depth):
        w = _layer_weights(layer, wt3, idx3, w_mem_kv, w_branch, w_out)
        x2 = _layer(layer, x2, mem2, batch, seq, mem_len, ln_g[layer], conv_w[layer],
                    conv_b[layer], mem_ln_g[layer], wt3, w,
                    final_g if layer == depth - 1 else None)
    return x2.reshape(batch, seq, d)
```

```python
import functools

import numpy as np
import jax
import jax.numpy as jnp
from jax import lax
from jax.experimental import pallas as pl
from jax.experimental.pallas import tpu as pltpu

EPS = 1e-6
BRANCH_W = 1024
MOBA_HEADS = 8
MOBA_BLOCK = 256
MOBA_TOPK = 3
DSA_HEADS = 8
DSA_TOPK_MAX = 256
IDX_HEADS = 4
IDX_DIM = 64
MEM_HEADS = 4
HEAD_DIM = 128
MEM_HEAD_DIM = 256
IDX_SLOT = 256
NEG = -1e30
ATT_UNROLL = 2
HEADS_PER_STEP = 4
BISECT_ITERS = 18
LANE = 128
VMEM_LIMIT = 56 * 1024 * 1024

F32 = jnp.float32
BF16 = jnp.bfloat16
_ONE = np.float32(1.0)
_ZERO = np.float32(0.0)
_NEG = np.float32(NEG)


def _cparams(sem):
    return pltpu.CompilerParams(dimension_semantics=sem, vmem_limit_bytes=VMEM_LIMIT)


def _dot(a, b):
    return jnp.dot(a, b, preferred_element_type=F32)


def _dot_nt(a, b):
    return lax.dot_general(a, b, (((1,), (1,)), ((), ())), preferred_element_type=F32)


def _split2(v):
    hi = v.astype(BF16)
    lo = (v - hi.astype(F32)).astype(BF16)
    return hi, lo


def _split3(v):
    h1 = v.astype(BF16)
    r1 = v - h1.astype(F32)
    h2 = r1.astype(BF16)
    h3 = (r1 - h2.astype(F32)).astype(BF16)
    return h1, h2, h3


def _ind(cond):
    return jnp.where(cond, _ONE, _ZERO)


def _silu(g):
    return g * jax.nn.sigmoid(g)


def _rms(x, g):
    var = jnp.mean(x * x, axis=-1, keepdims=True)
    return (x * lax.rsqrt(var + EPS)) * g


def _rmsnorm_kernel(x_ref, g_ref, o_ref):
    o_ref[...] = _rms(x_ref[...], g_ref[...]).astype(o_ref.dtype)


def _rmsnorm(x, g, out_dtype, tm=256):
    n, d = x.shape
    return pl.pallas_call(
        _rmsnorm_kernel,
        grid=(n // tm,),
        in_specs=[pl.BlockSpec((tm, d), lambda i: (i, 0)),
                  pl.BlockSpec((1, d), lambda i: (0, 0))],
        out_specs=pl.BlockSpec((tm, d), lambda i: (i, 0)),
        out_shape=jax.ShapeDtypeStruct((n, d), out_dtype),
        compiler_params=_cparams(("parallel",)),
        name="rmsnorm",
    )(x, g.reshape(1, d))


def _prep_kernel(x_ref, g_ref, whi_ref, wlo_ref, phi_ref, plo_ref, xn_ref, cat_ref, wt_ref):
    xn = _rms(x_ref[...], g_ref[...])
    hi, lo = _split2(xn)
    xn_ref[...] = hi
    acc = _dot_nt(hi, whi_ref[...]) + _dot_nt(hi, wlo_ref[...]) + _dot_nt(lo, whi_ref[...])
    vh, vl = _split2(acc)
    cat_ref[...] = (_dot(vh, phi_ref[...]) + _dot(vl, plo_ref[...])).astype(BF16)
    gate_row = IDX_DIM
    wt_ref[...] = acc[:, IDX_W_IN - LANE:].T[gate_row:gate_row + 8, :]


IDX_W_IN = 384


def _idx_placement():
    n_out = (IDX_HEADS + 1) * IDX_SLOT
    p_hi = np.zeros((IDX_W_IN, n_out), np.float32)
    p_lo = np.zeros((IDX_W_IN, n_out), np.float32)
    e = np.arange(IDX_DIM)
    for j in range(IDX_HEADS):
        src, dst = j * IDX_DIM + e, j * IDX_SLOT + e
        p_hi[src, dst] = 1
        p_hi[src, dst + IDX_DIM] = 1
        p_lo[src, dst + 2 * IDX_DIM] = 1
    src, dst = IDX_HEADS * IDX_DIM + e, IDX_HEADS * IDX_SLOT + e
    p_hi[src, dst] = 1
    p_lo[src, dst + IDX_DIM] = 1
    p_hi[src, dst + 2 * IDX_DIM] = 1
    return jnp.asarray(p_hi, BF16), jnp.asarray(p_lo, BF16)


def _prep(x, g, whi, wlo, tm=512):
    n, d = x.shape
    p_hi, p_lo = _idx_placement()
    co = p_hi.shape[1]
    return pl.pallas_call(
        _prep_kernel,
        grid=(n // tm,),
        in_specs=[pl.BlockSpec((tm, d), lambda i: (i, 0)),
                  pl.BlockSpec((1, d), lambda i: (0, 0)),
                  pl.BlockSpec((IDX_W_IN, d), lambda i: (0, 0)),
                  pl.BlockSpec((IDX_W_IN, d), lambda i: (0, 0)),
                  pl.BlockSpec((IDX_W_IN, co), lambda i: (0, 0)),
                  pl.BlockSpec((IDX_W_IN, co), lambda i: (0, 0))],
        out_specs=[pl.BlockSpec((tm, d), lambda i: (i, 0)),
                   pl.BlockSpec((tm, co), lambda i: (i, 0)),
                   pl.BlockSpec((8, tm), lambda i: (0, i))],
        out_shape=[jax.ShapeDtypeStruct((n, d), BF16),
                   jax.ShapeDtypeStruct((n, co), BF16),
                   jax.ShapeDtypeStruct((8, n), F32)],
        compiler_params=_cparams(("parallel",)),
        name="prep",
    )(x, g.reshape(1, d), whi, wlo, p_hi, p_lo)


def _transpose_cast_kernel(w_ref, o_ref):
    o_ref[...] = w_ref[...].T.astype(o_ref.dtype)


def _transpose_cast(w, name, t=512):
    d, n = w.shape
    return pl.pallas_call(
        _transpose_cast_kernel,
        grid=(n // t, d // t),
        in_specs=[pl.BlockSpec((t, t), lambda i, j: (j, i))],
        out_specs=pl.BlockSpec((t, t), lambda i, j: (i, j)),
        out_shape=jax.ShapeDtypeStruct((n, d), BF16),
        compiler_params=_cparams(("parallel", "parallel")),
        name=name,
    )(w)


def _mm_nn_kernel(a_ref, b_ref, o_ref):
    o_ref[...] = _dot(a_ref[...], b_ref[...]).astype(o_ref.dtype)


def _mm_nn(a, b, out_dtype, tm, tn, name):
    m, k = a.shape
    n = b.shape[1]
    return pl.pallas_call(
        _mm_nn_kernel,
        grid=(m // tm, n // tn),
        in_specs=[pl.BlockSpec((tm, k), lambda i, j: (i, 0)),
                  pl.BlockSpec((k, tn), lambda i, j: (0, j))],
        out_specs=pl.BlockSpec((tm, tn), lambda i, j: (i, j)),
        out_shape=jax.ShapeDtypeStruct((m, n), out_dtype),
        compiler_params=_cparams(("parallel", "parallel")),
        name=name,
    )(a, b)


def _proj_tok_kernel(blk_ref, a_ref, b_ref, s_ref, o_ref):
    del blk_ref
    o_ref[...] = (_dot_nt(a_ref[...], b_ref[...]) * s_ref[...]).astype(o_ref.dtype)


def _proj_tok(a, wt3, layer, row_blocks, colscale, tm, tn):
    m, k = a.shape
    n = len(row_blocks) * tn
    return pl.pallas_call(
        _proj_tok_kernel,
        grid_spec=pltpu.PrefetchScalarGridSpec(
            num_scalar_prefetch=1,
            grid=(m // tm, n // tn),
            in_specs=[pl.BlockSpec((tm, k), lambda i, j, blk: (i, 0)),
                      pl.BlockSpec((None, tn, k), lambda i, j, blk: (layer, blk[j], 0)),
                      pl.BlockSpec((1, tn), lambda i, j, blk: (0, j))],
            out_specs=pl.BlockSpec((tm, tn), lambda i, j, blk: (i, j))),
        out_shape=jax.ShapeDtypeStruct((m, n), BF16),
        compiler_params=_cparams(("parallel", "parallel")),
        name="proj_tok",
    )(jnp.asarray(row_blocks, jnp.int32), a, wt3, colscale)


def _proj_vt_kernel(blk_ref, a_ref, b_ref, o_ref):
    del blk_ref
    o_ref[...] = _dot_nt(a_ref[...], b_ref[...]).astype(o_ref.dtype)


def _proj_vt(wt3, layer, row_blocks, b, tm, tn):
    n, k = b.shape
    m = len(row_blocks) * tm
    return pl.pallas_call(
        _proj_vt_kernel,
        grid_spec=pltpu.PrefetchScalarGridSpec(
            num_scalar_prefetch=1,
            grid=(n // tn, m // tm),
            in_specs=[pl.BlockSpec((None, tm, k), lambda j, i, blk: (layer, blk[i], 0)),
                      pl.BlockSpec((tn, k), lambda j, i, blk: (j, 0))],
            out_specs=pl.BlockSpec((tm, tn), lambda j, i, blk: (i, j))),
        out_shape=jax.ShapeDtypeStruct((m, n), BF16),
        compiler_params=_cparams(("parallel", "parallel")),
        name="proj_vt",
    )(jnp.asarray(row_blocks, jnp.int32), wt3, b)


def _mm_nt_kernel(a_ref, b_ref, o_ref):
    o_ref[...] = _dot_nt(a_ref[...], b_ref[...]).astype(o_ref.dtype)


def _mm_nt(a, b, out_dtype, tm, tn, name):
    m, k = a.shape
    n = b.shape[0]
    return pl.pallas_call(
        _mm_nt_kernel,
        grid=(n // tn, m // tm),
        in_specs=[pl.BlockSpec((tm, k), lambda j, i: (i, 0)),
                  pl.BlockSpec((tn, k), lambda j, i: (j, 0))],
        out_specs=pl.BlockSpec((tm, tn), lambda j, i: (i, j)),
        out_shape=jax.ShapeDtypeStruct((m, n), out_dtype),
        compiler_params=_cparams(("parallel", "parallel")),
        name=name,
    )(a, b)


def _out_proj_kernel(a_ref, b_ref, r_ref, g_ref, o_ref, *, final_norm):
    y = r_ref[...] + _dot(a_ref[...], b_ref[...])
    o_ref[...] = _rms(y, g_ref[...]) if final_norm else y


def _out_proj(a, b, res, final_g, tm):
    m, k = a.shape
    n = b.shape[1]
    g = jnp.ones((1, n), F32) if final_g is None else final_g.reshape(1, n)
    return pl.pallas_call(
        functools.partial(_out_proj_kernel, final_norm=final_g is not None),
        grid=(m // tm,),
        in_specs=[pl.BlockSpec((tm, k), lambda i: (i, 0)),
                  pl.BlockSpec((k, n), lambda i: (0, 0)),
                  pl.BlockSpec((tm, n), lambda i: (i, 0)),
                  pl.BlockSpec((1, n), lambda i: (0, 0))],
        out_specs=pl.BlockSpec((tm, n), lambda i: (i, 0)),
        out_shape=jax.ShapeDtypeStruct((m, n), F32),
        compiler_params=_cparams(("parallel",)),
        name="out_proj",
    )(a, b, res, g)


def _merge_kernel(ya_ref, yc_ref, ys_ref, ym_ref, ra_ref, rc_ref, rs_ref, rm_ref, wb_ref, o_ref):
    ys = (ya_ref, yc_ref, ys_ref, ym_ref)
    rs = (ra_ref, rc_ref, rs_ref, rm_ref)
    acc = None
    for br in range(4):
        z = _dot(ys[br][...], wb_ref[br])
        term = jax.nn.sigmoid(rs[br][...].astype(F32)) * z
        acc = term if acc is None else acc + term
    o_ref[...] = acc.astype(o_ref.dtype)


def _merge(ys, gates, wb, tm, tn):
    n, w = ys[0].shape
    d = wb.shape[2]
    y_specs = [pl.BlockSpec((tm, w), lambda j, i: (i, 0)) for _ in range(4)]
    r_specs = [pl.BlockSpec((tm, tn), functools.partial(
        lambda j, i, base: (i, base + j), base=br * d // tn)) for br in range(4)]
    return pl.pallas_call(
        _merge_kernel,
        grid=(d // tn, n // tm),
        in_specs=y_specs + r_specs + [pl.BlockSpec((4, w, tn), lambda j, i: (0, 0, j))],
        out_specs=pl.BlockSpec((tm, tn), lambda j, i: (i, j)),
        out_shape=jax.ShapeDtypeStruct((n, d), BF16),
        compiler_params=_cparams(("parallel", "parallel")),
        name="merge",
    )(*ys, gates, gates, gates, gates, wb)


def _ceil_div_unroll(n):
    return (n + (ATT_UNROLL - 1)) >> (ATT_UNROLL.bit_length() - 1)


def _two_pass_attention(q_list, k_ref, vt_ref, sc_ref, acc_ref, bias_fn, trips, blk, own=None):
    heads = range(len(q_list))
    hd = HEAD_DIM
    span = ATT_UNROLL * blk
    max_trips = k_ref.shape[0] // span
    tq = q_list[0].shape[0]

    def run(n_trips):
        ms = []
        for h in heads:
            m = jnp.full((1, tq), _NEG, F32)
            for j in range(n_trips):
                s = _dot_nt(k_ref[j * span:(j + 1) * span, h * hd:(h + 1) * hd], q_list[h])
                for u in range(ATT_UNROLL):
                    n = j * ATT_UNROLL + u
                    su = s[u * blk:(u + 1) * blk, :] + bias_fn(h, n)
                    sc_ref[h, n * blk:(n + 1) * blk, :] = su
                    m = jnp.maximum(m, jnp.max(su, axis=0, keepdims=True))
            if own is not None:
                off, keep, own_ref = own
                s = _dot_nt(k_ref[pl.ds(off, blk), h * hd:(h + 1) * hd], q_list[h])
                s = jnp.where(keep, s, _NEG)
                own_ref[h] = s
                m = jnp.maximum(m, jnp.max(s, axis=0, keepdims=True))
            ms.append(m)
        ls = []
        for h in heads:
            l = jnp.zeros((1, tq), F32)
            acc = None
            for j in range(n_trips):
                p = jnp.exp2(sc_ref[h, j * span:(j + 1) * span, :] - ms[h])
                l = l + jnp.sum(p, axis=0, keepdims=True)
                pv = _dot(vt_ref[h * hd:(h + 1) * hd, j * span:(j + 1) * span], p.astype(BF16))
                acc = pv if acc is None else acc + pv
            if own is not None:
                off, keep, own_ref = own
                p = jnp.exp2(own_ref[h] - ms[h])
                l = l + jnp.sum(p, axis=0, keepdims=True)
                pv = _dot(vt_ref[h * hd:(h + 1) * hd, pl.ds(off, blk)], p.astype(BF16))
                acc = pv if acc is None else acc + pv
            acc_ref[h] = jnp.zeros((hd, tq), F32) if acc is None else acc
            ls.append(l)
        return tuple(ls)

    return lax.switch(trips, [functools.partial(run, t) for t in range(max_trips + 1)])


def _moba_kernel(q_ref, k_ref, vt_ref, g_ref, o_ref, kmean_ref, selb_ref, sc_ref, acc_ref, own_ref,
                 *, nblk):
    i = pl.program_id(2)
    blk = MOBA_BLOCK
    hd = HEAD_DIM
    heads = range(HEADS_PER_STEP)

    @pl.when(i == 0)
    def _():
        for h in heads:
            kf = k_ref[:, h * hd:(h + 1) * hd].astype(F32).reshape(nblk, blk, hd)
            kmean_ref[h] = jnp.mean(kf, axis=1)

    q_list = []
    for h in heads:
        q = q_ref[:, h * hd:(h + 1) * hd]
        q_list.append(q)
        k1, k2, k3 = _split3(kmean_ref[h])
        bs = _dot_nt(k1, q) + _dot_nt(k2, q) + _dot_nt(k3, q)
        n_iota = lax.broadcasted_iota(jnp.int32, bs.shape, 0)
        past = n_iota < i
        left = jnp.where(past, bs, -jnp.inf)
        bias = jnp.full(bs.shape, _NEG, F32)
        for _ in range(MOBA_TOPK):
            top = jnp.max(left, axis=0, keepdims=True)
            first = jnp.min(jnp.where(left == top, n_iota, nblk), axis=0, keepdims=True)
            hit = n_iota == first
            bias = jnp.where(hit & past, _ZERO, bias)
            left = jnp.where(hit, -jnp.inf, left)
        selb_ref[h] = bias

    kpos = lax.broadcasted_iota(jnp.int32, (blk, blk), 0)
    qpos = lax.broadcasted_iota(jnp.int32, (blk, blk), 1)
    ls = _two_pass_attention(
        q_list, k_ref, vt_ref, sc_ref, acc_ref, lambda h, n: selb_ref[h, pl.ds(n, 1), :],
        _ceil_div_unroll(i), blk,
        own=(pl.multiple_of(i * blk, blk), kpos <= qpos, own_ref))
    for h in heads:
        o = (acc_ref[h] / ls[h]).T
        g = g_ref[:, h * hd:(h + 1) * hd].astype(F32)
        o_ref[:, h * hd:(h + 1) * hd] = (o * _silu(g)).astype(o_ref.dtype)


def _moba(act, act_t, batch, seq, q_col, k_col, g_col, v_row):
    nblk = seq // MOBA_BLOCK
    n = act.shape[0]
    blk = MOBA_BLOCK
    hp = HEADS_PER_STEP
    w = hp * HEAD_DIM
    q_cb, k_cb, g_cb, v_rb = q_col // w, k_col // w, g_col // w, v_row // w
    return pl.pallas_call(
        functools.partial(_moba_kernel, nblk=nblk),
        grid=(batch, MOBA_HEADS // hp, nblk),
        in_specs=[pl.BlockSpec((blk, w), lambda b, h, i: (b * nblk + i, q_cb + h)),
                  pl.BlockSpec((seq, w), lambda b, h, i: (b, k_cb + h)),
                  pl.BlockSpec((w, seq), lambda b, h, i: (v_rb + h, b)),
                  pl.BlockSpec((blk, w), lambda b, h, i: (b * nblk + i, g_cb + h))],
        out_specs=pl.BlockSpec((blk, w), lambda b, h, i: (b * nblk + i, h)),
        out_shape=jax.ShapeDtypeStruct((n, BRANCH_W), BF16),
        scratch_shapes=[pltpu.VMEM((hp, nblk, HEAD_DIM), F32),
                        pltpu.VMEM((hp, nblk, blk), F32),
                        pltpu.VMEM((hp, seq, blk), F32),
                        pltpu.VMEM((hp, HEAD_DIM, blk), F32),
                        pltpu.VMEM((hp, blk, blk), F32)],
        compiler_params=_cparams(("parallel", "parallel", "arbitrary")),
        name="moba",
    )(act, act, act_t, act)


def _conv_kernel(cb_ref, cc_ref, ch_ref, cg_ref, w_ref, bias_ref, o_ref):
    u = cc_ref[...].astype(F32) * ch_ref[...].astype(F32)
    t = lax.broadcasted_iota(jnp.int32, u.shape, 0)
    u1 = jnp.where(t >= 1, pltpu.roll(u, 1, 0), 0.0)
    u2 = jnp.where(t >= 2, pltpu.roll(u, 2, 0), 0.0)
    w = w_ref[...]
    conv = u2 * w[0:1, :] + u1 * w[1:2, :] + u * w[2:3, :] + bias_ref[...]
    o_ref[...] = (cb_ref[...].astype(F32) * conv * _silu(cg_ref[...].astype(F32))).astype(o_ref.dtype)


def _conv(act, conv_w, conv_b, batch, seq, cb_cb, cc_cb, ch_cb, cg_cb):
    n = act.shape[0]
    ncb = BRANCH_W // LANE
    kw = conv_w.shape[0]
    wpad = jnp.zeros((8, BRANCH_W), F32).at[:kw].set(conv_w)

    def spec(base):
        return pl.BlockSpec((seq, LANE), lambda b, c: (b, base + c))

    return pl.pallas_call(
        _conv_kernel,
        grid=(batch, ncb),
        in_specs=[spec(cb_cb), spec(cc_cb), spec(ch_cb), spec(cg_cb),
                  pl.BlockSpec((8, LANE), lambda b, c: (0, c)),
                  pl.BlockSpec((1, LANE), lambda b, c: (0, c))],
        out_specs=pl.BlockSpec((seq, LANE), lambda b, c: (b, c)),
        out_shape=jax.ShapeDtypeStruct((n, BRANCH_W), BF16),
        compiler_params=_cparams(("parallel", "parallel")),
        name="conv",
    )(act, act, act, act, wpad, conv_b.reshape(1, BRANCH_W))


def _dsa_kernel(qc_ref, kc_ref, wt_ref, q_ref, k_ref, vt_ref, g_ref, o_ref, st_ref, tri_ref, sc_ref,
                acc_ref, *, topk, tq):
    i = pl.program_id(1)
    nch = i + 1
    npair = (nch + 1) >> 1
    trips = _ceil_div_unroll(nch)
    idx_scale = (IDX_DIM ** -0.5) * (IDX_HEADS ** -0.5)
    kf = float(topk)
    ninf = np.float32(-np.inf)

    def chunk(c):
        return pl.ds(pl.multiple_of(c * tq, tq), tq)

    def pair(c):
        return pl.ds(pl.multiple_of(c * (2 * tq), 2 * tq), 2 * tq)

    def fold8(x, op):
        return op(x.reshape(x.shape[0] // 8, 8, tq), axis=0)

    def count(pred):
        def body(c, acc):
            for u in range(2):
                acc = acc + fold8(_ind(pred(st_ref[chunk(2 * c + u), :])), jnp.sum)
            return acc
        acc = lax.fori_loop(0, npair, body, jnp.zeros((8, tq), F32))
        return jnp.sum(acc, axis=0, keepdims=True)

    @pl.when(pl.program_id(2) == 0)
    def _select():
        def score_body(c, carry):
            mx, mn = carry
            for u in range(2):
                cc = 2 * c + u
                kc = kc_ref[chunk(cc), :]
                sc = jnp.zeros((tq, tq), F32)
                for j in range(IDX_HEADS):
                    lg = _dot_nt(kc, qc_ref[:, j * IDX_SLOT:(j + 1) * IDX_SLOT])
                    sc = sc + (wt_ref[j:j + 1, :] * idx_scale) * jnp.maximum(lg, 0.0)
                kpos = cc * tq + lax.broadcasted_iota(jnp.int32, sc.shape, 0)
                qpos = i * tq + lax.broadcasted_iota(jnp.int32, sc.shape, 1)
                causal = kpos <= qpos
                st_ref[chunk(cc), :] = jnp.where(causal, sc, ninf)
                mx = jnp.maximum(mx, fold8(jnp.where(causal, sc, ninf), jnp.max))
                mn = jnp.minimum(mn, fold8(jnp.where(causal, sc, -ninf), jnp.min))
            return mx, mn
        mx8, mn8 = lax.fori_loop(0, npair, score_body,
                                 (jnp.full((8, tq), ninf, F32), jnp.full((8, tq), -ninf, F32)))
        smax = jnp.max(mx8, axis=0, keepdims=True)
        smin = jnp.min(mn8, axis=0, keepdims=True)

        qrow = i * tq + lax.broadcasted_iota(jnp.int32, (1, tq), 1)
        short = qrow + 1 < topk
        lo0 = jnp.where(short, _ZERO, smin)
        hi0 = jnp.where(short, _ONE, smax + jnp.maximum(jnp.abs(smax), np.float32(1e-30))
                        * np.float32(2.0 ** -10))

        def bis_body(_, carry):
            lo, hi = carry
            mid = lo + (hi - lo) * np.float32(0.5)
            ge = count(lambda s: s >= mid) >= kf
            return jnp.where(ge, mid, lo), jnp.where(ge, hi, mid)

        _, hi_f0 = lax.fori_loop(0, BISECT_ITERS, bis_body, (lo0, hi0))

        def walk_cond(carry):
            return jnp.min(carry[3]) < 0.5

        def walk_body(carry):
            hi_f, thr, cnt, done = carry

            def mx_body(c, acc):
                s = st_ref[pair(c), :]
                return jnp.maximum(acc, fold8(jnp.where(s < hi_f, s, ninf), jnp.max))
            mx = lax.fori_loop(0, npair, mx_body, jnp.full((8, tq), ninf, F32))
            v = jnp.max(mx, axis=0, keepdims=True)
            c_v = count(lambda s: s >= v)
            is_done = done > 0.5
            return (jnp.where(is_done, hi_f, v), jnp.where(is_done, thr, v),
                    jnp.where(is_done, cnt, c_v), _ind(is_done | (c_v >= kf)))

        zero_row = jnp.zeros((1, tq), F32)
        all_rows = (2 * npair * tq).astype(F32)
        _, thr, n_ge, _ = lax.while_loop(
            walk_cond, walk_body,
            (hi_f0, jnp.full((1, tq), ninf, F32), zero_row + all_rows, _ind(short)))

        n_gt = count(lambda s: s > thr)
        need = kf - n_gt
        n_eq = n_ge - n_gt

        def bias_all_ties(c, _):
            st_ref[pair(c), :] = jnp.where(st_ref[pair(c), :] >= thr, _ZERO, _NEG)
            return 0

        def bias_some_ties(c, carry):
            for u in range(2):
                cc = 2 * c + u
                s = st_ref[chunk(cc), :]
                eq = s == thr
                eqf = _ind(eq)
                before = _dot(tri_ref[...], eqf.astype(BF16)) + carry
                kpos = cc * tq + lax.broadcasted_iota(jnp.int32, s.shape, 0)
                qpos = i * tq + lax.broadcasted_iota(jnp.int32, s.shape, 1)
                selected = ((s > thr) | (eq & (before < need))) & (kpos <= qpos)
                st_ref[chunk(cc), :] = jnp.where(selected, _ZERO, _NEG)
                carry = carry + jnp.sum(eqf, axis=0, keepdims=True)
            return carry

        def all_ties():
            lax.fori_loop(0, npair, bias_all_ties, 0)

        def some_ties():
            r = lax.broadcasted_iota(jnp.int32, (tq, tq), 0)
            cidx = lax.broadcasted_iota(jnp.int32, (tq, tq), 1)
            tri_ref[...] = _ind(cidx < r).astype(BF16)
            lax.fori_loop(0, npair, bias_some_ties, zero_row)

        simple = jnp.logical_and(jnp.max(n_eq - need) <= 0.0, jnp.min(thr) > ninf)
        lax.cond(simple, all_ties, some_ties)

        def pad_body(c, _):
            st_ref[chunk(c), :] = jnp.full((tq, tq), _NEG, F32)
            return 0
        lax.fori_loop(2 * npair, trips * ATT_UNROLL, pad_body, 0)

    hd = HEAD_DIM
    heads = range(HEADS_PER_STEP)
    q_list = [q_ref[:, h * hd:(h + 1) * hd] for h in heads]
    ls = _two_pass_attention(
        q_list, k_ref, vt_ref, sc_ref, acc_ref, lambda h, c: st_ref[c * tq:(c + 1) * tq, :], trips, tq)
    for h in heads:
        o = (acc_ref[h] / ls[h]).T
        g = g_ref[:, h * hd:(h + 1) * hd].astype(F32)
        o_ref[:, h * hd:(h + 1) * hd] = (o * _silu(g)).astype(o_ref.dtype)


def _dsa(act, act_t, cat, wt, batch, seq, q_col, k_col, g_col, v_row, tq=256):
    n = act.shape[0]
    nq = seq // tq
    topk = min(DSA_TOPK_MAX, seq // 4)
    n_qcat = IDX_HEADS * IDX_SLOT
    hp = HEADS_PER_STEP
    w = hp * HEAD_DIM
    q_cb, k_cb, g_cb, v_rb = q_col // w, k_col // w, g_col // w, v_row // w
    return pl.pallas_call(
        functools.partial(_dsa_kernel, topk=topk, tq=tq),
        grid=(batch, nq, DSA_HEADS // hp),
        in_specs=[pl.BlockSpec((tq, n_qcat), lambda b, i, h: (b * nq + i, 0)),
                  pl.BlockSpec((seq, IDX_SLOT), lambda b, i, h: (b, n_qcat // IDX_SLOT)),
                  pl.BlockSpec((8, tq), lambda b, i, h: (0, b * nq + i)),
                  pl.BlockSpec((tq, w), lambda b, i, h: (b * nq + i, q_cb + h)),
                  pl.BlockSpec((seq, w), lambda b, i, h: (b, k_cb + h)),
                  pl.BlockSpec((w, seq), lambda b, i, h: (v_rb + h, b)),
                  pl.BlockSpec((tq, w), lambda b, i, h: (b * nq + i, g_cb + h))],
        out_specs=pl.BlockSpec((tq, w), lambda b, i, h: (b * nq + i, h)),
        out_shape=jax.ShapeDtypeStruct((n, BRANCH_W), BF16),
        scratch_shapes=[pltpu.VMEM((seq, tq), F32),
                        pltpu.VMEM((tq, tq), BF16),
                        pltpu.VMEM((hp, seq, tq), F32),
                        pltpu.VMEM((hp, HEAD_DIM, tq), F32)],
        compiler_params=_cparams(("parallel", "arbitrary", "arbitrary")),
        name="dsa",
    )(cat, cat, wt, act, act, act_t, act)


def _mem_kernel(q_ref, mk_ref, mvt_ref, g_ref, o_ref):
    hd = MEM_HEAD_DIM
    for h in range(MEM_HEADS):
        cols = slice(h * hd, (h + 1) * hd)
        s = _dot_nt(mk_ref[:, cols], q_ref[:, cols])
        m = jnp.max(s, axis=0, keepdims=True)
        p = jnp.exp2(s - m)
        l = jnp.sum(p, axis=0, keepdims=True)
        o = (_dot(mvt_ref[cols, :], p.astype(BF16)) / l).T
        o_ref[:, cols] = (o * _silu(g_ref[:, cols].astype(F32))).astype(o_ref.dtype)


def _mem_attn(act, mk, mvt, batch, seq, mem_len, q_col, g_col, tq=512):
    n = act.shape[0]
    nq = seq // tq
    w = BRANCH_W
    return pl.pallas_call(
        _mem_kernel,
        grid=(batch, nq),
        in_specs=[pl.BlockSpec((tq, w), lambda b, i: (b * nq + i, q_col // w)),
                  pl.BlockSpec((mem_len, w), lambda b, i: (b, 0)),
                  pl.BlockSpec((w, mem_len), lambda b, i: (0, b)),
                  pl.BlockSpec((tq, w), lambda b, i: (b * nq + i, g_col // w))],
        out_specs=pl.BlockSpec((tq, w), lambda b, i: (b * nq + i, 0)),
        out_shape=jax.ShapeDtypeStruct((n, BRANCH_W), BF16),
        compiler_params=_cparams(("parallel", "parallel")),
        name="mem_attn",
    )(act, mk, mvt, act)


O_IDX = 14 * BRANCH_W
O_MERGE = O_IDX + IDX_HEADS * IDX_DIM + IDX_DIM + IDX_HEADS


def _input_weights(w_in3):
    wt3 = jnp.swapaxes(w_in3, 1, 2).astype(BF16)
    idx3 = jnp.swapaxes(w_in3[:, :, O_IDX:O_MERGE], 1, 2)
    idx3 = jnp.pad(idx3, ((0, 0), (0, IDX_W_IN - idx3.shape[1]), (0, 0)))
    return wt3, idx3


def _layer_weights(layer, wt3, idx3, w_mem_kv3, w_branch3, w_out3):
    bw = BRANCH_W
    w_idx_hi, w_idx_lo = _split2(idx3[layer])
    return dict(w_merge=wt3[layer, O_MERGE:], w_idx_hi=w_idx_hi, w_idx_lo=w_idx_lo,
                w_mk=w_mem_kv3[layer][:, :bw].astype(BF16),
                w_mvt=_transpose_cast(w_mem_kv3[layer][:, bw:], "w_mvt"),
                w_branch=w_branch3[layer].astype(BF16), w_out=w_out3[layer].astype(BF16))


def _layer(layer, x2, mem2, batch, seq, mem_len, ln_g, conv_w, conv_b, mem_ln_g, wt3, w, final_g):
    nb = BRANCH_W // LANE
    bw = BRANCH_W
    (A_Q, A_K, A_G, C_B, C_C, C_H, C_G, S_Q, S_K, S_G, M_Q, M_G) = range(12)
    tok_groups = (0, 1, 3, 4, 5, 6, 7, 8, 9, 11, 12, 13)
    val_groups = (2, 10)

    log2e = float(np.log2(np.e))
    colscale = np.ones((1, len(tok_groups) * bw), np.float32)
    for col, hdim in ((A_Q, HEAD_DIM), (S_Q, HEAD_DIM), (M_Q, MEM_HEAD_DIM)):
        colscale[:, col * bw:(col + 1) * bw] = hdim ** -0.5 * log2e

    xn, cat, wt = _prep(x2, ln_g, w["w_idx_hi"], w["w_idx_lo"])
    act = _proj_tok(xn, wt3, layer, tok_groups, jnp.asarray(colscale), tm=1024, tn=bw)
    act_r = _mm_nt(xn, w["w_merge"], BF16, tm=1024, tn=1024, name="proj_merge")
    act_t = _proj_vt(wt3, layer, val_groups, xn, tm=bw, tn=1024)

    y_a = _moba(act, act_t, batch, seq, A_Q * bw, A_K * bw, A_G * bw, 0)
    y_c = _conv(act, conv_w, conv_b, batch, seq, C_B * nb, C_C * nb, C_H * nb, C_G * nb)
    y_s = _dsa(act, act_t, cat, wt, batch, seq, S_Q * bw, S_K * bw, S_G * bw, bw)

    mem_n = _rmsnorm(mem2, mem_ln_g, BF16)
    mk = _mm_nn(mem_n, w["w_mk"], BF16, tm=mem2.shape[0], tn=512, name="mem_k")
    mvt = _mm_nt(w["w_mvt"], mem_n, BF16, tm=512, tn=mem2.shape[0], name="mem_vt")
    y_m = _mem_attn(act, mk, mvt, batch, seq, mem_len, M_Q * bw, M_G * bw)

    merged = _merge((y_a, y_c, y_s, y_m), act_r, w["w_branch"], tm=512, tn=1024)
    return _out_proj(merged, w["w_out"], x2, final_g, tm=512)


def kernel(x, mem, ln_g, w_in, conv_w, conv_b, mem_ln_g, w_mem_kv, w_branch, w_out, final_g):
    batch, seq, d = x.shape
    mem_len = mem.shape[1]
    x2 = x.reshape(batch * seq, d)
    mem2 = mem.reshape(batch * mem_len, d)
    wt3, idx3 = _input_weights(w_in)
    depth = ln_g.shape[0]
    for layer in range(depth):
        w = _layer_weights(layer, wt3, idx3, w_mem_kv, w_branch, w_out)
        x2 = _layer(layer, x2, mem2, batch, seq, mem_len, ln_g[layer], conv_w[layer],
                    conv_b[layer], mem_ln_g[layer], wt3, w,
                    final_g if layer == depth - 1 else None)
    return x2.reshape(batch, seq, d)
```

```python
import functools

import numpy as np
import jax
import jax.numpy as jnp
from jax import lax
from jax.experimental import pallas as pl
from jax.experimental.pallas import tpu as pltpu

EPS = 1e-6
BRANCH_W = 1024
MOBA_HEADS = 8
MOBA_BLOCK = 256
MOBA_TOPK = 3
DSA_HEADS = 8
DSA_TOPK_MAX = 256
IDX_HEADS = 4
IDX_DIM = 64
MEM_HEADS = 4
HEAD_DIM = 128
MEM_HEAD_DIM = 256
IDX_SLOT = 256
NEG = -1e30
ATT_UNROLL = 2
HEADS_PER_STEP = 4
BISECT_ITERS = 18
LANE = 128
VMEM_LIMIT = 56 * 1024 * 1024

F32 = jnp.float32
BF16 = jnp.bfloat16
_ONE = np.float32(1.0)
_ZERO = np.float32(0.0)
_NEG = np.float32(NEG)


def _cparams(sem):
    return pltpu.CompilerParams(dimension_semantics=sem, vmem_limit_bytes=VMEM_LIMIT)


def _dot(a, b):
    return jnp.dot(a, b, preferred_element_type=F32)


def _dot_nt(a, b):
    return lax.dot_general(a, b, (((1,), (1,)), ((), ())), preferred_element_type=F32)


def _split2(v):
    hi = v.astype(BF16)
    lo = (v - hi.astype(F32)).astype(BF16)
    return hi, lo


def _split3(v):
    h1 = v.astype(BF16)
    r1 = v - h1.astype(F32)
    h2 = r1.astype(BF16)
    h3 = (r1 - h2.astype(F32)).astype(BF16)
    return h1, h2, h3


def _ind(cond):
    return jnp.where(cond, _ONE, _ZERO)


def _silu(g):
    return g * jax.nn.sigmoid(g)


def _rms(x, g):
    var = jnp.mean(x * x, axis=-1, keepdims=True)
    return (x * lax.rsqrt(var + EPS)) * g


def _rmsnorm_kernel(x_ref, g_ref, o_ref):
    o_ref[...] = _rms(x_ref[...], g_ref[...]).astype(o_ref.dtype)


def _rmsnorm(x, g, out_dtype, tm=256):
    n, d = x.shape
    return pl.pallas_call(
        _rmsnorm_kernel,
        grid=(n // tm,),
        in_specs=[pl.BlockSpec((tm, d), lambda i: (i, 0)),
                  pl.BlockSpec((1, d), lambda i: (0, 0))],
        out_specs=pl.BlockSpec((tm, d), lambda i: (i, 0)),
        out_shape=jax.ShapeDtypeStruct((n, d), out_dtype),
        compiler_params=_cparams(("parallel",)),
        name="rmsnorm",
    )(x, g.reshape(1, d))


def _prep_kernel(x_ref, g_ref, whi_ref, wlo_ref, phi_ref, plo_ref, xn_ref, cat_ref, wt_ref):
    xn = _rms(x_ref[...], g_ref[...])
    hi, lo = _split2(xn)
    xn_ref[...] = hi
    acc = _dot_nt(hi, whi_ref[...]) + _dot_nt(hi, wlo_ref[...]) + _dot_nt(lo, whi_ref[...])
    vh, vl = _split2(acc)
    cat_ref[...] = (_dot(vh, phi_ref[...]) + _dot(vl, plo_ref[...])).astype(BF16)
    gate_row = IDX_DIM
    wt_ref[...] = acc[:, IDX_W_IN - LANE:].T[gate_row:gate_row + 8, :]


IDX_W_IN = 384


def _idx_placement():
    n_out = (IDX_HEADS + 1) * IDX_SLOT
    p_hi = np.zeros((IDX_W_IN, n_out), np.float32)
    p_lo = np.zeros((IDX_W_IN, n_out), np.float32)
    e = np.arange(IDX_DIM)
    for j in range(IDX_HEADS):
        src, dst = j * IDX_DIM + e, j * IDX_SLOT + e
        p_hi[src, dst] = 1
        p_hi[src, dst + IDX_DIM] = 1
        p_lo[src, dst + 2 * IDX_DIM] = 1
    src, dst = IDX_HEADS * IDX_DIM + e, IDX_HEADS * IDX_SLOT + e
    p_hi[src, dst] = 1
    p_lo[src, dst + IDX_DIM] = 1
    p_hi[src, dst + 2 * IDX_DIM] = 1
    return jnp.asarray(p_hi, BF16), jnp.asarray(p_lo, BF16)


def _prep(x, g, whi, wlo, tm=512):
    n, d = x.shape
    p_hi, p_lo = _idx_placement()
    co = p_hi.shape[1]
    return pl.pallas_call(
        _prep_kernel,
        grid=(n // tm,),
        in_specs=[pl.BlockSpec((tm, d), lambda i: (i, 0)),
                  pl.BlockSpec((1, d), lambda i: (0, 0)),
                  pl.BlockSpec((IDX_W_IN, d), lambda i: (0, 0)),
                  pl.BlockSpec((IDX_W_IN, d), lambda i: (0, 0)),
                  pl.BlockSpec((IDX_W_IN, co), lambda i: (0, 0)),
                  pl.BlockSpec((IDX_W_IN, co), lambda i: (0, 0))],
        out_specs=[pl.BlockSpec((tm, d), lambda i: (i, 0)),
                   pl.BlockSpec((tm, co), lambda i: (i, 0)),
                   pl.BlockSpec((8, tm), lambda i: (0, i))],
        out_shape=[jax.ShapeDtypeStruct((n, d), BF16),
                   jax.ShapeDtypeStruct((n, co), BF16),
                   jax.ShapeDtypeStruct((8, n), F32)],
        compiler_params=_cparams(("parallel",)),
        name="prep",
    )(x, g.reshape(1, d), whi, wlo, p_hi, p_lo)


def _transpose_cast_kernel(w_ref, o_ref):
    o_ref[...] = w_ref[...].T.astype(o_ref.dtype)


def _transpose_cast(w, name, t=512):
    d, n = w.shape
    return pl.pallas_call(
        _transpose_cast_kernel,
        grid=(n // t, d // t),
        in_specs=[pl.BlockSpec((t, t), lambda i, j: (j, i))],
        out_specs=pl.BlockSpec((t, t), lambda i, j: (i, j)),
        out_shape=jax.ShapeDtypeStruct((n, d), BF16),
        compiler_params=_cparams(("parallel", "parallel")),
        name=name,
    )(w)


def _mm_nn_kernel(a_ref, b_ref, o_ref):
    o_ref[...] = _dot(a_ref[...], b_ref[...]).astype(o_ref.dtype)


def _mm_nn(a, b, out_dtype, tm, tn, name):
    m, k = a.shape
    n = b.shape[1]
    return pl.pallas_call(
        _mm_nn_kernel,
        grid=(m // tm, n // tn),
        in_specs=[pl.BlockSpec((tm, k), lambda i, j: (i, 0)),
                  pl.BlockSpec((k, tn), lambda i, j: (0, j))],
        out_specs=pl.BlockSpec((tm, tn), lambda i, j: (i, j)),
        out_shape=jax.ShapeDtypeStruct((m, n), out_dtype),
        compiler_params=_cparams(("parallel", "parallel")),
        name=name,
    )(a, b)


def _proj_tok_kernel(blk_ref, a_ref, b_ref, s_ref, o_ref):
    del blk_ref
    o_ref[...] = (_dot_nt(a_ref[...], b_ref[...]) * s_ref[...]).astype(o_ref.dtype)


def _proj_tok(a, wt3, layer, row_blocks, colscale, tm, tn):
    m, k = a.shape
    n = len(row_blocks) * tn
    return pl.pallas_call(
        _proj_tok_kernel,
        grid_spec=pltpu.PrefetchScalarGridSpec(
            num_scalar_prefetch=1,
            grid=(m // tm, n // tn),
            in_specs=[pl.BlockSpec((tm, k), lambda i, j, blk: (i, 0)),
                      pl.BlockSpec((None, tn, k), lambda i, j, blk: (layer, blk[j], 0)),
                      pl.BlockSpec((1, tn), lambda i, j, blk: (0, j))],
            out_specs=pl.BlockSpec((tm, tn), lambda i, j, blk: (i, j))),
        out_shape=jax.ShapeDtypeStruct((m, n), BF16),
        compiler_params=_cparams(("parallel", "parallel")),
        name="proj_tok",
    )(jnp.asarray(row_blocks, jnp.int32), a, wt3, colscale)


def _proj_vt_kernel(blk_ref, a_ref, b_ref, o_ref):
    del blk_ref
    o_ref[...] = _dot_nt(a_ref[...], b_ref[...]).astype(o_ref.dtype)


def _proj_vt(wt3, layer, row_blocks, b, tm, tn):
    n, k = b.shape
    m = len(row_blocks) * tm
    return pl.pallas_call(
        _proj_vt_kernel,
        grid_spec=pltpu.PrefetchScalarGridSpec(
            num_scalar_prefetch=1,
            grid=(n // tn, m // tm),
            in_specs=[pl.BlockSpec((None, tm, k), lambda j, i, blk: (layer, blk[i], 0)),
                      pl.BlockSpec((tn, k), lambda j, i, blk: (j, 0))],
            out_specs=pl.BlockSpec((tm, tn), lambda j, i, blk: (i, j))),
        out_shape=jax.ShapeDtypeStruct((m, n), BF16),
        compiler_params=_cparams(("parallel", "parallel")),
        name="proj_vt",
    )(jnp.asarray(row_blocks, jnp.int32), wt3, b)


def _mm_nt_kernel(a_ref, b_ref, o_ref):
    o_ref[...] = _dot_nt(a_ref[...], b_ref[...]).astype(o_ref.dtype)


def _mm_nt(a, b, out_dtype, tm, tn, name):
    m, k = a.shape
    n = b.shape[0]
    return pl.pallas_call(
        _mm_nt_kernel,
        grid=(n // tn, m // tm),
        in_specs=[pl.BlockSpec((tm, k), lambda j, i: (i, 0)),
                  pl.BlockSpec((tn, k), lambda j, i: (j, 0))],
        out_specs=pl.BlockSpec((tm, tn), lambda j, i: (i, j)),
        out_shape=jax.ShapeDtypeStruct((m, n), out_dtype),
        compiler_params=_cparams(("parallel", "parallel")),
        name=name,
    )(a, b)


def _out_proj_kernel(a_ref, b_ref, r_ref, g_ref, o_ref, *, final_norm):
    y = r_ref[...] + _dot(a_ref[...], b_ref[...])
    o_ref[...] = _rms(y, g_ref[...]) if final_norm else y


def _out_proj(a, b, res, final_g, tm):
    m, k = a.shape
    n = b.shape[1]
    g = jnp.ones((1, n), F32) if final_g is None else final_g.reshape(1, n)
    return pl.pallas_call(
        functools.partial(_out_proj_kernel, final_norm=final_g is not None),
        grid=(m // tm,),
        in_specs=[pl.BlockSpec((tm, k), lambda i: (i, 0)),
                  pl.BlockSpec((k, n), lambda i: (0, 0)),
                  pl.BlockSpec((tm, n), lambda i: (i, 0)),
                  pl.BlockSpec((1, n), lambda i: (0, 0))],
        out_specs=pl.BlockSpec((tm, n), lambda i: (i, 0)),
        out_shape=jax.ShapeDtypeStruct((m, n), F32),
        compiler_params=_cparams(("parallel",)),
        name="out_proj",
    )(a, b, res, g)


def _merge_kernel(ya_ref, yc_ref, ys_ref, ym_ref, ra_ref, rc_ref, rs_ref, rm_ref, wb_ref, o_ref):
    ys = (ya_ref, yc_ref, ys_ref, ym_ref)
    rs = (ra_ref, rc_ref, rs_ref, rm_ref)
    acc = None
    for br in range(4):
        z = _dot(ys[br][...], wb_ref[br])
        term = jax.nn.sigmoid(rs[br][...].astype(F32)) * z
        acc = term if acc is None else acc + term
    o_ref[...] = acc.astype(o_ref.dtype)


def _merge(ys, gates, wb, tm, tn):
    n, w = ys[0].shape
    d = wb.shape[2]
    y_specs = [pl.BlockSpec((tm, w), lambda j, i: (i, 0)) for _ in range(4)]
    r_specs = [pl.BlockSpec((tm, tn), functools.partial(
        lambda j, i, base: (i, base + j), base=br * d // tn)) for br in range(4)]
    return pl.pallas_call(
        _merge_kernel,
        grid=(d // tn, n // tm),
        in_specs=y_specs + r_specs + [pl.BlockSpec((4, w, tn), lambda j, i: (0, 0, j))],
        out_specs=pl.BlockSpec((tm, tn), lambda j, i: (i, j)),
        out_shape=jax.ShapeDtypeStruct((n, d), BF16),
        compiler_params=_cparams(("parallel", "parallel")),
        name="merge",
    )(*ys, gates, gates, gates, gates, wb)


def _ceil_div_unroll(n):
    return (n + (ATT_UNROLL - 1)) >> (ATT_UNROLL.bit_length() - 1)


def _two_pass_attention(q_list, k_ref, vt_ref, sc_ref, acc_ref, bias_fn, trips, blk, own=None):
    heads = range(len(q_list))
    hd = HEAD_DIM
    span = ATT_UNROLL * blk
    max_trips = k_ref.shape[0] // span
    tq = q_list[0].shape[0]

    def run(n_trips):
        ms = []
        for h in heads:
            m = jnp.full((1, tq), _NEG, F32)
            for j in range(n_trips):
                s = _dot_nt(k_ref[j * span:(j + 1) * span, h * hd:(h + 1) * hd], q_list[h])
                for u in range(ATT_UNROLL):
                    n = j * ATT_UNROLL + u
                    su = s[u * blk:(u + 1) * blk, :] + bias_fn(h, n)
                    sc_ref[h, n * blk:(n + 1) * blk, :] = su
                    m = jnp.maximum(m, jnp.max(su, axis=0, keepdims=True))
            if own is not None:
                off, keep, own_ref = own
                s = _dot_nt(k_ref[pl.ds(off, blk), h * hd:(h + 1) * hd], q_list[h])
                s = jnp.where(keep, s, _NEG)
                own_ref[h] = s
                m = jnp.maximum(m, jnp.max(s, axis=0, keepdims=True))
            ms.append(m)
        ls = []
        for h in heads:
            l = jnp.zeros((1, tq), F32)
            acc = None
            for j in range(n_trips):
                p = jnp.exp2(sc_ref[h, j * span:(j + 1) * span, :] - ms[h])
                l = l + jnp.sum(p, axis=0, keepdims=True)
                pv = _dot(vt_ref[h * hd:(h + 1) * hd, j * span:(j + 1) * span], p.astype(BF16))
                acc = pv if acc is None else acc + pv
            if own is not None:
                off, keep, own_ref = own
                p = jnp.exp2(own_ref[h] - ms[h])
                l = l + jnp.sum(p, axis=0, keepdims=True)
                pv = _dot(vt_ref[h * hd:(h + 1) * hd, pl.ds(off, blk)], p.astype(BF16))
                acc = pv if acc is None else acc + pv
            acc_ref[h] = jnp.zeros((hd, tq), F32) if acc is None else acc
            ls.append(l)
        return tuple(ls)

    return lax.switch(trips, [functools.partial(run, t) for t in range(max_trips + 1)])


def _moba_kernel(q_ref, k_ref, vt_ref, g_ref, o_ref, kmean_ref, selb_ref, sc_ref, acc_ref, own_ref,
                 *, nblk):
    i = pl.program_id(2)
    blk = MOBA_BLOCK
    hd = HEAD_DIM
    heads = range(HEADS_PER_STEP)

    @pl.when(i == 0)
    def _():
        for h in heads:
            kf = k_ref[:, h * hd:(h + 1) * hd].astype(F32).reshape(nblk, blk, hd)
            kmean_ref[h] = jnp.mean(kf, axis=1)

    q_list = []
    for h in heads:
        q = q_ref[:, h * hd:(h + 1) * hd]
        q_list.append(q)
        k1, k2, k3 = _split3(kmean_ref[h])
        bs = _dot_nt(k1, q) + _dot_nt(k2, q) + _dot_nt(k3, q)
        n_iota = lax.broadcasted_iota(jnp.int32, bs.shape, 0)
        past = n_iota < i
        left = jnp.where(past, bs, -jnp.inf)
        bias = jnp.full(bs.shape, _NEG, F32)
        for _ in range(MOBA_TOPK):
            top = jnp.max(left, axis=0, keepdims=True)
            first = jnp.min(jnp.where(left == top, n_iota, nblk), axis=0, keepdims=True)
            hit = n_iota == first
            bias = jnp.where(hit & past, _ZERO, bias)
            left = jnp.where(hit, -jnp.inf, left)
        selb_ref[h] = bias

    kpos = lax.broadcasted_iota(jnp.int32, (blk, blk), 0)
    qpos = lax.broadcasted_iota(jnp.int32, (blk, blk), 1)
    ls = _two_pass_attention(
        q_list, k_ref, vt_ref, sc_ref, acc_ref, lambda h, n: selb_ref[h, pl.ds(n, 1), :],
        _ceil_div_unroll(i), blk,
        own=(pl.multiple_of(i * blk, blk), kpos <= qpos, own_ref))
    for h in heads:
        o = (acc_ref[h] / ls[h]).T
        g = g_ref[:, h * hd:(h + 1) * hd].astype(F32)
        o_ref[:, h * hd:(h + 1) * hd] = (o * _silu(g)).astype(o_ref.dtype)


def _moba(act, act_t, batch, seq, q_col, k_col, g_col, v_row):
    nblk = seq // MOBA_BLOCK
    n = act.shape[0]
    blk = MOBA_BLOCK
    hp = HEADS_PER_STEP
    w = hp * HEAD_DIM
    q_cb, k_cb, g_cb, v_rb = q_col // w, k_col // w, g_col // w, v_row // w
    return pl.pallas_call(
        functools.partial(_moba_kernel, nblk=nblk),
        grid=(batch, MOBA_HEADS // hp, nblk),
        in_specs=[pl.BlockSpec((blk, w), lambda b, h, i: (b * nblk + i, q_cb + h)),
                  pl.BlockSpec((seq, w), lambda b, h, i: (b, k_cb + h)),
                  pl.BlockSpec((w, seq), lambda b, h, i: (v_rb + h, b)),
                  pl.BlockSpec((blk, w), lambda b, h, i: (b * nblk + i, g_cb + h))],
        out_specs=pl.BlockSpec((blk, w), lambda b, h, i: (b * nblk + i, h)),
        out_shape=jax.ShapeDtypeStruct((n, BRANCH_W), BF16),
        scratch_shapes=[pltpu.VMEM((hp, nblk, HEAD_DIM), F32),
                        pltpu.VMEM((hp, nblk, blk), F32),
                        pltpu.VMEM((hp, seq, blk), F32),
                        pltpu.VMEM((hp, HEAD_DIM, blk), F32),
                        pltpu.VMEM((hp, blk, blk), F32)],
        compiler_params=_cparams(("parallel", "parallel", "arbitrary")),
        name="moba",
    )(act, act, act_t, act)


def _conv_kernel(cb_ref, cc_ref, ch_ref, cg_ref, w_ref, bias_ref, o_ref):
    u = cc_ref[...].astype(F32) * ch_ref[...].astype(F32)
    t = lax.broadcasted_iota(jnp.int32, u.shape, 0)
    u1 = jnp.where(t >= 1, pltpu.roll(u, 1, 0), 0.0)
    u2 = jnp.where(t >= 2, pltpu.roll(u, 2, 0), 0.0)
    w = w_ref[...]
    conv = u2 * w[0:1, :] + u1 * w[1:2, :] + u * w[2:3, :] + bias_ref[...]
    o_ref[...] = (cb_ref[...].astype(F32) * conv * _silu(cg_ref[...].astype(F32))).astype(o_ref.dtype)


def _conv(act, conv_w, conv_b, batch, seq, cb_cb, cc_cb, ch_cb, cg_cb):
    n = act.shape[0]
    ncb = BRANCH_W // LANE
    kw = conv_w.shape[0]
    wpad = jnp.zeros((8, BRANCH_W), F32).at[:kw].set(conv_w)

    def spec(base):
        return pl.BlockSpec((seq, LANE), lambda b, c: (b, base + c))

    return pl.pallas_call(
        _conv_kernel,
        grid=(batch, ncb),
        in_specs=[spec(cb_cb), spec(cc_cb), spec(ch_cb), spec(cg_cb),
                  pl.BlockSpec((8, LANE), lambda b, c: (0, c)),
                  pl.BlockSpec((1, LANE), lambda b, c: (0, c))],
        out_specs=pl.BlockSpec((seq, LANE), lambda b, c: (b, c)),
        out_shape=jax.ShapeDtypeStruct((n, BRANCH_W), BF16),
        compiler_params=_cparams(("parallel", "parallel")),
        name="conv",
    )(act, act, act, act, wpad, conv_b.reshape(1, BRANCH_W))


def _dsa_kernel(qc_ref, kc_ref, wt_ref, q_ref, k_ref, vt_ref, g_ref, o_ref, st_ref, tri_ref, sc_ref,
                acc_ref, *, topk, tq):
    i = pl.program_id(1)
    nch = i + 1
    npair = (nch + 1) >> 1
    trips = _ceil_div_unroll(nch)
    idx_scale = (IDX_DIM ** -0.5) * (IDX_HEADS ** -0.5)
    kf = float(topk)
    ninf = np.float32(-np.inf)

    def chunk(c):
        return pl.ds(pl.multiple_of(c * tq, tq), tq)

    def pair(c):
        return pl.ds(pl.multiple_of(c * (2 * tq), 2 * tq), 2 * tq)

    def fold8(x, op):
        return op(x.reshape(x.shape[0] // 8, 8, tq), axis=0)

    @pl.when(pl.program_id(2) == 0)
    def _select():
        def score_body(c, carry):
            mx, mn = carry
            for u in range(2):
                cc = 2 * c + u
                kc = kc_ref[chunk(cc), :]
                sc = jnp.zeros((tq, tq), F32)
                for j in range(IDX_HEADS):
                    lg = _dot_nt(kc, qc_ref[:, j * IDX_SLOT:(j + 1) * IDX_SLOT])
                    sc = sc + (wt_ref[j:j + 1, :] * idx_scale) * jnp.maximum(lg, 0.0)
                kpos = cc * tq + lax.broadcasted_iota(jnp.int32, sc.shape, 0)
                qpos = i * tq + lax.broadcasted_iota(jnp.int32, sc.shape, 1)
                causal = kpos <= qpos
                st_ref[chunk(cc), :] = jnp.where(causal, sc, ninf)
                mx = jnp.maximum(mx, fold8(jnp.where(causal, sc, ninf), jnp.max))
                mn = jnp.minimum(mn, fold8(jnp.where(causal, sc, -ninf), jnp.min))
            return mx, mn
        mx8, mn8 = lax.fori_loop(0, npair, score_body,
                                 (jnp.full((8, tq), ninf, F32), jnp.full((8, tq), -ninf, F32)))
        smax = jnp.max(mx8, axis=0, keepdims=True)
        smin = jnp.min(mn8, axis=0, keepdims=True)

        qrow = i * tq + lax.broadcasted_iota(jnp.int32, (1, tq), 1)
        short = qrow + 1 < topk
        lo0 = jnp.where(short, _ZERO, smin)
        hi0 = jnp.where(short, _ONE, smax + jnp.maximum(jnp.abs(smax), np.float32(1e-30))
                        * np.float32(2.0 ** -10))

        def search(n_pairs):
            rows = [slice(c * tq, (c + 1) * tq) for c in range(2 * n_pairs)]

            def count(pred):
                acc = jnp.zeros((8, tq), F32)
                for r in rows:
                    acc = acc + fold8(_ind(pred(st_ref[r, :])), jnp.sum)
                return jnp.sum(acc, axis=0, keepdims=True)

            def bis_body(_, carry):
                lo, hi = carry
                mid = lo + (hi - lo) * np.float32(0.5)
                ge = count(lambda s: s >= mid) >= kf
                return jnp.where(ge, mid, lo), jnp.where(ge, hi, mid)

            _, hi_f0 = lax.fori_loop(0, BISECT_ITERS, bis_body, (lo0, hi0))

            def walk_cond(carry):
                return jnp.min(carry[3]) < 0.5

            def walk_body(carry):
                hi_f, thr, cnt, done = carry
                mx = jnp.full((8, tq), ninf, F32)
                for r in rows:
                    s = st_ref[r, :]
                    mx = jnp.maximum(mx, fold8(jnp.where(s < hi_f, s, ninf), jnp.max))
                v = jnp.max(mx, axis=0, keepdims=True)
                c_v = count(lambda s: s >= v)
                is_done = done > 0.5
                return (jnp.where(is_done, hi_f, v), jnp.where(is_done, thr, v),
                        jnp.where(is_done, cnt, c_v), _ind(is_done | (c_v >= kf)))

            zero_row = jnp.zeros((1, tq), F32)
            _, thr, n_ge, _ = lax.while_loop(
                walk_cond, walk_body,
                (hi_f0, jnp.full((1, tq), ninf, F32), zero_row + np.float32(len(rows) * tq),
                 _ind(short)))

            n_gt = count(lambda s: s > thr)
            need = kf - n_gt
            n_eq = n_ge - n_gt

            def all_ties():
                for r in rows:
                    st_ref[r, :] = jnp.where(st_ref[r, :] >= thr, _ZERO, _NEG)

            def bias_some_ties(cc, carry):
                s = st_ref[chunk(cc), :]
                eq = s == thr
                eqf = _ind(eq)
                before = _dot(tri_ref[...], eqf.astype(BF16)) + carry
                kpos = cc * tq + lax.broadcasted_iota(jnp.int32, s.shape, 0)
                qpos = i * tq + lax.broadcasted_iota(jnp.int32, s.shape, 1)
                selected = ((s > thr) | (eq & (before < need))) & (kpos <= qpos)
                st_ref[chunk(cc), :] = jnp.where(selected, _ZERO, _NEG)
                return carry + jnp.sum(eqf, axis=0, keepdims=True)

            def some_ties():
                r = lax.broadcasted_iota(jnp.int32, (tq, tq), 0)
                cidx = lax.broadcasted_iota(jnp.int32, (tq, tq), 1)
                tri_ref[...] = _ind(cidx < r).astype(BF16)
                lax.fori_loop(0, len(rows), bias_some_ties, zero_row)

            simple = jnp.logical_and(jnp.max(n_eq - need) <= 0.0, jnp.min(thr) > ninf)
            lax.cond(simple, all_ties, some_ties)

        max_pairs = st_ref.shape[0] // (2 * tq)
        lax.switch(npair - 1, [functools.partial(search, n) for n in range(1, max_pairs + 1)])

        def pad_body(c, _):
            st_ref[chunk(c), :] = jnp.full((tq, tq), _NEG, F32)
            return 0
        lax.fori_loop(2 * npair, trips * ATT_UNROLL, pad_body, 0)

    hd = HEAD_DIM
    heads = range(HEADS_PER_STEP)
    q_list = [q_ref[:, h * hd:(h + 1) * hd] for h in heads]
    ls = _two_pass_attention(
        q_list, k_ref, vt_ref, sc_ref, acc_ref, lambda h, c: st_ref[c * tq:(c + 1) * tq, :], trips, tq)
    for h in heads:
        o = (acc_ref[h] / ls[h]).T
        g = g_ref[:, h * hd:(h + 1) * hd].astype(F32)
        o_ref[:, h * hd:(h + 1) * hd] = (o * _silu(g)).astype(o_ref.dtype)


def _dsa(act, act_t, cat, wt, batch, seq, q_col, k_col, g_col, v_row, tq=256):
    n = act.shape[0]
    nq = seq // tq
    topk = min(DSA_TOPK_MAX, seq // 4)
    n_qcat = IDX_HEADS * IDX_SLOT
    hp = HEADS_PER_STEP
    w = hp * HEAD_DIM
    q_cb, k_cb, g_cb, v_rb = q_col // w, k_col // w, g_col // w, v_row // w
    return pl.pallas_call(
        functools.partial(_dsa_kernel, topk=topk, tq=tq),
        grid=(batch, nq, DSA_HEADS // hp),
        in_specs=[pl.BlockSpec((tq, n_qcat), lambda b, i, h: (b * nq + i, 0)),
                  pl.BlockSpec((seq, IDX_SLOT), lambda b, i, h: (b, n_qcat // IDX_SLOT)),
                  pl.BlockSpec((8, tq), lambda b, i, h: (0, b * nq + i)),
                  pl.BlockSpec((tq, w), lambda b, i, h: (b * nq + i, q_cb + h)),
                  pl.BlockSpec((seq, w), lambda b, i, h: (b, k_cb + h)),
                  pl.BlockSpec((w, seq), lambda b, i, h: (v_rb + h, b)),
                  pl.BlockSpec((tq, w), lambda b, i, h: (b * nq + i, g_cb + h))],
        out_specs=pl.BlockSpec((tq, w), lambda b, i, h: (b * nq + i, h)),
        out_shape=jax.ShapeDtypeStruct((n, BRANCH_W), BF16),
        scratch_shapes=[pltpu.VMEM((seq, tq), F32),
                        pltpu.VMEM((tq, tq), BF16),
                        pltpu.VMEM((hp, seq, tq), F32),
                        pltpu.VMEM((hp, HEAD_DIM, tq), F32)],
        compiler_params=_cparams(("parallel", "arbitrary", "arbitrary")),
        name="dsa",
    )(cat, cat, wt, act, act, act_t, act)


def _mem_kernel(q_ref, mk_ref, mvt_ref, g_ref, o_ref):
    hd = MEM_HEAD_DIM
    for h in range(MEM_HEADS):
        cols = slice(h * hd, (h + 1) * hd)
        s = _dot_nt(mk_ref[:, cols], q_ref[:, cols])
        m = jnp.max(s, axis=0, keepdims=True)
        p = jnp.exp2(s - m)
        l = jnp.sum(p, axis=0, keepdims=True)
        o = (_dot(mvt_ref[cols, :], p.astype(BF16)) / l).T
        o_ref[:, cols] = (o * _silu(g_ref[:, cols].astype(F32))).astype(o_ref.dtype)


def _mem_attn(act, mk, mvt, batch, seq, mem_len, q_col, g_col, tq=512):
    n = act.shape[0]
    nq = seq // tq
    w = BRANCH_W
    return pl.pallas_call(
        _mem_kernel,
        grid=(batch, nq),
        in_specs=[pl.BlockSpec((tq, w), lambda b, i: (b * nq + i, q_col // w)),
                  pl.BlockSpec((mem_len, w), lambda b, i: (b, 0)),
                  pl.BlockSpec((w, mem_len), lambda b, i: (0, b)),
                  pl.BlockSpec((tq, w), lambda b, i: (b * nq + i, g_col // w))],
        out_specs=pl.BlockSpec((tq, w), lambda b, i: (b * nq + i, 0)),
        out_shape=jax.ShapeDtypeStruct((n, BRANCH_W), BF16),
        compiler_params=_cparams(("parallel", "parallel")),
        name="mem_attn",
    )(act, mk, mvt, act)


O_IDX = 14 * BRANCH_W
O_MERGE = O_IDX + IDX_HEADS * IDX_DIM + IDX_DIM + IDX_HEADS


def _input_weights(w_in3):
    wt3 = jnp.swapaxes(w_in3, 1, 2).astype(BF16)
    idx3 = jnp.swapaxes(w_in3[:, :, O_IDX:O_MERGE], 1, 2)
    idx3 = jnp.pad(idx3, ((0, 0), (0, IDX_W_IN - idx3.shape[1]), (0, 0)))
    return wt3, idx3


def _layer_weights(layer, wt3, idx3, w_mem_kv3, w_branch3, w_out3):
    bw = BRANCH_W
    w_idx_hi, w_idx_lo = _split2(idx3[layer])
    return dict(w_merge=wt3[layer, O_MERGE:], w_idx_hi=w_idx_hi, w_idx_lo=w_idx_lo,
                w_mk=w_mem_kv3[layer][:, :bw].astype(BF16),
                w_mvt=_transpose_cast(w_mem_kv3[layer][:, bw:], "w_mvt"),
                w_branch=w_branch3[layer].astype(BF16), w_out=w_out3[layer].astype(BF16))


def _layer(layer, x2, mem2, batch, seq, mem_len, ln_g, conv_w, conv_b, mem_ln_g, wt3, w, final_g):
    nb = BRANCH_W // LANE
    bw = BRANCH_W
    (A_Q, A_K, A_G, C_B, C_C, C_H, C_G, S_Q, S_K, S_G, M_Q, M_G) = range(12)
    tok_groups = (0, 1, 3, 4, 5, 6, 7, 8, 9, 11, 12, 13)
    val_groups = (2, 10)

    log2e = float(np.log2(np.e))
    colscale = np.ones((1, len(tok_groups) * bw), np.float32)
    for col, hdim in ((A_Q, HEAD_DIM), (S_Q, HEAD_DIM), (M_Q, MEM_HEAD_DIM)):
        colscale[:, col * bw:(col + 1) * bw] = hdim ** -0.5 * log2e

    xn, cat, wt = _prep(x2, ln_g, w["w_idx_hi"], w["w_idx_lo"])
    act = _proj_tok(xn, wt3, layer, tok_groups, jnp.asarray(colscale), tm=1024, tn=bw)
    act_r = _mm_nt(xn, w["w_merge"], BF16, tm=1024, tn=1024, name="proj_merge")
    act_t = _proj_vt(wt3, layer, val_groups, xn, tm=bw, tn=1024)

    y_a = _moba(act, act_t, batch, seq, A_Q * bw, A_K * bw, A_G * bw, 0)
    y_c = _conv(act, conv_w, conv_b, batch, seq, C_B * nb, C_C * nb, C_H * nb, C_G * nb)
    y_s = _dsa(act, act_t, cat, wt, batch, seq, S_Q * bw, S_K * bw, S_G * bw, bw)

    mem_n = _rmsnorm(mem2, mem_ln_g, BF16)
    mk = _mm_nn(mem_n, w["w_mk"], BF16, tm=mem2.shape[0], tn=512, name="mem_k")
    mvt = _mm_nt(w["w_mvt"], mem_n, BF16, tm=512, tn=mem2.shape[0], name="mem_vt")
    y_m = _mem_attn(act, mk, mvt, batch, seq, mem_len, M_Q * bw, M_G * bw)

    merged = _merge((y_a, y_c, y_s, y_m), act_r, w["w_branch"], tm=512, tn=1024)
    return _out_proj(merged, w["w_out"], x2, final_g, tm=512)


def kernel(x, mem, ln_g, w_in, conv_w, conv_b, mem_ln_g, w_mem_kv, w_branch, w_out, final_g):
    batch, seq, d = x.shape
    mem_len = mem.shape[1]
    x2 = x.reshape(batch * seq, d)
    mem2 = mem.reshape(batch * mem_len, d)
    wt3, idx3 = _input_weights(w_in)
    depth = ln_g.shape[0]
    for layer in range(depth):
        w = _layer_weights(layer, wt3, idx3, w_mem_kv, w_branch, w_out)
        x2 = _layer(layer, x2, mem2, batch, seq, mem_len, ln_g[layer], conv_w[layer],
                    conv_b[layer], mem_ln_g[layer], wt3, w,
                    final_g if layer == depth - 1 else None)
    return x2.reshape(batch, seq, d)
```

```python
import functools

import numpy as np
import jax
import jax.numpy as jnp
from jax import lax
from jax.experimental import pallas as pl
from jax.experimental.pallas import tpu as pltpu

EPS = 1e-6
BRANCH_W = 1024
MOBA_HEADS = 8
MOBA_BLOCK = 256
MOBA_TOPK = 3
DSA_HEADS = 8
DSA_TOPK_MAX = 256
IDX_HEADS = 4
IDX_DIM = 64
MEM_HEADS = 4
HEAD_DIM = 128
MEM_HEAD_DIM = 256
IDX_SLOT = 256
NEG = -1e30
ATT_UNROLL = 2
HEADS_PER_STEP = 4
BISECT_ITERS = 18
LANE = 128
VMEM_LIMIT = 56 * 1024 * 1024

F32 = jnp.float32
BF16 = jnp.bfloat16
_ONE = np.float32(1.0)
_ZERO = np.float32(0.0)
_NEG = np.float32(NEG)


def _cparams(sem):
    return pltpu.CompilerParams(dimension_semantics=sem, vmem_limit_bytes=VMEM_LIMIT)


def _dot(a, b):
    return jnp.dot(a, b, preferred_element_type=F32)


def _dot_nt(a, b):
    return lax.dot_general(a, b, (((1,), (1,)), ((), ())), preferred_element_type=F32)


def _split2(v):
    hi = v.astype(BF16)
    lo = (v - hi.astype(F32)).astype(BF16)
    return hi, lo


def _split3(v):
    h1 = v.astype(BF16)
    r1 = v - h1.astype(F32)
    h2 = r1.astype(BF16)
    h3 = (r1 - h2.astype(F32)).astype(BF16)
    return h1, h2, h3


def _ind(cond):
    return jnp.where(cond, _ONE, _ZERO)


def _silu(g):
    return g * jax.nn.sigmoid(g)


def _rms(x, g):
    var = jnp.mean(x * x, axis=-1, keepdims=True)
    return (x * lax.rsqrt(var + EPS)) * g


def _rmsnorm_kernel(x_ref, g_ref, o_ref):
    o_ref[...] = _rms(x_ref[...], g_ref[...]).astype(o_ref.dtype)


def _rmsnorm(x, g, out_dtype, tm=256):
    n, d = x.shape
    return pl.pallas_call(
        _rmsnorm_kernel,
        grid=(n // tm,),
        in_specs=[pl.BlockSpec((tm, d), lambda i: (i, 0)),
                  pl.BlockSpec((1, d), lambda i: (0, 0))],
        out_specs=pl.BlockSpec((tm, d), lambda i: (i, 0)),
        out_shape=jax.ShapeDtypeStruct((n, d), out_dtype),
        compiler_params=_cparams(("parallel",)),
        name="rmsnorm",
    )(x, g.reshape(1, d))


def _prep_kernel(x_ref, g_ref, whi_ref, wlo_ref, phi_ref, plo_ref, xn_ref, cat_ref, wt_ref):
    xn = _rms(x_ref[...], g_ref[...])
    hi, lo = _split2(xn)
    xn_ref[...] = hi
    acc = _dot_nt(hi, whi_ref[...]) + _dot_nt(hi, wlo_ref[...]) + _dot_nt(lo, whi_ref[...])
    vh, vl = _split2(acc)
    cat_ref[...] = (_dot(vh, phi_ref[...]) + _dot(vl, plo_ref[...])).astype(BF16)
    gate_row = IDX_DIM
    wt_ref[...] = acc[:, IDX_W_IN - LANE:].T[gate_row:gate_row + 8, :]


IDX_W_IN = 384


def _idx_placement():
    n_out = (IDX_HEADS + 1) * IDX_SLOT
    p_hi = np.zeros((IDX_W_IN, n_out), np.float32)
    p_lo = np.zeros((IDX_W_IN, n_out), np.float32)
    e = np.arange(IDX_DIM)
    for j in range(IDX_HEADS):
        src, dst = j * IDX_DIM + e, j * IDX_SLOT + e
        p_hi[src, dst] = 1
        p_hi[src, dst + IDX_DIM] = 1
        p_lo[src, dst + 2 * IDX_DIM] = 1
    src, dst = IDX_HEADS * IDX_DIM + e, IDX_HEADS * IDX_SLOT + e
    p_hi[src, dst] = 1
    p_lo[src, dst + IDX_DIM] = 1
    p_hi[src, dst + 2 * IDX_DIM] = 1
    return jnp.asarray(p_hi, BF16), jnp.asarray(p_lo, BF16)


def _prep(x, g, whi, wlo, tm=512):
    n, d = x.shape
    p_hi, p_lo = _idx_placement()
    co = p_hi.shape[1]
    return pl.pallas_call(
        _prep_kernel,
        grid=(n // tm,),
        in_specs=[pl.BlockSpec((tm, d), lambda i: (i, 0)),
                  pl.BlockSpec((1, d), lambda i: (0, 0)),
                  pl.BlockSpec((IDX_W_IN, d), lambda i: (0, 0)),
                  pl.BlockSpec((IDX_W_IN, d), lambda i: (0, 0)),
                  pl.BlockSpec((IDX_W_IN, co), lambda i: (0, 0)),
                  pl.BlockSpec((IDX_W_IN, co), lambda i: (0, 0))],
        out_specs=[pl.BlockSpec((tm, d), lambda i: (i, 0)),
                   pl.BlockSpec((tm, co), lambda i: (i, 0)),
                   pl.BlockSpec((8, tm), lambda i: (0, i))],
        out_shape=[jax.ShapeDtypeStruct((n, d), BF16),
                   jax.ShapeDtypeStruct((n, co), BF16),
                   jax.ShapeDtypeStruct((8, n), F32)],
        compiler_params=_cparams(("parallel",)),
        name="prep",
    )(x, g.reshape(1, d), whi, wlo, p_hi, p_lo)


def _transpose_cast_kernel(w_ref, o_ref):
    o_ref[...] = w_ref[...].T.astype(o_ref.dtype)


def _transpose_cast(w, name, t=512):
    d, n = w.shape
    return pl.pallas_call(
        _transpose_cast_kernel,
        grid=(n // t, d // t),
        in_specs=[pl.BlockSpec((t, t), lambda i, j: (j, i))],
        out_specs=pl.BlockSpec((t, t), lambda i, j: (i, j)),
        out_shape=jax.ShapeDtypeStruct((n, d), BF16),
        compiler_params=_cparams(("parallel", "parallel")),
        name=name,
    )(w)


def _mm_nn_kernel(a_ref, b_ref, o_ref):
    o_ref[...] = _dot(a_ref[...], b_ref[...]).astype(o_ref.dtype)


def _mm_nn(a, b, out_dtype, tm, tn, name):
    m, k = a.shape
    n = b.shape[1]
    return pl.pallas_call(
        _mm_nn_kernel,
        grid=(m // tm, n // tn),
        in_specs=[pl.BlockSpec((tm, k), lambda i, j: (i, 0)),
                  pl.BlockSpec((k, tn), lambda i, j: (0, j))],
        out_specs=pl.BlockSpec((tm, tn), lambda i, j: (i, j)),
        out_shape=jax.ShapeDtypeStruct((m, n), out_dtype),
        compiler_params=_cparams(("parallel", "parallel")),
        name=name,
    )(a, b)


def _proj_tok_kernel(blk_ref, a_ref, b_ref, s_ref, o_ref):
    del blk_ref
    o_ref[...] = (_dot_nt(a_ref[...], b_ref[...]) * s_ref[...]).astype(o_ref.dtype)


def _proj_tok(a, wt3, layer, row_blocks, colscale, tm, tn):
    m, k = a.shape
    n = len(row_blocks) * tn
    return pl.pallas_call(
        _proj_tok_kernel,
        grid_spec=pltpu.PrefetchScalarGridSpec(
            num_scalar_prefetch=1,
            grid=(m // tm, n // tn),
            in_specs=[pl.BlockSpec((tm, k), lambda i, j, blk: (i, 0)),
                      pl.BlockSpec((None, tn, k), lambda i, j, blk: (layer, blk[j], 0)),
                      pl.BlockSpec((1, tn), lambda i, j, blk: (0, j))],
            out_specs=pl.BlockSpec((tm, tn), lambda i, j, blk: (i, j))),
        out_shape=jax.ShapeDtypeStruct((m, n), BF16),
        compiler_params=_cparams(("parallel", "parallel")),
        name="proj_tok",
    )(jnp.asarray(row_blocks, jnp.int32), a, wt3, colscale)


def _proj_vt_kernel(blk_ref, a_ref, b_ref, o_ref):
    del blk_ref
    o_ref[...] = _dot_nt(a_ref[...], b_ref[...]).astype(o_ref.dtype)


def _proj_vt(wt3, layer, row_blocks, b, tm, tn):
    n, k = b.shape
    m = len(row_blocks) * tm
    return pl.pallas_call(
        _proj_vt_kernel,
        grid_spec=pltpu.PrefetchScalarGridSpec(
            num_scalar_prefetch=1,
            grid=(n // tn, m // tm),
            in_specs=[pl.BlockSpec((None, tm, k), lambda j, i, blk: (layer, blk[i], 0)),
                      pl.BlockSpec((tn, k), lambda j, i, blk: (j, 0))],
            out_specs=pl.BlockSpec((tm, tn), lambda j, i, blk: (i, j))),
        out_shape=jax.ShapeDtypeStruct((m, n), BF16),
        compiler_params=_cparams(("parallel", "parallel")),
        name="proj_vt",
    )(jnp.asarray(row_blocks, jnp.int32), wt3, b)


def _mm_nt_kernel(a_ref, b_ref, o_ref):
    o_ref[...] = _dot_nt(a_ref[...], b_ref[...]).astype(o_ref.dtype)


def _mm_nt(a, b, out_dtype, tm, tn, name):
    m, k = a.shape
    n = b.shape[0]
    return pl.pallas_call(
        _mm_nt_kernel,
        grid=(n // tn, m // tm),
        in_specs=[pl.BlockSpec((tm, k), lambda j, i: (i, 0)),
                  pl.BlockSpec((tn, k), lambda j, i: (j, 0))],
        out_specs=pl.BlockSpec((tm, tn), lambda j, i: (i, j)),
        out_shape=jax.ShapeDtypeStruct((m, n), out_dtype),
        compiler_params=_cparams(("parallel", "parallel")),
        name=name,
    )(a, b)


def _out_proj_kernel(a_ref, b_ref, r_ref, g_ref, o_ref, *, final_norm):
    y = r_ref[...] + _dot(a_ref[...], b_ref[...])
    o_ref[...] = _rms(y, g_ref[...]) if final_norm else y


def _out_proj(a, b, res, final_g, tm):
    m, k = a.shape
    n = b.shape[1]
    g = jnp.ones((1, n), F32) if final_g is None else final_g.reshape(1, n)
    return pl.pallas_call(
        functools.partial(_out_proj_kernel, final_norm=final_g is not None),
        grid=(m // tm,),
        in_specs=[pl.BlockSpec((tm, k), lambda i: (i, 0)),
                  pl.BlockSpec((k, n), lambda i: (0, 0)),
                  pl.BlockSpec((tm, n), lambda i: (i, 0)),
                  pl.BlockSpec((1, n), lambda i: (0, 0))],
        out_specs=pl.BlockSpec((tm, n), lambda i: (i, 0)),
        out_shape=jax.ShapeDtypeStruct((m, n), F32),
        compiler_params=_cparams(("parallel",)),
        name="out_proj",
    )(a, b, res, g)


def _merge_kernel(ya_ref, yc_ref, ys_ref, ym_ref, ra_ref, rc_ref, rs_ref, rm_ref, wb_ref, o_ref):
    ys = (ya_ref, yc_ref, ys_ref, ym_ref)
    rs = (ra_ref, rc_ref, rs_ref, rm_ref)
    acc = None
    for br in range(4):
        z = _dot(ys[br][...], wb_ref[br])
        term = jax.nn.sigmoid(rs[br][...].astype(F32)) * z
        acc = term if acc is None else acc + term
    o_ref[...] = acc.astype(o_ref.dtype)


def _merge(ys, gates, wb, tm, tn):
    n, w = ys[0].shape
    d = wb.shape[2]
    y_specs = [pl.BlockSpec((tm, w), lambda j, i: (i, 0)) for _ in range(4)]
    r_specs = [pl.BlockSpec((tm, tn), functools.partial(
        lambda j, i, base: (i, base + j), base=br * d // tn)) for br in range(4)]
    return pl.pallas_call(
        _merge_kernel,
        grid=(d // tn, n // tm),
        in_specs=y_specs + r_specs + [pl.BlockSpec((4, w, tn), lambda j, i: (0, 0, j))],
        out_specs=pl.BlockSpec((tm, tn), lambda j, i: (i, j)),
        out_shape=jax.ShapeDtypeStruct((n, d), BF16),
        compiler_params=_cparams(("parallel", "parallel")),
        name="merge",
    )(*ys, gates, gates, gates, gates, wb)


def _ceil_div_unroll(n):
    return (n + (ATT_UNROLL - 1)) >> (ATT_UNROLL.bit_length() - 1)


def _two_pass_attention(q_list, k_ref, vt_ref, sc_ref, acc_ref, bias_fn, trips, blk, own=None):
    heads = range(len(q_list))
    hd = HEAD_DIM
    span = ATT_UNROLL * blk
    max_trips = k_ref.shape[0] // span
    tq = q_list[0].shape[0]

    def pass1(h, n_trips):
        m = jnp.full((1, tq), _NEG, F32)
        for j in range(n_trips):
            s = _dot_nt(k_ref[j * span:(j + 1) * span, h * hd:(h + 1) * hd], q_list[h])
            for u in range(ATT_UNROLL):
                n = j * ATT_UNROLL + u
                su = s[u * blk:(u + 1) * blk, :] + bias_fn(h, n)
                sc_ref[h, n * blk:(n + 1) * blk, :] = su
                m = jnp.maximum(m, jnp.max(su, axis=0, keepdims=True))
        if own is not None:
            off, keep, own_ref = own
            s = _dot_nt(k_ref[pl.ds(off, blk), h * hd:(h + 1) * hd], q_list[h])
            s = jnp.where(keep, s, _NEG)
            own_ref[h] = s
            m = jnp.maximum(m, jnp.max(s, axis=0, keepdims=True))
        return m

    def pass2(h, n_trips, m):
        l = jnp.zeros((1, tq), F32)
        acc = None
        for j in range(n_trips):
            p = jnp.exp2(sc_ref[h, j * span:(j + 1) * span, :] - m)
            l = l + jnp.sum(p, axis=0, keepdims=True)
            pv = _dot(vt_ref[h * hd:(h + 1) * hd, j * span:(j + 1) * span], p.astype(BF16))
            acc = pv if acc is None else acc + pv
        if own is not None:
            off, keep, own_ref = own
            p = jnp.exp2(own_ref[h] - m)
            l = l + jnp.sum(p, axis=0, keepdims=True)
            pv = _dot(vt_ref[h * hd:(h + 1) * hd, pl.ds(off, blk)], p.astype(BF16))
            acc = pv if acc is None else acc + pv
        acc_ref[h] = jnp.zeros((hd, tq), F32) if acc is None else acc
        return l

    def run(n_trips):
        ms, ls = [], []
        for h in heads:
            ms.append(pass1(h, n_trips))
            if h > 0:
                ls.append(pass2(h - 1, n_trips, ms[h - 1]))
        ls.append(pass2(len(q_list) - 1, n_trips, ms[-1]))
        return tuple(ls)

    return lax.switch(trips, [functools.partial(run, t) for t in range(max_trips + 1)])


def _moba_kernel(q_ref, k_ref, vt_ref, g_ref, o_ref, kmean_ref, selb_ref, sc_ref, acc_ref, own_ref,
                 *, nblk):
    i = pl.program_id(2)
    blk = MOBA_BLOCK
    hd = HEAD_DIM
    heads = range(HEADS_PER_STEP)

    @pl.when(i == 0)
    def _():
        for h in heads:
            kf = k_ref[:, h * hd:(h + 1) * hd].astype(F32).reshape(nblk, blk, hd)
            kmean_ref[h] = jnp.mean(kf, axis=1)

    q_list = []
    for h in heads:
        q = q_ref[:, h * hd:(h + 1) * hd]
        q_list.append(q)
        k1, k2, k3 = _split3(kmean_ref[h])
        bs = _dot_nt(k1, q) + _dot_nt(k2, q) + _dot_nt(k3, q)
        n_iota = lax.broadcasted_iota(jnp.int32, bs.shape, 0)
        past = n_iota < i
        left = jnp.where(past, bs, -jnp.inf)
        bias = jnp.full(bs.shape, _NEG, F32)
        for _ in range(MOBA_TOPK):
            top = jnp.max(left, axis=0, keepdims=True)
            first = jnp.min(jnp.where(left == top, n_iota, nblk), axis=0, keepdims=True)
            hit = n_iota == first
            bias = jnp.where(hit & past, _ZERO, bias)
            left = jnp.where(hit, -jnp.inf, left)
        selb_ref[h] = bias

    kpos = lax.broadcasted_iota(jnp.int32, (blk, blk), 0)
    qpos = lax.broadcasted_iota(jnp.int32, (blk, blk), 1)
    ls = _two_pass_attention(
        q_list, k_ref, vt_ref, sc_ref, acc_ref, lambda h, n: selb_ref[h, pl.ds(n, 1), :],
        _ceil_div_unroll(i), blk,
        own=(pl.multiple_of(i * blk, blk), kpos <= qpos, own_ref))
    for h in heads:
        o = (acc_ref[h] / ls[h]).T
        g = g_ref[:, h * hd:(h + 1) * hd].astype(F32)
        o_ref[:, h * hd:(h + 1) * hd] = (o * _silu(g)).astype(o_ref.dtype)


def _moba(act, act_t, batch, seq, q_col, k_col, g_col, v_row):
    nblk = seq // MOBA_BLOCK
    n = act.shape[0]
    blk = MOBA_BLOCK
    hp = HEADS_PER_STEP
    w = hp * HEAD_DIM
    q_cb, k_cb, g_cb, v_rb = q_col // w, k_col // w, g_col // w, v_row // w
    return pl.pallas_call(
        functools.partial(_moba_kernel, nblk=nblk),
        grid=(batch, MOBA_HEADS // hp, nblk),
        in_specs=[pl.BlockSpec((blk, w), lambda b, h, i: (b * nblk + i, q_cb + h)),
                  pl.BlockSpec((seq, w), lambda b, h, i: (b, k_cb + h)),
                  pl.BlockSpec((w, seq), lambda b, h, i: (v_rb + h, b)),
                  pl.BlockSpec((blk, w), lambda b, h, i: (b * nblk + i, g_cb + h))],
        out_specs=pl.BlockSpec((blk, w), lambda b, h, i: (b * nblk + i, h)),
        out_shape=jax.ShapeDtypeStruct((n, BRANCH_W), BF16),
        scratch_shapes=[pltpu.VMEM((hp, nblk, HEAD_DIM), F32),
                        pltpu.VMEM((hp, nblk, blk), F32),
                        pltpu.VMEM((hp, seq, blk), F32),
                        pltpu.VMEM((hp, HEAD_DIM, blk), F32),
                        pltpu.VMEM((hp, blk, blk), F32)],
        compiler_params=_cparams(("parallel", "parallel", "arbitrary")),
        name="moba",
    )(act, act, act_t, act)


def _conv_kernel(cb_ref, cc_ref, ch_ref, cg_ref, w_ref, bias_ref, o_ref):
    u = cc_ref[...].astype(F32) * ch_ref[...].astype(F32)
    t = lax.broadcasted_iota(jnp.int32, u.shape, 0)
    u1 = jnp.where(t >= 1, pltpu.roll(u, 1, 0), 0.0)
    u2 = jnp.where(t >= 2, pltpu.roll(u, 2, 0), 0.0)
    w = w_ref[...]
    conv = u2 * w[0:1, :] + u1 * w[1:2, :] + u * w[2:3, :] + bias_ref[...]
    o_ref[...] = (cb_ref[...].astype(F32) * conv * _silu(cg_ref[...].astype(F32))).astype(o_ref.dtype)


def _conv(act, conv_w, conv_b, batch, seq, cb_cb, cc_cb, ch_cb, cg_cb):
    n = act.shape[0]
    ncb = BRANCH_W // LANE
    kw = conv_w.shape[0]
    wpad = jnp.zeros((8, BRANCH_W), F32).at[:kw].set(conv_w)

    def spec(base):
        return pl.BlockSpec((seq, LANE), lambda b, c: (b, base + c))

    return pl.pallas_call(
        _conv_kernel,
        grid=(batch, ncb),
        in_specs=[spec(cb_cb), spec(cc_cb), spec(ch_cb), spec(cg_cb),
                  pl.BlockSpec((8, LANE), lambda b, c: (0, c)),
                  pl.BlockSpec((1, LANE), lambda b, c: (0, c))],
        out_specs=pl.BlockSpec((seq, LANE), lambda b, c: (b, c)),
        out_shape=jax.ShapeDtypeStruct((n, BRANCH_W), BF16),
        compiler_params=_cparams(("parallel", "parallel")),
        name="conv",
    )(act, act, act, act, wpad, conv_b.reshape(1, BRANCH_W))


def _dsa_kernel(qc_ref, kc_ref, wt_ref, q_ref, k_ref, vt_ref, g_ref, o_ref, st_ref, tri_ref, sc_ref,
                acc_ref, *, topk, tq):
    i = pl.program_id(1)
    nch = i + 1
    npair = (nch + 1) >> 1
    trips = _ceil_div_unroll(nch)
    idx_scale = (IDX_DIM ** -0.5) * (IDX_HEADS ** -0.5)
    kf = float(topk)
    ninf = np.float32(-np.inf)

    def chunk(c):
        return pl.ds(pl.multiple_of(c * tq, tq), tq)

    def pair(c):
        return pl.ds(pl.multiple_of(c * (2 * tq), 2 * tq), 2 * tq)

    def fold8(x, op):
        return op(x.reshape(x.shape[0] // 8, 8, tq), axis=0)

    @pl.when(pl.program_id(2) == 0)
    def _select():
        def score_body(c, carry):
            mx, mn = carry
            for u in range(2):
                cc = 2 * c + u
                kc = kc_ref[chunk(cc), :]
                sc = jnp.zeros((tq, tq), F32)
                for j in range(IDX_HEADS):
                    lg = _dot_nt(kc, qc_ref[:, j * IDX_SLOT:(j + 1) * IDX_SLOT])
                    sc = sc + (wt_ref[j:j + 1, :] * idx_scale) * jnp.maximum(lg, 0.0)
                kpos = cc * tq + lax.broadcasted_iota(jnp.int32, sc.shape, 0)
                qpos = i * tq + lax.broadcasted_iota(jnp.int32, sc.shape, 1)
                causal = kpos <= qpos
                st_ref[chunk(cc), :] = jnp.where(causal, sc, ninf)
                mx = jnp.maximum(mx, fold8(jnp.where(causal, sc, ninf), jnp.max))
                mn = jnp.minimum(mn, fold8(jnp.where(causal, sc, -ninf), jnp.min))
            return mx, mn
        mx8, mn8 = lax.fori_loop(0, npair, score_body,
                                 (jnp.full((8, tq), ninf, F32), jnp.full((8, tq), -ninf, F32)))
        smax = jnp.max(mx8, axis=0, keepdims=True)
        smin = jnp.min(mn8, axis=0, keepdims=True)

        qrow = i * tq + lax.broadcasted_iota(jnp.int32, (1, tq), 1)
        short = qrow + 1 < topk
        lo0 = jnp.where(short, _ZERO, smin)
        hi0 = jnp.where(short, _ONE, smax + jnp.maximum(jnp.abs(smax), np.float32(1e-30))
                        * np.float32(2.0 ** -10))

        def count(pred):
            def body(c, acc):
                for u in range(2):
                    acc = acc + fold8(_ind(pred(st_ref[chunk(2 * c + u), :])), jnp.sum)
                return acc
            acc = lax.fori_loop(0, npair, body, jnp.zeros((8, tq), F32))
            return jnp.sum(acc, axis=0, keepdims=True)

        def bis_body(_, carry):
            lo, hi = carry
            mid = lo + (hi - lo) * np.float32(0.5)
            ge = count(lambda s: s >= mid) >= kf
            return jnp.where(ge, mid, lo), jnp.where(ge, hi, mid)

        _, hi_f0 = lax.fori_loop(0, BISECT_ITERS, bis_body, (lo0, hi0))

        def walk_cond(carry):
            return jnp.min(carry[3]) < 0.5

        def walk_body(carry):
            hi_f, thr, cnt, done = carry

            def mx_body(c, acc):
                s = st_ref[pair(c), :]
                return jnp.maximum(acc, fold8(jnp.where(s < hi_f, s, ninf), jnp.max))
            mx = lax.fori_loop(0, npair, mx_body, jnp.full((8, tq), ninf, F32))
            v = jnp.max(mx, axis=0, keepdims=True)
            c_v = count(lambda s: s >= v)
            is_done = done > 0.5
            return (jnp.where(is_done, hi_f, v), jnp.where(is_done, thr, v),
                    jnp.where(is_done, cnt, c_v), _ind(is_done | (c_v >= kf)))

        zero_row = jnp.zeros((1, tq), F32)
        all_rows = (2 * npair * tq).astype(F32)
        _, thr, n_ge, _ = lax.while_loop(
            walk_cond, walk_body,
            (hi_f0, jnp.full((1, tq), ninf, F32), zero_row + all_rows, _ind(short)))

        n_gt = count(lambda s: s > thr)
        need = kf - n_gt
        n_eq = n_ge - n_gt

        def bias_all_ties(c, _):
            st_ref[pair(c), :] = jnp.where(st_ref[pair(c), :] >= thr, _ZERO, _NEG)
            return 0

        def bias_some_ties(cc, carry):
            s = st_ref[chunk(cc), :]
            eq = s == thr
            eqf = _ind(eq)
            before = _dot(tri_ref[...], eqf.astype(BF16)) + carry
            kpos = cc * tq + lax.broadcasted_iota(jnp.int32, s.shape, 0)
            qpos = i * tq + lax.broadcasted_iota(jnp.int32, s.shape, 1)
            selected = ((s > thr) | (eq & (before < need))) & (kpos <= qpos)
            st_ref[chunk(cc), :] = jnp.where(selected, _ZERO, _NEG)
            return carry + jnp.sum(eqf, axis=0, keepdims=True)

        def all_ties():
            lax.fori_loop(0, npair, bias_all_ties, 0)

        def some_ties():
            r = lax.broadcasted_iota(jnp.int32, (tq, tq), 0)
            cidx = lax.broadcasted_iota(jnp.int32, (tq, tq), 1)
            tri_ref[...] = _ind(cidx < r).astype(BF16)
            lax.fori_loop(0, 2 * npair, bias_some_ties, zero_row)

        simple = jnp.logical_and(jnp.max(n_eq - need) <= 0.0, jnp.min(thr) > ninf)
        lax.cond(simple, all_ties, some_ties)

    hd = HEAD_DIM
    heads = range(HEADS_PER_STEP)
    q_list = [q_ref[:, h * hd:(h + 1) * hd] for h in heads]
    ls = _two_pass_attention(
        q_list, k_ref, vt_ref, sc_ref, acc_ref, lambda h, c: st_ref[c * tq:(c + 1) * tq, :], trips, tq)
    for h in heads:
        o = (acc_ref[h] / ls[h]).T
        g = g_ref[:, h * hd:(h + 1) * hd].astype(F32)
        o_ref[:, h * hd:(h + 1) * hd] = (o * _silu(g)).astype(o_ref.dtype)


def _dsa(act, act_t, cat, wt, batch, seq, q_col, k_col, g_col, v_row, tq=256):
    assert ATT_UNROLL == 2
    n = act.shape[0]
    nq = seq // tq
    topk = min(DSA_TOPK_MAX, seq // 4)
    n_qcat = IDX_HEADS * IDX_SLOT
    hp = HEADS_PER_STEP
    w = hp * HEAD_DIM
    q_cb, k_cb, g_cb, v_rb = q_col // w, k_col // w, g_col // w, v_row // w
    return pl.pallas_call(
        functools.partial(_dsa_kernel, topk=topk, tq=tq),
        grid=(batch, nq, DSA_HEADS // hp),
        in_specs=[pl.BlockSpec((tq, n_qcat), lambda b, i, h: (b * nq + i, 0)),
                  pl.BlockSpec((seq, IDX_SLOT), lambda b, i, h: (b, n_qcat // IDX_SLOT)),
                  pl.BlockSpec((8, tq), lambda b, i, h: (0, b * nq + i)),
                  pl.BlockSpec((tq, w), lambda b, i, h: (b * nq + i, q_cb + h)),
                  pl.BlockSpec((seq, w), lambda b, i, h: (b, k_cb + h)),
                  pl.BlockSpec((w, seq), lambda b, i, h: (v_rb + h, b)),
                  pl.BlockSpec((tq, w), lambda b, i, h: (b * nq + i, g_cb + h))],
        out_specs=pl.BlockSpec((tq, w), lambda b, i, h: (b * nq + i, h)),
        out_shape=jax.ShapeDtypeStruct((n, BRANCH_W), BF16),
        scratch_shapes=[pltpu.VMEM((seq, tq), F32),
                        pltpu.VMEM((tq, tq), BF16),
                        pltpu.VMEM((hp, seq, tq), F32),
                        pltpu.VMEM((hp, HEAD_DIM, tq), F32)],
        compiler_params=_cparams(("parallel", "arbitrary", "arbitrary")),
        name="dsa",
    )(cat, cat, wt, act, act, act_t, act)


def _mem_kernel(q_ref, mk_ref, mvt_ref, g_ref, o_ref):
    hd = MEM_HEAD_DIM
    for h in range(MEM_HEADS):
        cols = slice(h * hd, (h + 1) * hd)
        s = _dot_nt(mk_ref[:, cols], q_ref[:, cols])
        m = jnp.max(s, axis=0, keepdims=True)
        p = jnp.exp2(s - m)
        l = jnp.sum(p, axis=0, keepdims=True)
        o = (_dot(mvt_ref[cols, :], p.astype(BF16)) / l).T
        o_ref[:, cols] = (o * _silu(g_ref[:, cols].astype(F32))).astype(o_ref.dtype)


def _mem_attn(act, mk, mvt, batch, seq, mem_len, q_col, g_col, tq=512):
    n = act.shape[0]
    nq = seq // tq
    w = BRANCH_W
    return pl.pallas_call(
        _mem_kernel,
        grid=(batch, nq),
        in_specs=[pl.BlockSpec((tq, w), lambda b, i: (b * nq + i, q_col // w)),
                  pl.BlockSpec((mem_len, w), lambda b, i: (b, 0)),
                  pl.BlockSpec((w, mem_len), lambda b, i: (0, b)),
                  pl.BlockSpec((tq, w), lambda b, i: (b * nq + i, g_col // w))],
        out_specs=pl.BlockSpec((tq, w), lambda b, i: (b * nq + i, 0)),
        out_shape=jax.ShapeDtypeStruct((n, BRANCH_W), BF16),
        compiler_params=_cparams(("parallel", "parallel")),
        name="mem_attn",
    )(act, mk, mvt, act)


O_IDX = 14 * BRANCH_W
O_MERGE = O_IDX + IDX_HEADS * IDX_DIM + IDX_DIM + IDX_HEADS


def _input_weights(w_in3):
    wt3 = jnp.swapaxes(w_in3, 1, 2).astype(BF16)
    idx3 = jnp.swapaxes(w_in3[:, :, O_IDX:O_MERGE], 1, 2)
    idx3 = jnp.pad(idx3, ((0, 0), (0, IDX_W_IN - idx3.shape[1]), (0, 0)))
    return wt3, idx3


def _layer_weights(layer, wt3, idx3, w_mem_kv3, w_branch3, w_out3):
    bw = BRANCH_W
    w_idx_hi, w_idx_lo = _split2(idx3[layer])
    return dict(w_merge=wt3[layer, O_MERGE:], w_idx_hi=w_idx_hi, w_idx_lo=w_idx_lo,
                w_mk=w_mem_kv3[layer][:, :bw].astype(BF16),
                w_mvt=_transpose_cast(w_mem_kv3[layer][:, bw:], "w_mvt"),
                w_branch=w_branch3[layer].astype(BF16), w_out=w_out3[layer].astype(BF16))


def _layer(layer, x2, mem2, batch, seq, mem_len, ln_g, conv_w, conv_b, mem_ln_g, wt3, w, final_g):
    nb = BRANCH_W // LANE
    bw = BRANCH_W
    (A_Q, A_K, A_G, C_B, C_C, C_H, C_G, S_Q, S_K, S_G, M_Q, M_G) = range(12)
    tok_groups = (0, 1, 3, 4, 5, 6, 7, 8, 9, 11, 12, 13)
    val_groups = (2, 10)

    log2e = float(np.log2(np.e))
    colscale = np.ones((1, len(tok_groups) * bw), np.float32)
    for col, hdim in ((A_Q, HEAD_DIM), (S_Q, HEAD_DIM), (M_Q, MEM_HEAD_DIM)):
        colscale[:, col * bw:(col + 1) * bw] = hdim ** -0.5 * log2e

    xn, cat, wt = _prep(x2, ln_g, w["w_idx_hi"], w["w_idx_lo"])
    act = _proj_tok(xn, wt3, layer, tok_groups, jnp.asarray(colscale), tm=1024, tn=bw)
    act_r = _mm_nt(xn, w["w_merge"], BF16, tm=1024, tn=1024, name="proj_merge")
    act_t = _proj_vt(wt3, layer, val_groups, xn, tm=bw, tn=1024)

    y_a = _moba(act, act_t, batch, seq, A_Q * bw, A_K * bw, A_G * bw, 0)
    y_c = _conv(act, conv_w, conv_b, batch, seq, C_B * nb, C_C * nb, C_H * nb, C_G * nb)
    y_s = _dsa(act, act_t, cat, wt, batch, seq, S_Q * bw, S_K * bw, S_G * bw, bw)

    mem_n = _rmsnorm(mem2, mem_ln_g, BF16)
    mk = _mm_nn(mem_n, w["w_mk"], BF16, tm=mem2.shape[0], tn=512, name="mem_k")
    mvt = _mm_nt(w["w_mvt"], mem_n, BF16, tm=512, tn=mem2.shape[0], name="mem_vt")
    y_m = _mem_attn(act, mk, mvt, batch, seq, mem_len, M_Q * bw, M_G * bw)

    merged = _merge((y_a, y_c, y_s, y_m), act_r, w["w_branch"], tm=512, tn=1024)
    return _out_proj(merged, w["w_out"], x2, final_g, tm=512)


def kernel(x, mem, ln_g, w_in, conv_w, conv_b, mem_ln_g, w_mem_kv, w_branch, w_out, final_g):
    batch, seq, d = x.shape
    mem_len = mem.shape[1]
    x2 = x.reshape(batch * seq, d)
    mem2 = mem.reshape(batch * mem_len, d)
    wt3, idx3 = _input_weights(w_in)
    depth = ln_g.shape[0]
    for layer in range(depth):
        w = _layer_weights(layer, wt3, idx3, w_mem_kv, w_branch, w_out)
        x2 = _layer(layer, x2, mem2, batch, seq, mem_len, ln_g[layer], conv_w[layer],
                    conv_b[layer], mem_ln_g[layer], wt3, w,
                    final_g if layer == depth - 1 else None)
    return x2.reshape(batch, seq, d)
```

```python
import functools

import numpy as np
import jax
import jax.numpy as jnp
from jax import lax
from jax.experimental import pallas as pl
from jax.experimental.pallas import tpu as pltpu

EPS = 1e-6
BRANCH_W = 1024
MOBA_HEADS = 8
MOBA_BLOCK = 256
MOBA_TOPK = 3
DSA_HEADS = 8
DSA_TOPK_MAX = 256
IDX_HEADS = 4
IDX_DIM = 64
MEM_HEADS = 4
HEAD_DIM = 128
MEM_HEAD_DIM = 256
IDX_SLOT = 256
NEG = -1e30
ATT_UNROLL = 2
HEADS_PER_STEP = 4
BISECT_ITERS = 18
LANE = 128
VMEM_LIMIT = 56 * 1024 * 1024

F32 = jnp.float32
BF16 = jnp.bfloat16
_ONE = np.float32(1.0)
_ZERO = np.float32(0.0)
_NEG = np.float32(NEG)


def _cparams(sem):
    return pltpu.CompilerParams(dimension_semantics=sem, vmem_limit_bytes=VMEM_LIMIT)


def _dot(a, b):
    return jnp.dot(a, b, preferred_element_type=F32)


def _dot_nt(a, b):
    return lax.dot_general(a, b, (((1,), (1,)), ((), ())), preferred_element_type=F32)


def _split2(v):
    hi = v.astype(BF16)
    lo = (v - hi.astype(F32)).astype(BF16)
    return hi, lo


def _split3(v):
    h1 = v.astype(BF16)
    r1 = v - h1.astype(F32)
    h2 = r1.astype(BF16)
    h3 = (r1 - h2.astype(F32)).astype(BF16)
    return h1, h2, h3


def _ind(cond):
    return jnp.where(cond, _ONE, _ZERO)


def _silu(g):
    return g * jax.nn.sigmoid(g)


def _rms(x, g):
    var = jnp.mean(x * x, axis=-1, keepdims=True)
    return (x * lax.rsqrt(var + EPS)) * g


def _rmsnorm_kernel(x_ref, g_ref, o_ref):
    o_ref[...] = _rms(x_ref[...], g_ref[...]).astype(o_ref.dtype)


def _rmsnorm(x, g, out_dtype, tm=256):
    n, d = x.shape
    return pl.pallas_call(
        _rmsnorm_kernel,
        grid=(n // tm,),
        in_specs=[pl.BlockSpec((tm, d), lambda i: (i, 0)),
                  pl.BlockSpec((1, d), lambda i: (0, 0))],
        out_specs=pl.BlockSpec((tm, d), lambda i: (i, 0)),
        out_shape=jax.ShapeDtypeStruct((n, d), out_dtype),
        compiler_params=_cparams(("parallel",)),
        name="rmsnorm",
    )(x, g.reshape(1, d))


def _prep_kernel(x_ref, g_ref, whi_ref, wlo_ref, phi_ref, plo_ref, xn_ref, cat_ref, wt_ref):
    xn = _rms(x_ref[...], g_ref[...])
    hi, lo = _split2(xn)
    xn_ref[...] = hi
    acc = _dot_nt(hi, whi_ref[...]) + _dot_nt(hi, wlo_ref[...]) + _dot_nt(lo, whi_ref[...])
    vh, vl = _split2(acc)
    cat_ref[...] = (_dot(vh, phi_ref[...]) + _dot(vl, plo_ref[...])).astype(BF16)
    gate_row = IDX_DIM
    wt_ref[...] = acc[:, IDX_W_IN - LANE:].T[gate_row:gate_row + 8, :]


IDX_W_IN = 384


def _idx_placement():
    n_out = (IDX_HEADS + 1) * IDX_SLOT
    p_hi = np.zeros((IDX_W_IN, n_out), np.float32)
    p_lo = np.zeros((IDX_W_IN, n_out), np.float32)
    e = np.arange(IDX_DIM)
    for j in range(IDX_HEADS):
        src, dst = j * IDX_DIM + e, j * IDX_SLOT + e
        p_hi[src, dst] = 1
        p_hi[src, dst + IDX_DIM] = 1
        p_lo[src, dst + 2 * IDX_DIM] = 1
    src, dst = IDX_HEADS * IDX_DIM + e, IDX_HEADS * IDX_SLOT + e
    p_hi[src, dst] = 1
    p_lo[src, dst + IDX_DIM] = 1
    p_hi[src, dst + 2 * IDX_DIM] = 1
    return jnp.asarray(p_hi, BF16), jnp.asarray(p_lo, BF16)


def _prep(x, g, whi, wlo, tm=512):
    n, d = x.shape
    p_hi, p_lo = _idx_placement()
    co = p_hi.shape[1]
    return pl.pallas_call(
        _prep_kernel,
        grid=(n // tm,),
        in_specs=[pl.BlockSpec((tm, d), lambda i: (i, 0)),
                  pl.BlockSpec((1, d), lambda i: (0, 0)),
                  pl.BlockSpec((IDX_W_IN, d), lambda i: (0, 0)),
                  pl.BlockSpec((IDX_W_IN, d), lambda i: (0, 0)),
                  pl.BlockSpec((IDX_W_IN, co), lambda i: (0, 0)),
                  pl.BlockSpec((IDX_W_IN, co), lambda i: (0, 0))],
        out_specs=[pl.BlockSpec((tm, d), lambda i: (i, 0)),
                   pl.BlockSpec((tm, co), lambda i: (i, 0)),
                   pl.BlockSpec((8, tm), lambda i: (0, i))],
        out_shape=[jax.ShapeDtypeStruct((n, d), BF16),
                   jax.ShapeDtypeStruct((n, co), BF16),
                   jax.ShapeDtypeStruct((8, n), F32)],
        compiler_params=_cparams(("parallel",)),
        name="prep",
    )(x, g.reshape(1, d), whi, wlo, p_hi, p_lo)


def _transpose_cast_kernel(w_ref, o_ref):
    o_ref[...] = w_ref[...].T.astype(o_ref.dtype)


def _transpose_cast(w, name, t=512):
    d, n = w.shape
    return pl.pallas_call(
        _transpose_cast_kernel,
        grid=(n // t, d // t),
        in_specs=[pl.BlockSpec((t, t), lambda i, j: (j, i))],
        out_specs=pl.BlockSpec((t, t), lambda i, j: (i, j)),
        out_shape=jax.ShapeDtypeStruct((n, d), BF16),
        compiler_params=_cparams(("parallel", "parallel")),
        name=name,
    )(w)


def _mm_nn_kernel(a_ref, b_ref, o_ref):
    o_ref[...] = _dot(a_ref[...], b_ref[...]).astype(o_ref.dtype)


def _mm_nn(a, b, out_dtype, tm, tn, name):
    m, k = a.shape
    n = b.shape[1]
    return pl.pallas_call(
        _mm_nn_kernel,
        grid=(m // tm, n // tn),
        in_specs=[pl.BlockSpec((tm, k), lambda i, j: (i, 0)),
                  pl.BlockSpec((k, tn), lambda i, j: (0, j))],
        out_specs=pl.BlockSpec((tm, tn), lambda i, j: (i, j)),
        out_shape=jax.ShapeDtypeStruct((m, n), out_dtype),
        compiler_params=_cparams(("parallel", "parallel")),
        name=name,
    )(a, b)


def _proj_tok_kernel(blk_ref, a_ref, b_ref, s_ref, o_ref):
    del blk_ref
    o_ref[...] = (_dot_nt(a_ref[...], b_ref[...]) * s_ref[...]).astype(o_ref.dtype)


def _proj_tok(a, wt3, layer, row_blocks, colscale, tm, tn):
    m, k = a.shape
    n = len(row_blocks) * tn
    return pl.pallas_call(
        _proj_tok_kernel,
        grid_spec=pltpu.PrefetchScalarGridSpec(
            num_scalar_prefetch=1,
            grid=(m // tm, n // tn),
            in_specs=[pl.BlockSpec((tm, k), lambda i, j, blk: (i, 0)),
                      pl.BlockSpec((None, tn, k), lambda i, j, blk: (layer, blk[j], 0)),
                      pl.BlockSpec((1, tn), lambda i, j, blk: (0, j))],
            out_specs=pl.BlockSpec((tm, tn), lambda i, j, blk: (i, j))),
        out_shape=jax.ShapeDtypeStruct((m, n), BF16),
        compiler_params=_cparams(("parallel", "parallel")),
        name="proj_tok",
    )(jnp.asarray(row_blocks, jnp.int32), a, wt3, colscale)


def _proj_vt_kernel(blk_ref, a_ref, b_ref, o_ref):
    del blk_ref
    o_ref[...] = _dot_nt(a_ref[...], b_ref[...]).astype(o_ref.dtype)


def _proj_vt(wt3, layer, row_blocks, b, tm, tn):
    n, k = b.shape
    m = len(row_blocks) * tm
    return pl.pallas_call(
        _proj_vt_kernel,
        grid_spec=pltpu.PrefetchScalarGridSpec(
            num_scalar_prefetch=1,
            grid=(n // tn, m // tm),
            in_specs=[pl.BlockSpec((None, tm, k), lambda j, i, blk: (layer, blk[i], 0)),
                      pl.BlockSpec((tn, k), lambda j, i, blk: (j, 0))],
            out_specs=pl.BlockSpec((tm, tn), lambda j, i, blk: (i, j))),
        out_shape=jax.ShapeDtypeStruct((m, n), BF16),
        compiler_params=_cparams(("parallel", "parallel")),
        name="proj_vt",
    )(jnp.asarray(row_blocks, jnp.int32), wt3, b)


def _mm_nt_kernel(a_ref, b_ref, o_ref):
    o_ref[...] = _dot_nt(a_ref[...], b_ref[...]).astype(o_ref.dtype)


def _mm_nt(a, b, out_dtype, tm, tn, name):
    m, k = a.shape
    n = b.shape[0]
    return pl.pallas_call(
        _mm_nt_kernel,
        grid=(n // tn, m // tm),
        in_specs=[pl.BlockSpec((tm, k), lambda j, i: (i, 0)),
                  pl.BlockSpec((tn, k), lambda j, i: (j, 0))],
        out_specs=pl.BlockSpec((tm, tn), lambda j, i: (i, j)),
        out_shape=jax.ShapeDtypeStruct((m, n), out_dtype),
        compiler_params=_cparams(("parallel", "parallel")),
        name=name,
    )(a, b)


def _out_proj_kernel(a_ref, b_ref, r_ref, g_ref, o_ref, *, final_norm):
    y = r_ref[...] + _dot(a_ref[...], b_ref[...])
    o_ref[...] = _rms(y, g_ref[...]) if final_norm else y


def _out_proj(a, b, res, final_g, tm):
    m, k = a.shape
    n = b.shape[1]
    g = jnp.ones((1, n), F32) if final_g is None else final_g.reshape(1, n)
    return pl.pallas_call(
        functools.partial(_out_proj_kernel, final_norm=final_g is not None),
        grid=(m // tm,),
        in_specs=[pl.BlockSpec((tm, k), lambda i: (i, 0)),
                  pl.BlockSpec((k, n), lambda i: (0, 0)),
                  pl.BlockSpec((tm, n), lambda i: (i, 0)),
                  pl.BlockSpec((1, n), lambda i: (0, 0))],
        out_specs=pl.BlockSpec((tm, n), lambda i: (i, 0)),
        out_shape=jax.ShapeDtypeStruct((m, n), F32),
        compiler_params=_cparams(("parallel",)),
        name="out_proj",
    )(a, b, res, g)


def _merge_kernel(ya_ref, yc_ref, ys_ref, ym_ref, ra_ref, rc_ref, rs_ref, rm_ref, wb_ref, o_ref):
    ys = (ya_ref, yc_ref, ys_ref, ym_ref)
    rs = (ra_ref, rc_ref, rs_ref, rm_ref)
    acc = None
    for br in range(4):
        z = _dot(ys[br][...], wb_ref[br])
        term = jax.nn.sigmoid(rs[br][...].astype(F32)) * z
        acc = term if acc is None else acc + term
    o_ref[...] = acc.astype(o_ref.dtype)


def _merge(ys, gates, wb, tm, tn):
    n, w = ys[0].shape
    d = wb.shape[2]
    y_specs = [pl.BlockSpec((tm, w), lambda j, i: (i, 0)) for _ in range(4)]
    r_specs = [pl.BlockSpec((tm, tn), functools.partial(
        lambda j, i, base: (i, base + j), base=br * d // tn)) for br in range(4)]
    return pl.pallas_call(
        _merge_kernel,
        grid=(d // tn, n // tm),
        in_specs=y_specs + r_specs + [pl.BlockSpec((4, w, tn), lambda j, i: (0, 0, j))],
        out_specs=pl.BlockSpec((tm, tn), lambda j, i: (i, j)),
        out_shape=jax.ShapeDtypeStruct((n, d), BF16),
        compiler_params=_cparams(("parallel", "parallel")),
        name="merge",
    )(*ys, gates, gates, gates, gates, wb)


def _ceil_div_unroll(n):
    return (n + (ATT_UNROLL - 1)) >> (ATT_UNROLL.bit_length() - 1)


def _two_pass_attention(q_list, k_ref, vt_ref, sc_ref, acc_ref, bias_fn, trips, blk, own=None):
    heads = range(len(q_list))
    hd = HEAD_DIM
    span = ATT_UNROLL * blk
    max_trips = k_ref.shape[0] // span
    tq = q_list[0].shape[0]

    def pass1(h, n_trips):
        m = jnp.full((1, tq), _NEG, F32)
        for j in range(n_trips):
            s = _dot_nt(k_ref[j * span:(j + 1) * span, h * hd:(h + 1) * hd], q_list[h])
            for u in range(ATT_UNROLL):
                n = j * ATT_UNROLL + u
                su = s[u * blk:(u + 1) * blk, :] + bias_fn(h, n)
                sc_ref[h, n * blk:(n + 1) * blk, :] = su
                m = jnp.maximum(m, jnp.max(su, axis=0, keepdims=True))
        if own is not None:
            off, keep, own_ref = own
            s = _dot_nt(k_ref[pl.ds(off, blk), h * hd:(h + 1) * hd], q_list[h])
            s = jnp.where(keep, s, _NEG)
            own_ref[h] = s
            m = jnp.maximum(m, jnp.max(s, axis=0, keepdims=True))
        return m

    def pass2(h, n_trips, m):
        l = jnp.zeros((1, tq), F32)
        acc = None
        for j in range(n_trips):
            p = jnp.exp2(sc_ref[h, j * span:(j + 1) * span, :] - m)
            l = l + jnp.sum(p, axis=0, keepdims=True)
            pv = _dot(vt_ref[h * hd:(h + 1) * hd, j * span:(j + 1) * span], p.astype(BF16))
            acc = pv if acc is None else acc + pv
        if own is not None:
            off, keep, own_ref = own
            p = jnp.exp2(own_ref[h] - m)
            l = l + jnp.sum(p, axis=0, keepdims=True)
            pv = _dot(vt_ref[h * hd:(h + 1) * hd, pl.ds(off, blk)], p.astype(BF16))
            acc = pv if acc is None else acc + pv
        acc_ref[h] = jnp.zeros((hd, tq), F32) if acc is None else acc
        return l

    def run(n_trips):
        ms, ls = [], []
        for h in heads:
            ms.append(pass1(h, n_trips))
            if h > 0:
                ls.append(pass2(h - 1, n_trips, ms[h - 1]))
        ls.append(pass2(len(q_list) - 1, n_trips, ms[-1]))
        return tuple(ls)

    return lax.switch(trips, [functools.partial(run, t) for t in range(max_trips + 1)])


def _moba_kernel(q_ref, k_ref, vt_ref, g_ref, o_ref, kmean_ref, selb_ref, sc_ref, acc_ref, own_ref,
                 *, nblk):
    i = pl.program_id(2)
    blk = MOBA_BLOCK
    hd = HEAD_DIM
    heads = range(HEADS_PER_STEP)

    @pl.when(i == 0)
    def _():
        for h in heads:
            kf = k_ref[:, h * hd:(h + 1) * hd].astype(F32).reshape(nblk, blk, hd)
            kmean_ref[h] = jnp.mean(kf, axis=1)

    q_list = []
    for h in heads:
        q = q_ref[:, h * hd:(h + 1) * hd]
        q_list.append(q)
        k1, k2, k3 = _split3(kmean_ref[h])
        bs = _dot_nt(k1, q) + _dot_nt(k2, q) + _dot_nt(k3, q)
        n_iota = lax.broadcasted_iota(jnp.int32, bs.shape, 0)
        past = n_iota < i
        left = jnp.where(past, bs, -jnp.inf)
        bias = jnp.full(bs.shape, _NEG, F32)
        for _ in range(MOBA_TOPK):
            top = jnp.max(left, axis=0, keepdims=True)
            first = jnp.min(jnp.where(left == top, n_iota, nblk), axis=0, keepdims=True)
            hit = n_iota == first
            bias = jnp.where(hit & past, _ZERO, bias)
            left = jnp.where(hit, -jnp.inf, left)
        selb_ref[h] = bias

    kpos = lax.broadcasted_iota(jnp.int32, (blk, blk), 0)
    qpos = lax.broadcasted_iota(jnp.int32, (blk, blk), 1)
    ls = _two_pass_attention(
        q_list, k_ref, vt_ref, sc_ref, acc_ref, lambda h, n: selb_ref[h, pl.ds(n, 1), :],
        _ceil_div_unroll(i), blk,
        own=(pl.multiple_of(i * blk, blk), kpos <= qpos, own_ref))
    for h in heads:
        o = (acc_ref[h] / ls[h]).T
        g = g_ref[:, h * hd:(h + 1) * hd].astype(F32)
        o_ref[:, h * hd:(h + 1) * hd] = (o * _silu(g)).astype(o_ref.dtype)


def _moba(act, act_t, batch, seq, q_col, k_col, g_col, v_row):
    nblk = seq // MOBA_BLOCK
    n = act.shape[0]
    blk = MOBA_BLOCK
    hp = HEADS_PER_STEP
    w = hp * HEAD_DIM
    q_cb, k_cb, g_cb, v_rb = q_col // w, k_col // w, g_col // w, v_row // w
    return pl.pallas_call(
        functools.partial(_moba_kernel, nblk=nblk),
        grid=(batch, MOBA_HEADS // hp, nblk),
        in_specs=[pl.BlockSpec((blk, w), lambda b, h, i: (b * nblk + i, q_cb + h)),
                  pl.BlockSpec((seq, w), lambda b, h, i: (b, k_cb + h)),
                  pl.BlockSpec((w, seq), lambda b, h, i: (v_rb + h, b)),
                  pl.BlockSpec((blk, w), lambda b, h, i: (b * nblk + i, g_cb + h))],
        out_specs=pl.BlockSpec((blk, w), lambda b, h, i: (b * nblk + i, h)),
        out_shape=jax.ShapeDtypeStruct((n, BRANCH_W), BF16),
        scratch_shapes=[pltpu.VMEM((hp, nblk, HEAD_DIM), F32),
                        pltpu.VMEM((hp, nblk, blk), F32),
                        pltpu.VMEM((hp, seq, blk), F32),
                        pltpu.VMEM((hp, HEAD_DIM, blk), F32),
                        pltpu.VMEM((hp, blk, blk), F32)],
        compiler_params=_cparams(("parallel", "parallel", "arbitrary")),
        name="moba",
    )(act, act, act_t, act)


def _conv_kernel(cb_ref, cc_ref, ch_ref, cg_ref, w_ref, bias_ref, o_ref):
    u = cc_ref[...].astype(F32) * ch_ref[...].astype(F32)
    t = lax.broadcasted_iota(jnp.int32, u.shape, 0)
    u1 = jnp.where(t >= 1, pltpu.roll(u, 1, 0), 0.0)
    u2 = jnp.where(t >= 2, pltpu.roll(u, 2, 0), 0.0)
    w = w_ref[...]
    conv = u2 * w[0:1, :] + u1 * w[1:2, :] + u * w[2:3, :] + bias_ref[...]
    o_ref[...] = (cb_ref[...].astype(F32) * conv * _silu(cg_ref[...].astype(F32))).astype(o_ref.dtype)


def _conv(act, conv_w, conv_b, batch, seq, cb_cb, cc_cb, ch_cb, cg_cb):
    n = act.shape[0]
    ncb = BRANCH_W // LANE
    kw = conv_w.shape[0]
    wpad = jnp.zeros((8, BRANCH_W), F32).at[:kw].set(conv_w)

    def spec(base):
        return pl.BlockSpec((seq, LANE), lambda b, c: (b, base + c))

    return pl.pallas_call(
        _conv_kernel,
        grid=(batch, ncb),
        in_specs=[spec(cb_cb), spec(cc_cb), spec(ch_cb), spec(cg_cb),
                  pl.BlockSpec((8, LANE), lambda b, c: (0, c)),
                  pl.BlockSpec((1, LANE), lambda b, c: (0, c))],
        out_specs=pl.BlockSpec((seq, LANE), lambda b, c: (b, c)),
        out_shape=jax.ShapeDtypeStruct((n, BRANCH_W), BF16),
        compiler_params=_cparams(("parallel", "parallel")),
        name="conv",
    )(act, act, act, act, wpad, conv_b.reshape(1, BRANCH_W))


def _dsa_kernel(qc_ref, kc_ref, wt_ref, q_ref, k_ref, vt_ref, g_ref, o_ref, st_ref, tri_ref, sc_ref,
                acc_ref, *, topk, tq):
    i = pl.program_id(1)
    nch = i + 1
    npair = (nch + 1) >> 1
    trips = _ceil_div_unroll(nch)
    idx_scale = (IDX_DIM ** -0.5) * (IDX_HEADS ** -0.5)
    kf = float(topk)
    ninf = np.float32(-np.inf)

    def chunk(c):
        return pl.ds(pl.multiple_of(c * tq, tq), tq)

    def pair(c):
        return pl.ds(pl.multiple_of(c * (2 * tq), 2 * tq), 2 * tq)

    def fold8(x, op):
        return op(x.reshape(x.shape[0] // 8, 8, tq), axis=0)

    @pl.when(pl.program_id(2) == 0)
    def _select():
        def score_body(c, carry):
            mx, mn = carry
            for u in range(2):
                cc = 2 * c + u
                kc = kc_ref[chunk(cc), :]
                sc = jnp.zeros((tq, tq), F32)
                for j in range(IDX_HEADS):
                    lg = _dot_nt(kc, qc_ref[:, j * IDX_SLOT:(j + 1) * IDX_SLOT])
                    sc = sc + (wt_ref[j:j + 1, :] * idx_scale) * jnp.maximum(lg, 0.0)
                kpos = cc * tq + lax.broadcasted_iota(jnp.int32, sc.shape, 0)
                qpos = i * tq + lax.broadcasted_iota(jnp.int32, sc.shape, 1)
                causal = kpos <= qpos
                st_ref[chunk(cc), :] = jnp.where(causal, sc, ninf)
                mx = jnp.maximum(mx, fold8(jnp.where(causal, sc, ninf), jnp.max))
                mn = jnp.minimum(mn, fold8(jnp.where(causal, sc, -ninf), jnp.min))
            return mx, mn
        mx8, mn8 = lax.fori_loop(0, npair, score_body,
                                 (jnp.full((8, tq), ninf, F32), jnp.full((8, tq), -ninf, F32)))
        smax = jnp.max(mx8, axis=0, keepdims=True)
        smin = jnp.min(mn8, axis=0, keepdims=True)

        qrow = i * tq + lax.broadcasted_iota(jnp.int32, (1, tq), 1)
        short = qrow + 1 < topk
        lo0 = jnp.where(short, _ZERO, smin)
        hi0 = jnp.where(short, _ONE, smax + jnp.maximum(jnp.abs(smax), np.float32(1e-30))
                        * np.float32(2.0 ** -10))

        def count(pred):
            def body(c, acc):
                for u in range(2):
                    acc = acc + fold8(_ind(pred(st_ref[chunk(2 * c + u), :])), jnp.sum)
                return acc
            acc = lax.fori_loop(0, npair, body, jnp.zeros((8, tq), F32))
            return jnp.sum(acc, axis=0, keepdims=True)

        def bis_body(_, carry):
            lo, hi = carry
            mid = lo + (hi - lo) * np.float32(0.5)
            ge = count(lambda s: s >= mid) >= kf
            return jnp.where(ge, mid, lo), jnp.where(ge, hi, mid)

        _, hi_f0 = lax.fori_loop(0, BISECT_ITERS, bis_body, (lo0, hi0))

        def walk_cond(carry):
            return jnp.min(carry[3]) < 0.5

        def walk_body(carry):
            hi_f, thr, cnt, done = carry

            def mx_body(c, acc):
                s = st_ref[pair(c), :]
                return jnp.maximum(acc, fold8(jnp.where(s < hi_f, s, ninf), jnp.max))
            mx = lax.fori_loop(0, npair, mx_body, jnp.full((8, tq), ninf, F32))
            v = jnp.max(mx, axis=0, keepdims=True)
            c_v = count(lambda s: s >= v)
            is_done = done > 0.5
            return (jnp.where(is_done, hi_f, v), jnp.where(is_done, thr, v),
                    jnp.where(is_done, cnt, c_v), _ind(is_done | (c_v >= kf)))

        zero_row = jnp.zeros((1, tq), F32)
        all_rows = (2 * npair * tq).astype(F32)
        _, thr, n_ge, _ = lax.while_loop(
            walk_cond, walk_body,
            (hi_f0, jnp.full((1, tq), ninf, F32), zero_row + all_rows, _ind(short)))

        n_gt = count(lambda s: s > thr)
        need = kf - n_gt
        n_eq = n_ge - n_gt

        def bias_all_ties(c, _):
            st_ref[pair(c), :] = jnp.where(st_ref[pair(c), :] >= thr, _ZERO, _NEG)
            return 0

        def bias_some_ties(cc, carry):
            s = st_ref[chunk(cc), :]
            eq = s == thr
            eqf = _ind(eq)
            before = _dot(tri_ref[...], eqf.astype(BF16)) + carry
            kpos = cc * tq + lax.broadcasted_iota(jnp.int32, s.shape, 0)
            qpos = i * tq + lax.broadcasted_iota(jnp.int32, s.shape, 1)
            selected = ((s > thr) | (eq & (before < need))) & (kpos <= qpos)
            st_ref[chunk(cc), :] = jnp.where(selected, _ZERO, _NEG)
            return carry + jnp.sum(eqf, axis=0, keepdims=True)

        def all_ties():
            lax.fori_loop(0, npair, bias_all_ties, 0)

        def some_ties():
            r = lax.broadcasted_iota(jnp.int32, (tq, tq), 0)
            cidx = lax.broadcasted_iota(jnp.int32, (tq, tq), 1)
            tri_ref[...] = _ind(cidx < r).astype(BF16)
            lax.fori_loop(0, 2 * npair, bias_some_ties, zero_row)

        simple = jnp.logical_and(jnp.max(n_eq - need) <= 0.0, jnp.min(thr) > ninf)
        lax.cond(simple, all_ties, some_ties)

    hd = HEAD_DIM
    heads = range(HEADS_PER_STEP)
    q_list = [q_ref[:, h * hd:(h + 1) * hd] for h in heads]
    ls = _two_pass_attention(
        q_list, k_ref, vt_ref, sc_ref, acc_ref, lambda h, c: st_ref[c * tq:(c + 1) * tq, :], trips, tq)
    for h in heads:
        o = (acc_ref[h] / ls[h]).T
        g = g_ref[:, h * hd:(h + 1) * hd].astype(F32)
        o_ref[:, h * hd:(h + 1) * hd] = (o * _silu(g)).astype(o_ref.dtype)


def _dsa(act, act_t, cat, wt, batch, seq, q_col, k_col, g_col, v_row, tq=256):
    assert ATT_UNROLL == 2
    n = act.shape[0]
    nq = seq // tq
    topk = min(DSA_TOPK_MAX, seq // 4)
    n_qcat = IDX_HEADS * IDX_SLOT
    hp = HEADS_PER_STEP
    w = hp * HEAD_DIM
    q_cb, k_cb, g_cb, v_rb = q_col // w, k_col // w, g_col // w, v_row // w
    return pl.pallas_call(
        functools.partial(_dsa_kernel, topk=topk, tq=tq),
        grid=(batch, nq, DSA_HEADS // hp),
        in_specs=[pl.BlockSpec((tq, n_qcat), lambda b, i, h: (b * nq + i, 0)),
                  pl.BlockSpec((seq, IDX_SLOT), lambda b, i, h: (b, n_qcat // IDX_SLOT)),
                  pl.BlockSpec((8, tq), lambda b, i, h: (0, b * nq + i)),
                  pl.BlockSpec((tq, w), lambda b, i, h: (b * nq + i, q_cb + h)),
                  pl.BlockSpec((seq, w), lambda b, i, h: (b, k_cb + h)),
                  pl.BlockSpec((w, seq), lambda b, i, h: (v_rb + h, b)),
                  pl.BlockSpec((tq, w), lambda b, i, h: (b * nq + i, g_cb + h))],
        out_specs=pl.BlockSpec((tq, w), lambda b, i, h: (b * nq + i, h)),
        out_shape=jax.ShapeDtypeStruct((n, BRANCH_W), BF16),
        scratch_shapes=[pltpu.VMEM((seq, tq), F32),
                        pltpu.VMEM((tq, tq), BF16),
                        pltpu.VMEM((hp, seq, tq), F32),
                        pltpu.VMEM((hp, HEAD_DIM, tq), F32)],
        compiler_params=_cparams(("parallel", "arbitrary", "arbitrary")),
        name="dsa",
    )(cat, cat, wt, act, act, act_t, act)


def _mem_kernel(q_ref, mk_ref, mvt_ref, g_ref, o_ref):
    hd = MEM_HEAD_DIM
    for h in range(MEM_HEADS):
        cols = slice(h * hd, (h + 1) * hd)
        s = _dot_nt(mk_ref[:, cols], q_ref[:, cols])
        m = jnp.max(s, axis=0, keepdims=True)
        p = jnp.exp2(s - m)
        l = jnp.sum(p, axis=0, keepdims=True)
        o = (_dot(mvt_ref[cols, :], p.astype(BF16)) / l).T
        o_ref[:, cols] = (o * _silu(g_ref[:, cols].astype(F32))).astype(o_ref.dtype)


def _mem_attn(act, mk, mvt, batch, seq, mem_len, q_col, g_col, tq=512):
    n = act.shape[0]
    nq = seq // tq
    w = BRANCH_W
    return pl.pallas_call(
        _mem_kernel,
        grid=(batch, nq),
        in_specs=[pl.BlockSpec((tq, w), lambda b, i: (b * nq + i, q_col // w)),
                  pl.BlockSpec((mem_len, w), lambda b, i: (b, 0)),
                  pl.BlockSpec((w, mem_len), lambda b, i: (0, b)),
                  pl.BlockSpec((tq, w), lambda b, i: (b * nq + i, g_col // w))],
        out_specs=pl.BlockSpec((tq, w), lambda b, i: (b * nq + i, 0)),
        out_shape=jax.ShapeDtypeStruct((n, BRANCH_W), BF16),
        compiler_params=_cparams(("parallel", "parallel")),
        name="mem_attn",
    )(act, mk, mvt, act)


O_IDX = 14 * BRANCH_W
O_MERGE = O_IDX + IDX_HEADS * IDX_DIM + IDX_DIM + IDX_HEADS


def _input_weights(w_in3):
    wt3 = jnp.swapaxes(w_in3, 1, 2).astype(BF16)
    idx3 = jnp.swapaxes(w_in3[:, :, O_IDX:O_MERGE], 1, 2)
    idx3 = jnp.pad(idx3, ((0, 0), (0, IDX_W_IN - idx3.shape[1]), (0, 0)))
    return wt3, idx3


def _layer_weights(layer, wt3, idx3, w_mem_kv3, w_branch3, w_out3):
    bw = BRANCH_W
    w_idx_hi, w_idx_lo = _split2(idx3[layer])
    return dict(w_merge=wt3[layer, O_MERGE:], w_idx_hi=w_idx_hi, w_idx_lo=w_idx_lo,
                w_mk=w_mem_kv3[layer][:, :bw].astype(BF16),
                w_mvt=_transpose_cast(w_mem_kv3[layer][:, bw:], "w_mvt"),
                w_branch=w_branch3[layer].astype(BF16), w_out=w_out3[layer].astype(BF16))


def _layer(layer, x2, mem2, batch, seq, mem_len, ln_g, conv_w, conv_b, mem_ln_g, wt3, w, final_g):
    nb = BRANCH_W // LANE
    bw = BRANCH_W
    (A_Q, A_K, A_G, C_B, C_C, C_H, C_G, S_Q, S_K, S_G, M_Q, M_G) = range(12)
    tok_groups = (0, 1, 3, 4, 5, 6, 7, 8, 9, 11, 12, 13)
    val_groups = (2, 10)

    log2e = float(np.log2(np.e))
    colscale = np.ones((1, len(tok_groups) * bw), np.float32)
    for col, hdim in ((A_Q, HEAD_DIM), (S_Q, HEAD_DIM), (M_Q, MEM_HEAD_DIM)):
        colscale[:, col * bw:(col + 1) * bw] = hdim ** -0.5 * log2e

    xn, cat, wt = _prep(x2, ln_g, w["w_idx_hi"], w["w_idx_lo"])
    act = _proj_tok(xn, wt3, layer, tok_groups, jnp.asarray(colscale), tm=2048, tn=bw)
    act_r = _mm_nt(xn, w["w_merge"], BF16, tm=2048, tn=1024, name="proj_merge")
    act_t = _proj_vt(wt3, layer, val_groups, xn, tm=bw, tn=2048)

    y_a = _moba(act, act_t, batch, seq, A_Q * bw, A_K * bw, A_G * bw, 0)
    y_c = _conv(act, conv_w, conv_b, batch, seq, C_B * nb, C_C * nb, C_H * nb, C_G * nb)
    y_s = _dsa(act, act_t, cat, wt, batch, seq, S_Q * bw, S_K * bw, S_G * bw, bw)

    mem_n = _rmsnorm(mem2, mem_ln_g, BF16)
    mk = _mm_nn(mem_n, w["w_mk"], BF16, tm=mem2.shape[0], tn=512, name="mem_k")
    mvt = _mm_nt(w["w_mvt"], mem_n, BF16, tm=512, tn=mem2.shape[0], name="mem_vt")
    y_m = _mem_attn(act, mk, mvt, batch, seq, mem_len, M_Q * bw, M_G * bw)

    merged = _merge((y_a, y_c, y_s, y_m), act_r, w["w_branch"], tm=512, tn=1024)
    return _out_proj(merged, w["w_out"], x2, final_g, tm=512)


def kernel(x, mem, ln_g, w_in, conv_w, conv_b, mem_ln_g, w_mem_kv, w_branch, w_out, final_g):
    batch, seq, d = x.shape
    mem_len = mem.shape[1]
    x2 = x.reshape(batch * seq, d)
    mem2 = mem.reshape(batch * mem_len, d)
    wt3, idx3 = _input_weights(w_in)
    depth = ln_g.shape[0]
    for layer in range(depth):
        w = _layer_weights(layer, wt3, idx3, w_mem_kv, w_branch, w_out)
        x2 = _layer(layer, x2, mem2, batch, seq, mem_len, ln_g[layer], conv_w[layer],
                    conv_b[layer], mem_ln_g[layer], wt3, w,
                    final_g if layer == depth - 1 else None)
    return x2.reshape(batch, seq, d)
```

```python
import functools

import numpy as np
import jax
import jax.numpy as jnp
from jax import lax
from jax.experimental import pallas as pl
from jax.experimental.pallas import tpu as pltpu

EPS = 1e-6
BRANCH_W = 1024
MOBA_HEADS = 8
MOBA_BLOCK = 256
MOBA_TOPK = 3
DSA_HEADS = 8
DSA_TOPK_MAX = 256
IDX_HEADS = 4
IDX_DIM = 64
MEM_HEADS = 4
HEAD_DIM = 128
MEM_HEAD_DIM = 256
IDX_SLOT = 256
NEG = -1e30
ATT_UNROLL = 2
HEADS_PER_STEP = 4
BISECT_ITERS = 18
ONES_ROWS = 16
LANE = 128
VMEM_LIMIT = 56 * 1024 * 1024

F32 = jnp.float32
BF16 = jnp.bfloat16
_ONE = np.float32(1.0)
_ZERO = np.float32(0.0)
_NEG = np.float32(NEG)


def _cparams(sem):
    return pltpu.CompilerParams(dimension_semantics=sem, vmem_limit_bytes=VMEM_LIMIT)


def _dot(a, b):
    return jnp.dot(a, b, preferred_element_type=F32)


def _dot_nt(a, b):
    return lax.dot_general(a, b, (((1,), (1,)), ((), ())), preferred_element_type=F32)


def _split2(v):
    hi = v.astype(BF16)
    lo = (v - hi.astype(F32)).astype(BF16)
    return hi, lo


def _split3(v):
    h1 = v.astype(BF16)
    r1 = v - h1.astype(F32)
    h2 = r1.astype(BF16)
    h3 = (r1 - h2.astype(F32)).astype(BF16)
    return h1, h2, h3


def _ind(cond):
    return jnp.where(cond, _ONE, _ZERO)


def _silu(g):
    return g * jax.nn.sigmoid(g)


def _rms(x, g):
    var = jnp.mean(x * x, axis=-1, keepdims=True)
    return (x * lax.rsqrt(var + EPS)) * g


def _rmsnorm_kernel(x_ref, g_ref, o_ref):
    o_ref[...] = _rms(x_ref[...], g_ref[...]).astype(o_ref.dtype)


def _rmsnorm(x, g, out_dtype, tm=256):
    n, d = x.shape
    return pl.pallas_call(
        _rmsnorm_kernel,
        grid=(n // tm,),
        in_specs=[pl.BlockSpec((tm, d), lambda i: (i, 0)),
                  pl.BlockSpec((1, d), lambda i: (0, 0))],
        out_specs=pl.BlockSpec((tm, d), lambda i: (i, 0)),
        out_shape=jax.ShapeDtypeStruct((n, d), out_dtype),
        compiler_params=_cparams(("parallel",)),
        name="rmsnorm",
    )(x, g.reshape(1, d))


def _prep_kernel(x_ref, g_ref, whi_ref, wlo_ref, phi_ref, plo_ref, xn_ref, cat_ref, wt_ref):
    xn = _rms(x_ref[...], g_ref[...])
    hi, lo = _split2(xn)
    xn_ref[...] = hi
    acc = _dot_nt(hi, whi_ref[...]) + _dot_nt(hi, wlo_ref[...]) + _dot_nt(lo, whi_ref[...])
    vh, vl = _split2(acc)
    cat_ref[...] = (_dot(vh, phi_ref[...]) + _dot(vl, plo_ref[...])).astype(BF16)
    gate_row = IDX_DIM
    wt_ref[...] = acc[:, IDX_W_IN - LANE:].T[gate_row:gate_row + 8, :]


IDX_W_IN = 384


def _idx_placement():
    n_out = (IDX_HEADS + 1) * IDX_SLOT
    p_hi = np.zeros((IDX_W_IN, n_out), np.float32)
    p_lo = np.zeros((IDX_W_IN, n_out), np.float32)
    e = np.arange(IDX_DIM)
    for j in range(IDX_HEADS):
        src, dst = j * IDX_DIM + e, j * IDX_SLOT + e
        p_hi[src, dst] = 1
        p_hi[src, dst + IDX_DIM] = 1
        p_lo[src, dst + 2 * IDX_DIM] = 1
    src, dst = IDX_HEADS * IDX_DIM + e, IDX_HEADS * IDX_SLOT + e
    p_hi[src, dst] = 1
    p_lo[src, dst + IDX_DIM] = 1
    p_hi[src, dst + 2 * IDX_DIM] = 1
    return jnp.asarray(p_hi, BF16), jnp.asarray(p_lo, BF16)


def _prep(x, g, whi, wlo, tm=512):
    n, d = x.shape
    p_hi, p_lo = _idx_placement()
    co = p_hi.shape[1]
    return pl.pallas_call(
        _prep_kernel,
        grid=(n // tm,),
        in_specs=[pl.BlockSpec((tm, d), lambda i: (i, 0)),
                  pl.BlockSpec((1, d), lambda i: (0, 0)),
                  pl.BlockSpec((IDX_W_IN, d), lambda i: (0, 0)),
                  pl.BlockSpec((IDX_W_IN, d), lambda i: (0, 0)),
                  pl.BlockSpec((IDX_W_IN, co), lambda i: (0, 0)),
                  pl.BlockSpec((IDX_W_IN, co), lambda i: (0, 0))],
        out_specs=[pl.BlockSpec((tm, d), lambda i: (i, 0)),
                   pl.BlockSpec((tm, co), lambda i: (i, 0)),
                   pl.BlockSpec((8, tm), lambda i: (0, i))],
        out_shape=[jax.ShapeDtypeStruct((n, d), BF16),
                   jax.ShapeDtypeStruct((n, co), BF16),
                   jax.ShapeDtypeStruct((8, n), F32)],
        compiler_params=_cparams(("parallel",)),
        name="prep",
    )(x, g.reshape(1, d), whi, wlo, p_hi, p_lo)


def _transpose_cast_kernel(w_ref, o_ref):
    o_ref[...] = w_ref[...].T.astype(o_ref.dtype)


def _transpose_cast(w, name, t=512):
    d, n = w.shape
    return pl.pallas_call(
        _transpose_cast_kernel,
        grid=(n // t, d // t),
        in_specs=[pl.BlockSpec((t, t), lambda i, j: (j, i))],
        out_specs=pl.BlockSpec((t, t), lambda i, j: (i, j)),
        out_shape=jax.ShapeDtypeStruct((n, d), BF16),
        compiler_params=_cparams(("parallel", "parallel")),
        name=name,
    )(w)


def _mm_nn_kernel(a_ref, b_ref, o_ref):
    o_ref[...] = _dot(a_ref[...], b_ref[...]).astype(o_ref.dtype)


def _mm_nn(a, b, out_dtype, tm, tn, name):
    m, k = a.shape
    n = b.shape[1]
    return pl.pallas_call(
        _mm_nn_kernel,
        grid=(m // tm, n // tn),
        in_specs=[pl.BlockSpec((tm, k), lambda i, j: (i, 0)),
                  pl.BlockSpec((k, tn), lambda i, j: (0, j))],
        out_specs=pl.BlockSpec((tm, tn), lambda i, j: (i, j)),
        out_shape=jax.ShapeDtypeStruct((m, n), out_dtype),
        compiler_params=_cparams(("parallel", "parallel")),
        name=name,
    )(a, b)


def _proj_tok_kernel(blk_ref, a_ref, b_ref, s_ref, o_ref):
    del blk_ref
    o_ref[...] = (_dot_nt(a_ref[...], b_ref[...]) * s_ref[...]).astype(o_ref.dtype)


def _proj_tok(a, wt3, layer, row_blocks, colscale, tm, tn):
    m, k = a.shape
    n = len(row_blocks) * tn
    return pl.pallas_call(
        _proj_tok_kernel,
        grid_spec=pltpu.PrefetchScalarGridSpec(
            num_scalar_prefetch=1,
            grid=(m // tm, n // tn),
            in_specs=[pl.BlockSpec((tm, k), lambda i, j, blk: (i, 0)),
                      pl.BlockSpec((None, tn, k), lambda i, j, blk: (layer, blk[j], 0)),
                      pl.BlockSpec((1, tn), lambda i, j, blk: (0, j))],
            out_specs=pl.BlockSpec((tm, tn), lambda i, j, blk: (i, j))),
        out_shape=jax.ShapeDtypeStruct((m, n), BF16),
        compiler_params=_cparams(("parallel", "parallel")),
        name="proj_tok",
    )(jnp.asarray(row_blocks, jnp.int32), a, wt3, colscale)


def _proj_vt_kernel(blk_ref, a_ref, b_ref, o_ref):
    del blk_ref
    o_ref[...] = _dot_nt(a_ref[...], b_ref[...]).astype(o_ref.dtype)


def _proj_vt(wt3, layer, row_blocks, b, tm, tn):
    n, k = b.shape
    m = len(row_blocks) * tm
    return pl.pallas_call(
        _proj_vt_kernel,
        grid_spec=pltpu.PrefetchScalarGridSpec(
            num_scalar_prefetch=1,
            grid=(n // tn, m // tm),
            in_specs=[pl.BlockSpec((None, tm, k), lambda j, i, blk: (layer, blk[i], 0)),
                      pl.BlockSpec((tn, k), lambda j, i, blk: (j, 0))],
            out_specs=pl.BlockSpec((tm, tn), lambda j, i, blk: (i, j))),
        out_shape=jax.ShapeDtypeStruct((m, n), BF16),
        compiler_params=_cparams(("parallel", "parallel")),
        name="proj_vt",
    )(jnp.asarray(row_blocks, jnp.int32), wt3, b)


def _mm_nt_kernel(a_ref, b_ref, o_ref):
    o_ref[...] = _dot_nt(a_ref[...], b_ref[...]).astype(o_ref.dtype)


def _mm_nt(a, b, out_dtype, tm, tn, name):
    m, k = a.shape
    n = b.shape[0]
    return pl.pallas_call(
        _mm_nt_kernel,
        grid=(n // tn, m // tm),
        in_specs=[pl.BlockSpec((tm, k), lambda j, i: (i, 0)),
                  pl.BlockSpec((tn, k), lambda j, i: (j, 0))],
        out_specs=pl.BlockSpec((tm, tn), lambda j, i: (i, j)),
        out_shape=jax.ShapeDtypeStruct((m, n), out_dtype),
        compiler_params=_cparams(("parallel", "parallel")),
        name=name,
    )(a, b)


def _out_proj_kernel(a_ref, b_ref, r_ref, g_ref, o_ref, *, final_norm):
    y = r_ref[...] + _dot(a_ref[...], b_ref[...])
    o_ref[...] = _rms(y, g_ref[...]) if final_norm else y


def _out_proj(a, b, res, final_g, tm):
    m, k = a.shape
    n = b.shape[1]
    g = jnp.ones((1, n), F32) if final_g is None else final_g.reshape(1, n)
    return pl.pallas_call(
        functools.partial(_out_proj_kernel, final_norm=final_g is not None),
        grid=(m // tm,),
        in_specs=[pl.BlockSpec((tm, k), lambda i: (i, 0)),
                  pl.BlockSpec((k, n), lambda i: (0, 0)),
                  pl.BlockSpec((tm, n), lambda i: (i, 0)),
                  pl.BlockSpec((1, n), lambda i: (0, 0))],
        out_specs=pl.BlockSpec((tm, n), lambda i: (i, 0)),
        out_shape=jax.ShapeDtypeStruct((m, n), F32),
        compiler_params=_cparams(("parallel",)),
        name="out_proj",
    )(a, b, res, g)


def _merge_kernel(ya_ref, yc_ref, ys_ref, ym_ref, ra_ref, rc_ref, rs_ref, rm_ref, wb_ref, o_ref):
    ys = (ya_ref, yc_ref, ys_ref, ym_ref)
    rs = (ra_ref, rc_ref, rs_ref, rm_ref)
    acc = None
    for br in range(4):
        z = _dot(ys[br][...], wb_ref[br])
        term = jax.nn.sigmoid(rs[br][...].astype(F32)) * z
        acc = term if acc is None else acc + term
    o_ref[...] = acc.astype(o_ref.dtype)


def _merge(ys, gates, wb, tm, tn):
    n, w = ys[0].shape
    d = wb.shape[2]
    y_specs = [pl.BlockSpec((tm, w), lambda j, i: (i, 0)) for _ in range(4)]
    r_specs = [pl.BlockSpec((tm, tn), functools.partial(
        lambda j, i, base: (i, base + j), base=br * d // tn)) for br in range(4)]
    return pl.pallas_call(
        _merge_kernel,
        grid=(d // tn, n // tm),
        in_specs=y_specs + r_specs + [pl.BlockSpec((4, w, tn), lambda j, i: (0, 0, j))],
        out_specs=pl.BlockSpec((tm, tn), lambda j, i: (i, j)),
        out_shape=jax.ShapeDtypeStruct((n, d), BF16),
        compiler_params=_cparams(("parallel", "parallel")),
        name="merge",
    )(*ys, gates, gates, gates, gates, wb)


def _ceil_div_unroll(n):
    return (n + (ATT_UNROLL - 1)) >> (ATT_UNROLL.bit_length() - 1)


def _two_pass_attention(q_list, k_ref, vt_ref, sc_ref, acc_ref, bias_fn, trips, blk, own=None):
    heads = range(len(q_list))
    hd = HEAD_DIM
    span = ATT_UNROLL * blk
    max_trips = k_ref.shape[0] // span
    tq = q_list[0].shape[0]

    def pass1(h, n_trips):
        m = jnp.full((1, tq), _NEG, F32)
        for j in range(n_trips):
            s = _dot_nt(k_ref[j * span:(j + 1) * span, h * hd:(h + 1) * hd], q_list[h])
            for u in range(ATT_UNROLL):
                n = j * ATT_UNROLL + u
                su = s[u * blk:(u + 1) * blk, :] + bias_fn(h, n)
                sc_ref[h, n * blk:(n + 1) * blk, :] = su
                m = jnp.maximum(m, jnp.max(su, axis=0, keepdims=True))
        if own is not None:
            off, keep, own_ref = own
            s = _dot_nt(k_ref[pl.ds(off, blk), h * hd:(h + 1) * hd], q_list[h])
            s = jnp.where(keep, s, _NEG)
            own_ref[h] = s
            m = jnp.maximum(m, jnp.max(s, axis=0, keepdims=True))
        return m

    def pv_and_sum(vt, p):
        ones = jnp.ones((ONES_ROWS, vt.shape[1]), BF16)
        return _dot(jnp.concatenate([vt, ones], axis=0), p.astype(BF16))

    def pass2(h, n_trips, m):
        acc = None
        for j in range(n_trips):
            p = jnp.exp2(sc_ref[h, j * span:(j + 1) * span, :] - m)
            pv = pv_and_sum(vt_ref[h * hd:(h + 1) * hd, j * span:(j + 1) * span], p)
            acc = pv if acc is None else acc + pv
        if own is not None:
            off, keep, own_ref = own
            p = jnp.exp2(own_ref[h] - m)
            pv = pv_and_sum(vt_ref[h * hd:(h + 1) * hd, pl.ds(off, blk)], p)
            acc = pv if acc is None else acc + pv
        if acc is None:
            acc = jnp.zeros((hd + ONES_ROWS, tq), F32)
        acc_ref[h] = acc[:hd]
        return acc[hd:hd + 1]

    def run(n_trips):
        ms, ls = [], []
        for h in heads:
            ms.append(pass1(h, n_trips))
            if h > 0:
                ls.append(pass2(h - 1, n_trips, ms[h - 1]))
        ls.append(pass2(len(q_list) - 1, n_trips, ms[-1]))
        return tuple(ls)

    return lax.switch(trips, [functools.partial(run, t) for t in range(max_trips + 1)])


def _moba_kernel(q_ref, k_ref, vt_ref, g_ref, o_ref, kmean_ref, selb_ref, sc_ref, acc_ref, own_ref,
                 *, nblk):
    i = pl.program_id(2)
    blk = MOBA_BLOCK
    hd = HEAD_DIM
    heads = range(HEADS_PER_STEP)

    @pl.when(i == 0)
    def _():
        for h in heads:
            kf = k_ref[:, h * hd:(h + 1) * hd].astype(F32).reshape(nblk, blk, hd)
            kmean_ref[h] = jnp.mean(kf, axis=1)

    q_list = []
    for h in heads:
        q = q_ref[:, h * hd:(h + 1) * hd]
        q_list.append(q)
        k1, k2, k3 = _split3(kmean_ref[h])
        bs = _dot_nt(k1, q) + _dot_nt(k2, q) + _dot_nt(k3, q)
        n_iota = lax.broadcasted_iota(jnp.int32, bs.shape, 0)
        past = n_iota < i
        left = jnp.where(past, bs, -jnp.inf)
        bias = jnp.full(bs.shape, _NEG, F32)
        for _ in range(MOBA_TOPK):
            top = jnp.max(left, axis=0, keepdims=True)
            first = jnp.min(jnp.where(left == top, n_iota, nblk), axis=0, keepdims=True)
            hit = n_iota == first
            bias = jnp.where(hit & past, _ZERO, bias)
            left = jnp.where(hit, -jnp.inf, left)
        selb_ref[h] = bias

    kpos = lax.broadcasted_iota(jnp.int32, (blk, blk), 0)
    qpos = lax.broadcasted_iota(jnp.int32, (blk, blk), 1)
    ls = _two_pass_attention(
        q_list, k_ref, vt_ref, sc_ref, acc_ref, lambda h, n: selb_ref[h, pl.ds(n, 1), :],
        _ceil_div_unroll(i), blk,
        own=(pl.multiple_of(i * blk, blk), kpos <= qpos, own_ref))
    for h in heads:
        o = (acc_ref[h] / ls[h]).T
        g = g_ref[:, h * hd:(h + 1) * hd].astype(F32)
        o_ref[:, h * hd:(h + 1) * hd] = (o * _silu(g)).astype(o_ref.dtype)


def _moba(act, act_t, batch, seq, q_col, k_col, g_col, v_row):
    nblk = seq // MOBA_BLOCK
    n = act.shape[0]
    blk = MOBA_BLOCK
    hp = HEADS_PER_STEP
    w = hp * HEAD_DIM
    q_cb, k_cb, g_cb, v_rb = q_col // w, k_col // w, g_col // w, v_row // w
    return pl.pallas_call(
        functools.partial(_moba_kernel, nblk=nblk),
        grid=(batch, MOBA_HEADS // hp, nblk),
        in_specs=[pl.BlockSpec((blk, w), lambda b, h, i: (b * nblk + i, q_cb + h)),
                  pl.BlockSpec((seq, w), lambda b, h, i: (b, k_cb + h)),
                  pl.BlockSpec((w, seq), lambda b, h, i: (v_rb + h, b)),
                  pl.BlockSpec((blk, w), lambda b, h, i: (b * nblk + i, g_cb + h))],
        out_specs=pl.BlockSpec((blk, w), lambda b, h, i: (b * nblk + i, h)),
        out_shape=jax.ShapeDtypeStruct((n, BRANCH_W), BF16),
        scratch_shapes=[pltpu.VMEM((hp, nblk, HEAD_DIM), F32),
                        pltpu.VMEM((hp, nblk, blk), F32),
                        pltpu.VMEM((hp, seq, blk), F32),
                        pltpu.VMEM((hp, HEAD_DIM, blk), F32),
                        pltpu.VMEM((hp, blk, blk), F32)],
        compiler_params=_cparams(("parallel", "parallel", "arbitrary")),
        name="moba",
    )(act, act, act_t, act)


def _conv_kernel(cb_ref, cc_ref, ch_ref, cg_ref, w_ref, bias_ref, o_ref):
    u = cc_ref[...].astype(F32) * ch_ref[...].astype(F32)
    t = lax.broadcasted_iota(jnp.int32, u.shape, 0)
    u1 = jnp.where(t >= 1, pltpu.roll(u, 1, 0), 0.0)
    u2 = jnp.where(t >= 2, pltpu.roll(u, 2, 0), 0.0)
    w = w_ref[...]
    conv = u2 * w[0:1, :] + u1 * w[1:2, :] + u * w[2:3, :] + bias_ref[...]
    o_ref[...] = (cb_ref[...].astype(F32) * conv * _silu(cg_ref[...].astype(F32))).astype(o_ref.dtype)


def _conv(act, conv_w, conv_b, batch, seq, cb_cb, cc_cb, ch_cb, cg_cb):
    n = act.shape[0]
    ncb = BRANCH_W // LANE
    kw = conv_w.shape[0]
    wpad = jnp.zeros((8, BRANCH_W), F32).at[:kw].set(conv_w)

    def spec(base):
        return pl.BlockSpec((seq, LANE), lambda b, c: (b, base + c))

    return pl.pallas_call(
        _conv_kernel,
        grid=(batch, ncb),
        in_specs=[spec(cb_cb), spec(cc_cb), spec(ch_cb), spec(cg_cb),
                  pl.BlockSpec((8, LANE), lambda b, c: (0, c)),
                  pl.BlockSpec((1, LANE), lambda b, c: (0, c))],
        out_specs=pl.BlockSpec((seq, LANE), lambda b, c: (b, c)),
        out_shape=jax.ShapeDtypeStruct((n, BRANCH_W), BF16),
        compiler_params=_cparams(("parallel", "parallel")),
        name="conv",
    )(act, act, act, act, wpad, conv_b.reshape(1, BRANCH_W))


def _dsa_kernel(qc_ref, kc_ref, wt_ref, q_ref, k_ref, vt_ref, g_ref, o_ref, st_ref, tri_ref, sc_ref,
                acc_ref, *, topk, tq):
    i = pl.program_id(1)
    nch = i + 1
    npair = (nch + 1) >> 1
    trips = _ceil_div_unroll(nch)
    idx_scale = (IDX_DIM ** -0.5) * (IDX_HEADS ** -0.5)
    kf = float(topk)
    ninf = np.float32(-np.inf)

    def chunk(c):
        return pl.ds(pl.multiple_of(c * tq, tq), tq)

    def pair(c):
        return pl.ds(pl.multiple_of(c * (2 * tq), 2 * tq), 2 * tq)

    def fold8(x, op):
        return op(x.reshape(x.shape[0] // 8, 8, tq), axis=0)

    @pl.when(pl.program_id(2) == 0)
    def _select():
        def score_body(c, carry):
            mx, mn = carry
            for u in range(2):
                cc = 2 * c + u
                kc = kc_ref[chunk(cc), :]
                sc = jnp.zeros((tq, tq), F32)
                for j in range(IDX_HEADS):
                    lg = _dot_nt(kc, qc_ref[:, j * IDX_SLOT:(j + 1) * IDX_SLOT])
                    sc = sc + (wt_ref[j:j + 1, :] * idx_scale) * jnp.maximum(lg, 0.0)
                kpos = cc * tq + lax.broadcasted_iota(jnp.int32, sc.shape, 0)
                qpos = i * tq + lax.broadcasted_iota(jnp.int32, sc.shape, 1)
                causal = kpos <= qpos
                st_ref[chunk(cc), :] = jnp.where(causal, sc, ninf)
                mx = jnp.maximum(mx, fold8(jnp.where(causal, sc, ninf), jnp.max))
                mn = jnp.minimum(mn, fold8(jnp.where(causal, sc, -ninf), jnp.min))
            return mx, mn
        mx8, mn8 = lax.fori_loop(0, npair, score_body,
                                 (jnp.full((8, tq), ninf, F32), jnp.full((8, tq), -ninf, F32)))
        smax = jnp.max(mx8, axis=0, keepdims=True)
        smin = jnp.min(mn8, axis=0, keepdims=True)

        qrow = i * tq + lax.broadcasted_iota(jnp.int32, (1, tq), 1)
        short = qrow + 1 < topk
        lo0 = jnp.where(short, _ZERO, smin)
        hi0 = jnp.where(short, _ONE, smax + jnp.maximum(jnp.abs(smax), np.float32(1e-30))
                        * np.float32(2.0 ** -10))

        def count(pred):
            def body(c, acc):
                for u in range(2):
                    acc = acc + fold8(_ind(pred(st_ref[chunk(2 * c + u), :])), jnp.sum)
                return acc
            acc = lax.fori_loop(0, npair, body, jnp.zeros((8, tq), F32))
            return jnp.sum(acc, axis=0, keepdims=True)

        def bis_body(_, carry):
            lo, hi = carry
            mid = lo + (hi - lo) * np.float32(0.5)
            ge = count(lambda s: s >= mid) >= kf
            return jnp.where(ge, mid, lo), jnp.where(ge, hi, mid)

        _, hi_f0 = lax.fori_loop(0, BISECT_ITERS, bis_body, (lo0, hi0))

        def walk_cond(carry):
            return jnp.min(carry[3]) < 0.5

        def walk_body(carry):
            hi_f, thr, cnt, done = carry

            def mx_body(c, acc):
                s = st_ref[pair(c), :]
                return jnp.maximum(acc, fold8(jnp.where(s < hi_f, s, ninf), jnp.max))
            mx = lax.fori_loop(0, npair, mx_body, jnp.full((8, tq), ninf, F32))
            v = jnp.max(mx, axis=0, keepdims=True)
            c_v = count(lambda s: s >= v)
            is_done = done > 0.5
            return (jnp.where(is_done, hi_f, v), jnp.where(is_done, thr, v),
                    jnp.where(is_done, cnt, c_v), _ind(is_done | (c_v >= kf)))

        zero_row = jnp.zeros((1, tq), F32)
        all_rows = (2 * npair * tq).astype(F32)
        _, thr, n_ge, _ = lax.while_loop(
            walk_cond, walk_body,
            (hi_f0, jnp.full((1, tq), ninf, F32), zero_row + all_rows, _ind(short)))

        n_gt = count(lambda s: s > thr)
        need = kf - n_gt
        n_eq = n_ge - n_gt

        def bias_all_ties(c, _):
            st_ref[pair(c), :] = jnp.where(st_ref[pair(c), :] >= thr, _ZERO, _NEG)
            return 0

        def bias_some_ties(cc, carry):
            s = st_ref[chunk(cc), :]
            eq = s == thr
            eqf = _ind(eq)
            before = _dot(tri_ref[...], eqf.astype(BF16)) + carry
            kpos = cc * tq + lax.broadcasted_iota(jnp.int32, s.shape, 0)
            qpos = i * tq + lax.broadcasted_iota(jnp.int32, s.shape, 1)
            selected = ((s > thr) | (eq & (before < need))) & (kpos <= qpos)
            st_ref[chunk(cc), :] = jnp.where(selected, _ZERO, _NEG)
            return carry + jnp.sum(eqf, axis=0, keepdims=True)

        def all_ties():
            lax.fori_loop(0, npair, bias_all_ties, 0)

        def some_ties():
            r = lax.broadcasted_iota(jnp.int32, (tq, tq), 0)
            cidx = lax.broadcasted_iota(jnp.int32, (tq, tq), 1)
            tri_ref[...] = _ind(cidx < r).astype(BF16)
            lax.fori_loop(0, 2 * npair, bias_some_ties, zero_row)

        simple = jnp.logical_and(jnp.max(n_eq - need) <= 0.0, jnp.min(thr) > ninf)
        lax.cond(simple, all_ties, some_ties)

    hd = HEAD_DIM
    heads = range(HEADS_PER_STEP)
    q_list = [q_ref[:, h * hd:(h + 1) * hd] for h in heads]
    ls = _two_pass_attention(
        q_list, k_ref, vt_ref, sc_ref, acc_ref, lambda h, c: st_ref[c * tq:(c + 1) * tq, :], trips, tq)
    for h in heads:
        o = (acc_ref[h] / ls[h]).T
        g = g_ref[:, h * hd:(h + 1) * hd].astype(F32)
        o_ref[:, h * hd:(h + 1) * hd] = (o * _silu(g)).astype(o_ref.dtype)


def _dsa(act, act_t, cat, wt, batch, seq, q_col, k_col, g_col, v_row, tq=256):
    assert ATT_UNROLL == 2
    n = act.shape[0]
    nq = seq // tq
    topk = min(DSA_TOPK_MAX, seq // 4)
    n_qcat = IDX_HEADS * IDX_SLOT
    hp = HEADS_PER_STEP
    w = hp * HEAD_DIM
    q_cb, k_cb, g_cb, v_rb = q_col // w, k_col // w, g_col // w, v_row // w
    return pl.pallas_call(
        functools.partial(_dsa_kernel, topk=topk, tq=tq),
        grid=(batch, nq, DSA_HEADS // hp),
        in_specs=[pl.BlockSpec((tq, n_qcat), lambda b, i, h: (b * nq + i, 0)),
                  pl.BlockSpec((seq, IDX_SLOT), lambda b, i, h: (b, n_qcat // IDX_SLOT)),
                  pl.BlockSpec((8, tq), lambda b, i, h: (0, b * nq + i)),
                  pl.BlockSpec((tq, w), lambda b, i, h: (b * nq + i, q_cb + h)),
                  pl.BlockSpec((seq, w), lambda b, i, h: (b, k_cb + h)),
                  pl.BlockSpec((w, seq), lambda b, i, h: (v_rb + h, b)),
                  pl.BlockSpec((tq, w), lambda b, i, h: (b * nq + i, g_cb + h))],
        out_specs=pl.BlockSpec((tq, w), lambda b, i, h: (b * nq + i, h)),
        out_shape=jax.ShapeDtypeStruct((n, BRANCH_W), BF16),
        scratch_shapes=[pltpu.VMEM((seq, tq), F32),
                        pltpu.VMEM((tq, tq), BF16),
                        pltpu.VMEM((hp, seq, tq), F32),
                        pltpu.VMEM((hp, HEAD_DIM, tq), F32)],
        compiler_params=_cparams(("parallel", "arbitrary", "arbitrary")),
        name="dsa",
    )(cat, cat, wt, act, act, act_t, act)


def _mem_kernel(q_ref, mk_ref, mvt_ref, g_ref, o_ref):
    hd = MEM_HEAD_DIM
    for h in range(MEM_HEADS):
        cols = slice(h * hd, (h + 1) * hd)
        s = _dot_nt(mk_ref[:, cols], q_ref[:, cols])
        m = jnp.max(s, axis=0, keepdims=True)
        p = jnp.exp2(s - m)
        l = jnp.sum(p, axis=0, keepdims=True)
        o = (_dot(mvt_ref[cols, :], p.astype(BF16)) / l).T
        o_ref[:, cols] = (o * _silu(g_ref[:, cols].astype(F32))).astype(o_ref.dtype)


def _mem_attn(act, mk, mvt, batch, seq, mem_len, q_col, g_col, tq=512):
    n = act.shape[0]
    nq = seq // tq
    w = BRANCH_W
    return pl.pallas_call(
        _mem_kernel,
        grid=(batch, nq),
        in_specs=[pl.BlockSpec((tq, w), lambda b, i: (b * nq + i, q_col // w)),
                  pl.BlockSpec((mem_len, w), lambda b, i: (b, 0)),
                  pl.BlockSpec((w, mem_len), lambda b, i: (0, b)),
                  pl.BlockSpec((tq, w), lambda b, i: (b * nq + i, g_col // w))],
        out_specs=pl.BlockSpec((tq, w), lambda b, i: (b * nq + i, 0)),
        out_shape=jax.ShapeDtypeStruct((n, BRANCH_W), BF16),
        compiler_params=_cparams(("parallel", "parallel")),
        name="mem_attn",
    )(act, mk, mvt, act)


O_IDX = 14 * BRANCH_W
O_MERGE = O_IDX + IDX_HEADS * IDX_DIM + IDX_DIM + IDX_HEADS


def _input_weights(w_in3):
    wt3 = jnp.swapaxes(w_in3, 1, 2).astype(BF16)
    idx3 = jnp.swapaxes(w_in3[:, :, O_IDX:O_MERGE], 1, 2)
    idx3 = jnp.pad(idx3, ((0, 0), (0, IDX_W_IN - idx3.shape[1]), (0, 0)))
    return wt3, idx3


def _layer_weights(layer, wt3, idx3, w_mem_kv3, w_branch3, w_out3):
    bw = BRANCH_W
    w_idx_hi, w_idx_lo = _split2(idx3[layer])
    return dict(w_merge=wt3[layer, O_MERGE:], w_idx_hi=w_idx_hi, w_idx_lo=w_idx_lo,
                w_mk=w_mem_kv3[layer][:, :bw].astype(BF16),
                w_mvt=_transpose_cast(w_mem_kv3[layer][:, bw:], "w_mvt"),
                w_branch=w_branch3[layer].astype(BF16), w_out=w_out3[layer].astype(BF16))


def _layer(layer, x2, mem2, batch, seq, mem_len, ln_g, conv_w, conv_b, mem_ln_g, wt3, w, final_g):
    nb = BRANCH_W // LANE
    bw = BRANCH_W
    (A_Q, A_K, A_G, C_B, C_C, C_H, C_G, S_Q, S_K, S_G, M_Q, M_G) = range(12)
    tok_groups = (0, 1, 3, 4, 5, 6, 7, 8, 9, 11, 12, 13)
    val_groups = (2, 10)

    log2e = float(np.log2(np.e))
    colscale = np.ones((1, len(tok_groups) * bw), np.float32)
    for col, hdim in ((A_Q, HEAD_DIM), (S_Q, HEAD_DIM), (M_Q, MEM_HEAD_DIM)):
        colscale[:, col * bw:(col + 1) * bw] = hdim ** -0.5 * log2e

    xn, cat, wt = _prep(x2, ln_g, w["w_idx_hi"], w["w_idx_lo"])
    act = _proj_tok(xn, wt3, layer, tok_groups, jnp.asarray(colscale), tm=2048, tn=bw)
    act_r = _mm_nt(xn, w["w_merge"], BF16, tm=2048, tn=1024, name="proj_merge")
    act_t = _proj_vt(wt3, layer, val_groups, xn, tm=bw, tn=2048)

    y_a = _moba(act, act_t, batch, seq, A_Q * bw, A_K * bw, A_G * bw, 0)
    y_c = _conv(act, conv_w, conv_b, batch, seq, C_B * nb, C_C * nb, C_H * nb, C_G * nb)
    y_s = _dsa(act, act_t, cat, wt, batch, seq, S_Q * bw, S_K * bw, S_G * bw, bw)

    mem_n = _rmsnorm(mem2, mem_ln_g, BF16)
    mk = _mm_nn(mem_n, w["w_mk"], BF16, tm=mem2.shape[0], tn=512, name="mem_k")
    mvt = _mm_nt(w["w_mvt"], mem_n, BF16, tm=512, tn=mem2.shape[0], name="mem_vt")
    y_m = _mem_attn(act, mk, mvt, batch, seq, mem_len, M_Q * bw, M_G * bw)

    merged = _merge((y_a, y_c, y_s, y_m), act_r, w["w_branch"], tm=512, tn=1024)
    return _out_proj(merged, w["w_out"], x2, final_g, tm=512)


def kernel(x, mem, ln_g, w_in, conv_w, conv_b, mem_ln_g, w_mem_kv, w_branch, w_out, final_g):
    batch, seq, d = x.shape
    mem_len = mem.shape[1]
    x2 = x.reshape(batch * seq, d)
    mem2 = mem.reshape(batch * mem_len, d)
    wt3, idx3 = _input_weights(w_in)
    depth = ln_g.shape[0]
    for layer in range(depth):
        w = _layer_weights(layer, wt3, idx3, w_mem_kv, w_branch, w_out)
        x2 = _layer(layer, x2, mem2, batch, seq, mem_len, ln_g[layer], conv_w[layer],
                    conv_b[layer], mem_ln_g[layer], wt3, w,
                    final_g if layer == depth - 1 else None)
    return x2.reshape(batch, seq, d)
```
